```python
import math
import jax
import jax.numpy as jnp
from jax import lax
import numpy as np

D_MODEL = 1024
BATCH = 4
SEQ = 4096
DEPTH = 2

GRID_W = 64
CTX_LEN = 256
HEAD_DIM = 64
N_BRANCH = 4
BRANCH_WIDTH = D_MODEL // N_BRANCH
A_HEADS = BRANCH_WIDTH // HEAD_DIM
B_HEADS = BRANCH_WIDTH // HEAD_DIM
B_GROUPS = 2
B_STATE = 128
B_CONV = 5
C_HEADS = BRANCH_WIDTH // HEAD_DIM
NA_WIN_R = 8
NA_WIN_C = 16
NA_QBLK = 16
NA_KSPAN = 32
D_HEADS = BRANCH_WIDTH // HEAD_DIM
D_KV_HEADS = 2
D_KV_WIDTH = D_KV_HEADS * HEAD_DIM
Q_BLOCK = 128
ROPE_BASE = 10000.0
SCAN_CHUNK = 64
LN_EPS = 1e-6
FFN_HIDDEN = -(-8 * D_MODEL // (3 * 256)) * 256
SSD_CONV_CH = BRANCH_WIDTH + 2 * B_GROUPS * B_STATE
IN_SPLITS = ((BRANCH_WIDTH,) * 5
             + (BRANCH_WIDTH, SSD_CONV_CH, B_HEADS, B_HEADS)
             + (BRANCH_WIDTH,) * 3
             + (BRANCH_WIDTH, D_KV_WIDTH, D_KV_WIDTH)
             + (N_BRANCH * D_MODEL,))
IN_COLS = sum(IN_SPLITS)
IN_OFFSETS = [int(o) for o in np.cumsum(IN_SPLITS)[:-1]]
DEEPNORM_ALPHA = (2.0 * DEPTH) ** 0.25
DEEPNORM_BETA = (8.0 * DEPTH) ** -0.25

kernel_name = "hybrid_gated_dit_block"


def layer_norm(x):
    xf = x.astype(jnp.float32)
    mu = jnp.mean(xf, -1, keepdims=True)
    var = jnp.mean(jnp.square(xf - mu), -1, keepdims=True)
    return ((xf - mu) * lax.rsqrt(var + LN_EPS)).astype(x.dtype)


def rms_norm(x, w):
    xf = x.astype(jnp.float32)
    return (xf * lax.rsqrt(jnp.mean(xf * xf, -1, keepdims=True) + LN_EPS)).astype(x.dtype) * w


def modulate(x, shift, scale):
    return layer_norm(x) * (1.0 + scale) + shift


def post_norm(x, y, g, b):
    return layer_norm(DEEPNORM_ALPHA * x + y) * g + b


def split_heads(a, n):
    return a.reshape(a.shape[0], a.shape[1], n, -1)


def head_major(a, n):
    return split_heads(a, n).transpose(0, 2, 1, 3)


def axial_rope_angles(T):
    t = jnp.arange(T)
    nf = HEAD_DIM // 4
    inv_freq = ROPE_BASE ** (-jnp.arange(nf, dtype=jnp.float32) / nf)
    row = (t // GRID_W).astype(jnp.float32)[:, None] * inv_freq
    col = (t % GRID_W).astype(jnp.float32)[:, None] * inv_freq
    return row, col


def rope_half(x, ang):
    x1, x2 = jnp.split(x, 2, axis=-1)
    cos = jnp.cos(ang)[:, None, :].astype(x.dtype)
    sin = jnp.sin(ang)[:, None, :].astype(x.dtype)
    return jnp.concatenate([x1 * cos - x2 * sin, x2 * cos + x1 * sin], axis=-1)


def rope_2d(x, ang_r, ang_c):
    xr, xc = jnp.split(x, 2, axis=-1)
    return jnp.concatenate([rope_half(xr, ang_r), rope_half(xc, ang_c)], axis=-1)


def dwconv_centred(x, w, b):
    K = w.shape[0]
    y = lax.conv_general_dilated(x, w[:, None, :], window_strides=(1,), padding=[(K // 2, K // 2)],
                                 dimension_numbers=('NWC', 'WIO', 'NWC'), feature_group_count=x.shape[-1])
    return y + b


def chunked_scan(q, k, v, logf, s0):
    Bsz, H, T, _ = q.shape
    n = T // SCAN_CHUNK
    out_dtype = v.dtype

    def chunks(a):
        return a.astype(jnp.float32).reshape(Bsz, H, n, SCAN_CHUNK, a.shape[-1]).transpose(2, 0, 1, 3, 4)

    tri = jnp.tril(jnp.ones((SCAN_CHUNK, SCAN_CHUNK), dtype=bool))[:, :, None]
    scalar = logf.shape[-1] == 1

    def step(S, inp):
        qc, kc, vc, gc = inp
        b = jnp.cumsum(gc, axis=2)
        decay = jnp.exp(jnp.where(tri, b[:, :, :, None, :] - b[:, :, None, :, :], -jnp.inf))
        if scalar:
            scores = jnp.einsum('bhtk,bhsk->bhts', qc, kc) * decay[..., 0]
        else:
            scores = jnp.einsum('bhtk,bhsk,bhtsk->bhts', qc, kc, decay)
        o = jnp.einsum('bhts,bhsv->bhtv', scores, vc) + jnp.einsum('bhtk,bhkv->bhtv', qc * jnp.exp(b), S)
        b_end = b[:, :, -1:, :]
        S = S * jnp.exp(b_end[:, :, 0, :])[..., None] + jnp.einsum('bhsk,bhsv->bhkv', kc * jnp.exp(b_end - b), vc)
        return S, o

    S, o = lax.scan(step, s0, (chunks(q), chunks(k), chunks(v), chunks(logf)))
    return o.transpose(1, 2, 0, 3, 4).reshape(Bsz, H, T, v.shape[-1]).astype(out_dtype), S


def bidir_scan(q, dirs, qc, dirs_c):
    o_lat, o_ctx = [], []
    for rev, ((k, v, g), (kc, vc, gc)) in zip((False, True), zip(dirs, dirs_c)):
        fl = (lambda a: jnp.flip(a, axis=2)) if rev else (lambda a: a)
        s0 = jnp.zeros(q.shape[:2] + (k.shape[-1], v.shape[-1]), jnp.float32)
        oc, s_ctx = chunked_scan(fl(qc), fl(kc), fl(vc), fl(gc), s0)
        ol, _ = chunked_scan(fl(q), fl(k), fl(v), fl(g), s_ctx)
        o_lat.append(fl(ol))
        o_ctx.append(fl(oc))
    return o_lat[0] + o_lat[1], o_ctx[0] + o_ctx[1]


def hgrn_lower_bound(lb_param, l):
    sm = jax.nn.softmax(lb_param.astype(jnp.float32), axis=0)
    return jnp.cumsum(sm, axis=0)[l] - sm[0]


def hgrn2_inputs(q, f_fwd, f_bwd, v, lb_fwd, lb_bwd):
    dirs = []
    for f, lb in ((f_fwd, lb_fwd), (f_bwd, lb_bwd)):
        fg = lb + (1.0 - lb) * jax.nn.sigmoid(f.astype(jnp.float32))
        dirs.append((head_major(1.0 - fg, A_HEADS), head_major(v, A_HEADS), head_major(jnp.log(fg), A_HEADS)))
    return head_major(q, A_HEADS), dirs


def hgrn2_out(o, g, w):
    o = rms_norm(o.transpose(0, 2, 1, 3), w.reshape(A_HEADS, HEAD_DIM))
    return o.reshape(o.shape[0], o.shape[1], BRANCH_WIDTH) * jax.nn.silu(g)


def ssd_inputs(xbc, dt_fwd, dt_bwd, conv_w, conv_b, dt_bias, a_log):
    xbc = jax.nn.silu(dwconv_centred(xbc, conv_w, conv_b))
    xs, bm, cm = jnp.split(xbc, [BRANCH_WIDTH, BRANCH_WIDTH + B_GROUPS * B_STATE], axis=-1)
    rep = B_HEADS // B_GROUPS

    def group_heads(a):
        return jnp.repeat(split_heads(a, B_GROUPS), rep, axis=2).transpose(0, 2, 1, 3)

    xh = split_heads(xs, B_HEADS)
    k = group_heads(bm)
    dirs = []
    for dt_raw, bias, alog in ((dt_fwd, dt_bias[0], a_log[0]), (dt_bwd, dt_bias[1], a_log[1])):
        dt = jax.nn.softplus(dt_raw + bias)
        v = (xh * dt[..., None]).transpose(0, 2, 1, 3)
        g = (dt * -jnp.exp(alog)).transpose(0, 2, 1)[..., None]
        dirs.append((k, v, g))
    return group_heads(cm), dirs, xh


def ssd_out(y, xh, z, d_skip, w):
    y = y.transpose(0, 2, 1, 3) + xh * d_skip[:, None]
    y = y.reshape(xh.shape[0], xh.shape[1], BRANCH_WIDTH) * jax.nn.silu(z)
    return rms_norm(y, w)


def attend_dense(q, k, v):
    Bsz, S, Hq, d = q.shape
    Hkv = k.shape[2]
    qg = q.reshape(Bsz, S, Hkv, Hq // Hkv, d)
    s = jnp.einsum('bqkgd,bskd->bkgqs', qg, k).astype(jnp.float32) * d ** -0.5
    p = jax.nn.softmax(s, axis=-1).astype(v.dtype)
    return jnp.einsum('bkgqs,bskd->bqkgd', p, v).reshape(Bsz, S, Hq * d)


def na_latent(q, k, v, kc, vc, rpb):
    Bsz, T, H, d = q.shape
    rows = T // GRID_W
    wr = min(NA_WIN_R, rows)
    ncb = GRID_W // NA_QBLK
    r = np.arange(rows)
    row_idx = np.clip(r - wr // 2, 0, rows - wr)[:, None] + np.arange(wr)
    qcol = np.arange(GRID_W).reshape(ncb, NA_QBLK)
    kcol = np.clip(qcol[:, :1] - NA_WIN_C // 2, 0, GRID_W - NA_KSPAN) + np.arange(NA_KSPAN)
    win0 = np.clip(qcol - NA_WIN_C // 2, 0, GRID_W - NA_WIN_C)[:, :, None]
    in_win = (kcol[:, None, :] >= win0) & (kcol[:, None, :] < win0 + NA_WIN_C)
    dcol = np.clip(kcol[:, None, :] - qcol[:, :, None] + NA_WIN_C - 1, 0, 2 * NA_WIN_C - 2)
    drow = row_idx - r[:, None] + NA_WIN_R - 1
    bias = rpb[:, drow[:, None, None, :, None], dcol[None, :, :, None, :]]
    bias = jnp.where(in_win[None, None, :, :, None, :], bias, -jnp.inf)
    ridx = row_idx[:, :, None, None]
    cidx = kcol[None, None]
    kg = k.reshape(Bsz, rows, GRID_W, H, d)[:, ridx, cidx]
    vg = v.reshape(Bsz, rows, GRID_W, H, d)[:, ridx, cidx]
    qg = q.reshape(Bsz, rows, ncb, NA_QBLK, H, d)
    scale = d ** -0.5
    s_win = jnp.einsum('brjqhd,brwjkhd->bhrjqwk', qg, kg).astype(jnp.float32) * scale + bias
    s_ctx = jnp.einsum('brjqhd,bchd->bhrjqc', qg, kc).astype(jnp.float32) * scale
    nw = wr * NA_KSPAN
    p = jax.nn.softmax(jnp.concatenate([s_win.reshape(s_win.shape[:5] + (nw,)), s_ctx], axis=-1), axis=-1).astype(v.dtype)
    o = (jnp.einsum('bhrjqwk,brwjkhd->brjqhd', p[..., :nw].reshape(s_win.shape), vg)
         + jnp.einsum('bhrjqc,bchd->brjqhd', p[..., nw:], vc))
    return o.reshape(Bsz, T, H * d)


def gqa_latent(q, k, v, kc, vc):
    Bsz, T, Hq, d = q.shape
    Hkv = k.shape[2]
    k_all = jnp.concatenate([k, kc], axis=1)
    v_all = jnp.concatenate([v, vc], axis=1)
    qb = q.reshape(Bsz, T // Q_BLOCK, Q_BLOCK, Hkv, Hq // Hkv, d).transpose(1, 0, 2, 3, 4, 5)

    def block(qi):
        s = jnp.einsum('bqkgd,bskd->bkgqs', qi, k_all).astype(jnp.float32) * d ** -0.5
        p = jax.nn.softmax(s, axis=-1).astype(v_all.dtype)
        return jnp.einsum('bkgqs,bskd->bqkgd', p, v_all)

    o = lax.map(block, qb)
    return o.transpose(1, 0, 2, 3, 4, 5).reshape(Bsz, T, Hq * d)


def gated_merge(branches, gate_logits, w_branch, w_out):
    br = jnp.stack(branches, axis=2)
    proj = jnp.einsum('btnw,nwd->btnd', br, w_branch)
    g = jax.nn.sigmoid(gate_logits.reshape(proj.shape))
    return jnp.sum(g * proj, axis=2) @ w_out


def token_mixers(h, hc, w_in, lb_f, lb_b, hgrn_norm, conv_w, conv_b, dt_bias, a_log, d_skip, ssd_norm,
                 rpb, q_norm, k_norm, w_branch, w_out, ang_r, ang_c, ctx_out):
    (a_q, a_ff, a_fb, a_v, a_g, b_z, b_xbc, b_dtf, b_dtb, c_q, c_k, c_v, d_q, d_k, d_v, gate) = jnp.split(h @ w_in, IN_OFFSETS, axis=-1)
    (ac_q, ac_ff, ac_fb, ac_v, ac_g, bc_z, bc_xbc, bc_dtf, bc_dtb, cc_q, cc_k, cc_v, dc_q, dc_k, dc_v, gate_c) = jnp.split(hc @ w_in, IN_OFFSETS, axis=-1)
    qa, dirs_a = hgrn2_inputs(a_q, a_ff, a_fb, a_v, lb_f, lb_b)
    qac, dirs_ac = hgrn2_inputs(ac_q, ac_ff, ac_fb, ac_v, lb_f, lb_b)
    oa, oac = bidir_scan(qa, dirs_a, qac, dirs_ac)
    qb, dirs_b, xb = ssd_inputs(b_xbc, b_dtf, b_dtb, conv_w, conv_b, dt_bias, a_log)
    qbc, dirs_bc, xbc_ = ssd_inputs(bc_xbc, bc_dtf, bc_dtb, conv_w, conv_b, dt_bias, a_log)
    ob, obc = bidir_scan(qb, dirs_b, qbc, dirs_bc)
    kc_na, vc_na = split_heads(cc_k, C_HEADS), split_heads(cc_v, C_HEADS)
    y_c = na_latent(split_heads(c_q, C_HEADS), split_heads(c_k, C_HEADS), split_heads(c_v, C_HEADS), kc_na, vc_na, rpb)
    kc_g, vc_g = rms_norm(split_heads(dc_k, D_KV_HEADS), k_norm), split_heads(dc_v, D_KV_HEADS)
    q_g = rope_2d(rms_norm(split_heads(d_q, D_HEADS), q_norm), ang_r, ang_c)
    k_g = rope_2d(rms_norm(split_heads(d_k, D_KV_HEADS), k_norm), ang_r, ang_c)
    y_d = gqa_latent(q_g, k_g, split_heads(d_v, D_KV_HEADS), kc_g, vc_g)
    y = gated_merge([hgrn2_out(oa, a_g, hgrn_norm), ssd_out(ob, xb, b_z, d_skip, ssd_norm), y_c, y_d], gate, w_branch, w_out)
    if not ctx_out:
        return y, None
    yc_c = attend_dense(split_heads(cc_q, C_HEADS), kc_na, vc_na)
    yc_d = attend_dense(rms_norm(split_heads(dc_q, D_HEADS), q_norm), kc_g, vc_g)
    yc = gated_merge([hgrn2_out(oac, ac_g, hgrn_norm), ssd_out(obc, xbc_, bc_z, d_skip, ssd_norm), yc_c, yc_d], gate_c, w_branch, w_out)
    return y, yc


def swiglu(h, w_up, w_down):
    g, u = jnp.split(h @ w_up, 2, axis=-1)
    return (jax.nn.silu(g) * u) @ w_down


def setup_inputs(seed: int = 0) -> dict:
    key = jax.random.key(seed)
    ks = jax.random.split(key, 26)
    L = DEPTH

    def nrm(k, shape, scale):
        return jax.random.normal(k, shape, jnp.float32) * scale

    def gain(k, shape):
        return 1.0 + nrm(k, shape, 0.02)

    dt = jnp.exp(jax.random.uniform(ks[12], (L, 2, B_HEADS), jnp.float32, math.log(1e-3), math.log(1e-1)))
    return {
        "x": nrm(ks[0], (BATCH, SEQ, D_MODEL), 1.0),
        "c": nrm(ks[1], (BATCH, D_MODEL), 1.0),
        "ctx": nrm(ks[2], (BATCH, CTX_LEN, D_MODEL), 1.0),
        "c_ctx": nrm(ks[3], (D_MODEL,), 1.0),
        "ada_w": nrm(ks[4], (L, D_MODEL, 6 * D_MODEL), 0.5 * D_MODEL ** -0.5),
        "ada_b": nrm(ks[5], (L, 6 * D_MODEL), 0.02),
        "w_in": nrm(ks[6], (L, D_MODEL, IN_COLS), D_MODEL ** -0.5),
        "hgrn_lb": nrm(ks[7], (2, L, BRANCH_WIDTH), 0.5),
        "hgrn_norm": gain(ks[8], (L, BRANCH_WIDTH)),
        "ssd_conv_w": nrm(ks[9], (L, B_CONV, SSD_CONV_CH), B_CONV ** -0.5),
        "ssd_conv_b": nrm(ks[10], (L, SSD_CONV_CH), 0.02),
        "ssd_dt_bias": dt + jnp.log(-jnp.expm1(-dt)),
        "ssd_a_log": jnp.log(jax.random.uniform(ks[11], (L, 2, B_HEADS), jnp.float32, 1.0, 16.0)),
        "ssd_d": gain(ks[13], (L, B_HEADS)),
        "ssd_norm": gain(ks[14], (L, BRANCH_WIDTH)),
        "na_rpb": nrm(ks[15], (L, C_HEADS, 2 * NA_WIN_R - 1, 2 * NA_WIN_C - 1), 0.1),
        "q_norm": gain(ks[16], (L, HEAD_DIM)),
        "k_norm": gain(ks[17], (L, HEAD_DIM)),
        "w_branch": nrm(ks[18], (L, N_BRANCH, BRANCH_WIDTH, D_MODEL), BRANCH_WIDTH ** -0.5),
        "w_out": nrm(ks[19], (L, D_MODEL, D_MODEL), DEEPNORM_BETA * D_MODEL ** -0.5),
        "ln1_g": gain(ks[20], (L, D_MODEL)),
        "ln1_b": nrm(ks[21], (L, D_MODEL), 0.02),
        "ffn_w_up": nrm(ks[22], (L, D_MODEL, 2 * FFN_HIDDEN), D_MODEL ** -0.5),
        "ffn_w_down": nrm(ks[23], (L, FFN_HIDDEN, D_MODEL), DEEPNORM_BETA * FFN_HIDDEN ** -0.5),
        "ln2_g": gain(ks[24], (L, D_MODEL)),
        "ln2_b": nrm(ks[25], (L, D_MODEL), 0.02),
    }


def reference(x, c, ctx, c_ctx, ada_w, ada_b, w_in, hgrn_lb, hgrn_norm, ssd_conv_w, ssd_conv_b, ssd_dt_bias,
              ssd_a_log, ssd_d, ssd_norm, na_rpb, q_norm, k_norm, w_branch, w_out, ln1_g, ln1_b,
              ffn_w_up, ffn_w_down, ln2_g, ln2_b):
    ang_r, ang_c = axial_rope_angles(x.shape[1])
    xc = ctx
    for l in range(DEPTH):
        last = l == DEPTH - 1
        mod = jnp.split(jax.nn.silu(c) @ ada_w[l] + ada_b[l], 6, axis=-1)
        shift1, scale1, gate1, shift2, scale2, gate2 = [m[:, None, :] for m in mod]
        shift1c, scale1c, gate1c, shift2c, scale2c, gate2c = jnp.split(jax.nn.silu(c_ctx) @ ada_w[l] + ada_b[l], 6, axis=-1)
        h = modulate(x, shift1, scale1)
        hc = modulate(xc, shift1c, scale1c)
        y, yc = token_mixers(h, hc, w_in[l], hgrn_lower_bound(hgrn_lb[0], l), hgrn_lower_bound(hgrn_lb[1], l),
                             hgrn_norm[l], ssd_conv_w[l], ssd_conv_b[l], ssd_dt_bias[l], ssd_a_log[l], ssd_d[l],
                             ssd_norm[l], na_rpb[l], q_norm[l], k_norm[l], w_branch[l], w_out[l], ang_r, ang_c,
                             not last)
        x = post_norm(x, gate1 * y, ln1_g[l], ln1_b[l])
        x = post_norm(x, gate2 * swiglu(modulate(x, shift2, scale2), ffn_w_up[l], ffn_w_down[l]), ln2_g[l], ln2_b[l])
        if not last:
            xc = post_norm(xc, gate1c * yc, ln1_g[l], ln1_b[l])
            xc = post_norm(xc, gate2c * swiglu(modulate(xc, shift2c, scale2c), ffn_w_up[l], ffn_w_down[l]), ln2_g[l], ln2_b[l])
    return x
```

```python
import functools
import math

import jax
import jax.numpy as jnp
import numpy as np
from jax import lax
from jax.experimental import pallas as pl
from jax.experimental.pallas import tpu as pltpu

F32 = jnp.float32
BF16 = jnp.bfloat16
HIGHEST = lax.Precision.HIGHEST

D_MODEL = 1024
DEPTH = 2
GRID_W = 64
N_CTX = 256
N_LAT = 4096
SEQ_ALL = N_CTX + N_LAT
HEAD_DIM = 64
BRANCH_W = 256
N_HEADS = 4
SSD_STATE = 128
SSD_GROUPS = 2
SSD_CONV_K = 5
SSD_CONV_CH = BRANCH_W + 2 * SSD_GROUPS * SSD_STATE
KV_HEADS = 2
NA_WIN_R = 8
NA_WIN_C = 16
ROPE_BASE = 10000.0
LN_EPS = 1e-6
FFN_HIDDEN = 2816
ALPHA = (2.0 * DEPTH) ** 0.25

ROW_TILE = 256
N_TILES = SEQ_ALL // ROW_TILE
LAT_TILES = N_LAT // ROW_TILE
HGRN_CHUNK = 64
HGRN_SUB = 16
SSD_CHUNK = 128
NEG_BIG = -1e30
VMEM_LIMIT = 48 * 1024 * 1024

P_COLS = 8192
COL_GATE = 0
COL_DQ = 4096
COL_XBC = 4608
COL_Z = 5376
COL_AQ = 5632
COL_CQ = 6912
COL_DK = 7680
COL_DV = 7808
COL_DT = 7936


def _cparams(sem):
    return pltpu.CompilerParams(dimension_semantics=sem, vmem_limit_bytes=VMEM_LIMIT)


def _sigmoid(x):
    return 1.0 / (1.0 + jnp.exp(-x))


def _div(x, n):
    return lax.shift_right_logical(x, int(math.log2(n)))


def _mod(x, n):
    return x & (n - 1)


def _softplus(x):
    return jnp.maximum(x, 0.0) + jnp.log(1.0 + jnp.exp(-jnp.abs(x)))


def _layer_norm(x):
    mu = jnp.mean(x, axis=-1, keepdims=True)
    xc = x - mu
    var = jnp.mean(xc * xc, axis=-1, keepdims=True)
    return xc * lax.rsqrt(var + LN_EPS)


def _dot(a, b):
    return jnp.dot(a, b, preferred_element_type=F32)


def _dot_nt(a, b):
    return lax.dot_general(a, b, (((1,), (1,)), ((), ())), preferred_element_type=F32)


def _dot_hi(a, b):
    return jnp.dot(a, b, precision=HIGHEST, preferred_element_type=F32)


def _segment_mean(x, seg):
    hi = x.astype(BF16)
    lo = (x - hi.astype(F32)).astype(BF16)
    return _dot(hi, seg) + _dot(lo, seg)


def _ada_kernel(c_ref, w_ref, b_ref, o_ref):
    cv = c_ref[...]
    a = cv * _sigmoid(cv)
    o_ref[0] = _dot_hi(a, w_ref[0]) + b_ref[0]


def _ada_call(c8, ada_w, ada_b):
    n_l = ada_w.shape[0]
    return pl.pallas_call(
        _ada_kernel,
        grid=(n_l, 6),
        in_specs=[
            pl.BlockSpec((8, D_MODEL), lambda l, j: (0, 0)),
            pl.BlockSpec((1, D_MODEL, D_MODEL), lambda l, j: (l, 0, j)),
            pl.BlockSpec((1, 1, D_MODEL), lambda l, j: (l, 0, j)),
        ],
        out_specs=pl.BlockSpec((1, 8, D_MODEL), lambda l, j: (l, 0, j)),
        out_shape=jax.ShapeDtypeStruct((n_l, 8, 6 * D_MODEL), F32),
        compiler_params=_cparams(("arbitrary", "arbitrary")),
        name="ada_mod",
    )(c8, ada_w, ada_b.reshape(n_l, 1, 6 * D_MODEL))


def _mod_index(layer, which):
    def index(b, j):
        return (layer * 48 + jnp.where(j == 0, 4, b) * 6 + which, 0, 0)
    return index


def _mod_spec(layer, which):
    return pl.BlockSpec((1, 1, D_MODEL), _mod_index(layer, which))


def _modulate_kernel(x_ref, sh_ref, sc_ref, o_ref):
    hn = _layer_norm(x_ref[0])
    o_ref[0] = (hn * (1.0 + sc_ref[0]) + sh_ref[0]).astype(BF16)


def _modulate_call(xa, mods, layer):
    bsz = xa.shape[0]
    tile = pl.BlockSpec((1, ROW_TILE, D_MODEL), lambda b, j: (b, j, 0))
    return pl.pallas_call(
        _modulate_kernel,
        grid=(bsz, N_TILES),
        in_specs=[tile, _mod_spec(layer, 0), _mod_spec(layer, 1)],
        out_specs=tile,
        out_shape=jax.ShapeDtypeStruct(xa.shape, BF16),
        compiler_params=_cparams(("parallel", "parallel")),
        name="modulate",
    )(xa, mods, mods)


def _matmul_kernel(a_ref, w_ref, o_ref):
    o_ref[...] = _dot(a_ref[...], w_ref[...]).astype(o_ref.dtype)


def _matmul_call(a, w, tm, tn, out_dtype, name):
    m, k = a.shape
    n = w.shape[1]
    return pl.pallas_call(
        _matmul_kernel,
        grid=(n // tn, m // tm),
        in_specs=[
            pl.BlockSpec((tm, k), lambda j, i: (i, 0)),
            pl.BlockSpec((k, tn), lambda j, i: (0, j)),
        ],
        out_specs=pl.BlockSpec((tm, tn), lambda j, i: (i, j)),
        out_shape=jax.ShapeDtypeStruct((m, n), out_dtype),
        compiler_params=_cparams(("parallel", "parallel")),
        name=name,
    )(a, w)


CONV_PAD = 8


def _conv_kernel(x_ref, w_ref, b_ref, o_ref, xs_ref):
    zeros = jnp.zeros((CONV_PAD, 128), F32)
    xs_ref[0:CONV_PAD, :] = zeros
    xs_ref[CONV_PAD + SEQ_ALL:, :] = zeros
    xs_ref[CONV_PAD:CONV_PAD + SEQ_ALL, :] = x_ref[0].astype(F32)
    w = w_ref[...]
    row = lax.broadcasted_iota(jnp.int32, (ROW_TILE, 128), 0)
    half = SSD_CONV_K // 2
    for ci in range(N_TILES):
        t0 = ci * ROW_TILE
        acc = jnp.broadcast_to(b_ref[...], (ROW_TILE, 128))
        for k in range(SSD_CONV_K):
            j = k - half
            xk = xs_ref[CONV_PAD + t0 + j:CONV_PAD + t0 + j + ROW_TILE, :]
            if ci == 0 and j > 0:
                xk = jnp.where(row + j < ROW_TILE, xk, 0.0)
            if ci == 1 and j < 0:
                xk = jnp.where(row + j >= 0, xk, 0.0)
            acc = acc + xk * w[k:k + 1, :]
        o_ref[0, t0:t0 + ROW_TILE, :] = (acc * _sigmoid(acc)).astype(BF16)


def _conv_call(p3, conv_w, conv_b):
    bsz = p3.shape[0]
    n_ct = SSD_CONV_CH // 128
    return pl.pallas_call(
        _conv_kernel,
        grid=(bsz, n_ct),
        in_specs=[
            pl.BlockSpec((1, SEQ_ALL, 128), lambda b, c: (b, 0, COL_XBC // 128 + c)),
            pl.BlockSpec((SSD_CONV_K, 128), lambda b, c: (0, c)),
            pl.BlockSpec((1, 128), lambda b, c: (0, c)),
        ],
        out_specs=pl.BlockSpec((1, SEQ_ALL, 128), lambda b, c: (b, 0, c)),
        out_shape=jax.ShapeDtypeStruct((bsz, SEQ_ALL, SSD_CONV_CH), BF16),
        scratch_shapes=[pltpu.VMEM((SEQ_ALL + 2 * CONV_PAD, 128), F32)],
        compiler_params=_cparams(("parallel", "parallel")),
        name="ssd_conv",
    )(p3, conv_w, conv_b.reshape(1, SSD_CONV_CH))


def _fwd_tile(j):
    return j


def _bwd_tile(j):
    return jnp.where(j == 0, 0, N_TILES - j)


def _tri(n, upper):
    r = lax.broadcasted_iota(jnp.int32, (n, n), 0)
    c = lax.broadcasted_iota(jnp.int32, (n, n), 1)
    return (c >= r) if upper else (c <= r)


def _ssd_dir(rev, xa_ref, dtc_ref, dtr_ref, bias_r, alog_r, bias_c, alog_c, s_ref, o_ref):
    d = 1 if rev else 0
    dt_c = _softplus(dtc_ref[0].astype(F32) + bias_r)
    g_c = dt_c * (-jnp.exp(alog_r))
    dt_r = _softplus(dtr_ref[0] + bias_c)
    g_r = dt_r * (-jnp.exp(alog_c))
    n = SSD_CHUNK
    tril = _tri(n, False)
    triu = _tri(n, True)
    m_col = (triu if rev else tril).astype(F32)
    m_row = (tril if rev else triu).astype(F32)
    keep = triu if rev else tril
    lane = lax.broadcasted_iota(jnp.int32, (n, 128), 1)
    first = lane < HEAD_DIM
    n_chunks = ROW_TILE // n
    order = range(n_chunks - 1, -1, -1) if rev else range(n_chunks)
    for ci in order:
        c0 = ci * n
        cum_c = _dot_hi(m_col, g_c[c0:c0 + n, :])
        cum_r = _dot_hi(g_r[:, c0:c0 + n], m_row)
        end_row = 0 if rev else n - 1
        for g in range(SSD_GROUPS):
            bm = xa_ref[0, c0:c0 + n, BRANCH_W + g * SSD_STATE:BRANCH_W + (g + 1) * SSD_STATE]
            cm = xa_ref[0, c0:c0 + n, BRANCH_W + (SSD_GROUPS + g) * SSD_STATE:
                        BRANCH_W + (SSD_GROUPS + g + 1) * SSD_STATE]
            x2 = xa_ref[0, c0:c0 + n, g * 128:(g + 1) * 128].astype(F32)
            gmat = _dot_nt(cm, bm)
            col0 = 4 * d + 2 * g
            bc0 = cum_c[:, col0:col0 + 1]
            bc1 = cum_c[:, col0 + 1:col0 + 2]
            br0 = cum_r[col0:col0 + 1, :]
            br1 = cum_r[col0 + 1:col0 + 2, :]
            p0 = jnp.where(keep, jnp.exp(jnp.minimum(bc0 - br0, 0.0)), 0.0) * gmat
            p1 = jnp.where(keep, jnp.exp(jnp.minimum(bc1 - br1, 0.0)), 0.0) * gmat
            dt2 = jnp.where(first, dt_c[c0:c0 + n, col0:col0 + 1], dt_c[c0:c0 + n, col0 + 1:col0 + 2])
            v2 = x2 * dt2
            v2b = v2.astype(BF16)
            pv = jnp.where(first, _dot(p0.astype(BF16), v2b), _dot(p1.astype(BF16), v2b))
            bc2 = jnp.where(first, bc0, bc1)
            s_old = s_ref[g]
            qs = _dot(cm, s_old.astype(BF16))
            o_ref[0, c0:c0 + n, g * 128:(g + 1) * 128] = pv + jnp.exp(bc2) * qs
            e0 = cum_c[end_row:end_row + 1, col0:col0 + 1]
            e1 = cum_c[end_row:end_row + 1, col0 + 1:col0 + 2]
            end2 = jnp.where(first[0:1, :], e0, e1)
            wv = (v2 * jnp.exp(end2 - bc2)).astype(BF16)
            upd = lax.dot_general(bm, wv, (((0,), (0,)), ((), ())), preferred_element_type=F32)
            s_ref[g] = s_old * jnp.exp(end2) + upd


def _ssd_kernel(xa_f, xa_b, dtc_f, dtc_b, dtr_f, dtr_b, bias_r, alog_r, bias_c, alog_c,
                o_f, o_b, s_ref):
    @pl.when(pl.program_id(1) == 0)
    def _():
        s_ref[...] = jnp.zeros(s_ref.shape, F32)

    br, ar, bc, ac = bias_r[...], alog_r[...], bias_c[...], alog_c[...]
    _ssd_dir(False, xa_f, dtc_f, dtr_f, br, ar, bc, ac, s_ref.at[0], o_f)
    _ssd_dir(True, xa_b, dtc_b, dtr_b, br, ar, bc, ac, s_ref.at[1], o_b)


def _ssd_call(xact, p3, dt_rows, dt_bias, a_log):
    bsz = xact.shape[0]
    bias8 = dt_bias.reshape(8).astype(F32)
    alog8 = a_log.reshape(8).astype(F32)
    pad_row = lambda v: jnp.zeros((1, 128), F32).at[0, :8].set(v)
    col = lambda v: jnp.broadcast_to(v[:, None], (8, ROW_TILE))

    def spec_x(tile):
        return pl.BlockSpec((1, ROW_TILE, SSD_CONV_CH), lambda b, j: (b, tile(j), 0))

    def spec_dtc(tile):
        return pl.BlockSpec((1, ROW_TILE, 128), lambda b, j: (b, tile(j), COL_DT // 128))

    def spec_dtr(tile):
        return pl.BlockSpec((1, 8, ROW_TILE), lambda b, j: (b, 0, tile(j)))

    def spec_o(tile):
        return pl.BlockSpec((1, ROW_TILE, BRANCH_W), lambda b, j: (b, tile(j), 0))

    small_r = pl.BlockSpec((1, 128), lambda b, j: (0, 0))
    small_c = pl.BlockSpec((8, ROW_TILE), lambda b, j: (0, 0))
    out = jax.ShapeDtypeStruct((bsz, SEQ_ALL, BRANCH_W), F32)
    return pl.pallas_call(
        _ssd_kernel,
        grid=(bsz, N_TILES),
        in_specs=[spec_x(_fwd_tile), spec_x(_bwd_tile), spec_dtc(_fwd_tile), spec_dtc(_bwd_tile),
                  spec_dtr(_fwd_tile), spec_dtr(_bwd_tile), small_r, small_r, small_c, small_c],
        out_specs=[spec_o(_fwd_tile), spec_o(_bwd_tile)],
        out_shape=[out, out],
        scratch_shapes=[pltpu.VMEM((2, SSD_GROUPS, SSD_STATE, 128), F32)],
        compiler_params=_cparams(("parallel", "arbitrary")),
        name="ssd_scan",
    )(xact, xact, p3, p3, dt_rows, dt_rows, pad_row(bias8), pad_row(alog8), col(bias8), col(alog8))


def _hgrn_lower_bound(lb_ref, layer):
    p = lb_ref[...]
    e = jnp.exp(p - jnp.max(p, axis=0, keepdims=True))
    sm = e / jnp.sum(e, axis=0, keepdims=True)
    acc = sm[0:1, :]
    for i in range(1, layer + 1):
        acc = acc + sm[i:i + 1, :]
    return acc - sm[0:1, :]


def _tile_head(rolls, h, blk):
    pick = lambda i: rolls[(i - h) % N_HEADS]
    return jnp.where(blk == 0, pick(0), jnp.where(blk == 1, pick(1), jnp.where(blk == 2, pick(2), pick(3))))


def _lane_rolls(x):
    return [x] + [pltpu.roll(x, s * HEAD_DIM, axis=1) for s in range(1, N_HEADS)]


def _hgrn_chunk(rev, q, f, v, lb, st_ref, ones_seg):
    n, c = HGRN_CHUNK, HGRN_SUB
    fg = lb + (1.0 - lb) * _sigmoid(f)
    kk = 1.0 - fg
    lg = jnp.log(fg)
    r_i = lax.broadcasted_iota(jnp.int32, (n, n), 0)
    c_i = lax.broadcasted_iota(jnp.int32, (n, n), 1)
    tri = (c_i >= r_i) if rev else (c_i <= r_i)
    same = _div(r_i, c) == _div(c_i, c)
    cum = _dot_hi(tri.astype(F32), lg)
    cl = _dot_hi((tri & same).astype(F32), lg)
    row = lax.broadcasted_iota(jnp.int32, (n, BRANCH_W), 0)
    lane = lax.broadcasted_iota(jnp.int32, (n, BRANCH_W), 1)
    blk = _div(lane, HEAD_DIM)
    sub = _div(row, c)
    pos = _mod(row, c)

    o = jnp.zeros((n, BRANCH_W), F32)
    for dlt in range(c):
        if dlt == 0:
            ks, cs, vs = kk, cum, v
            ok = None
        else:
            sh = (n - dlt) if rev else dlt
            ks = pltpu.roll(kk, sh, axis=0)
            cs = pltpu.roll(cum, sh, axis=0)
            vs = pltpu.roll(v, sh, axis=0)
            ok = (pos + dlt < c) if rev else (pos >= dlt)
        e = q * ks * jnp.exp(jnp.minimum(cum - cs, 0.0))
        if ok is not None:
            e = jnp.where(ok, e, 0.0)
        o = o + _dot(e.astype(BF16), ones_seg) * vs

    qe = q * jnp.exp(cl)
    r_q, r_k, r_c = _lane_rolls(qe), _lane_rolls(kk), _lane_rolls(cum)
    blk1 = blk[0:1, :]
    for h in range(N_HEADS):
        lhs = jnp.where(sub == blk, _tile_head(r_q, h, blk), 0.0)
        kt = _tile_head(r_k, h, blk)
        ct = _tile_head(r_c, h, blk)
        if rev:
            ref_row = jnp.where(blk1 == 0, ct[c:c + 1, :], jnp.where(blk1 == 1, ct[2 * c:2 * c + 1, :],
                                                                    ct[3 * c:3 * c + 1, :]))
            valid = sub > blk
        else:
            ref_row = jnp.where(blk1 == 1, ct[c - 1:c, :], jnp.where(blk1 == 2, ct[2 * c - 1:2 * c, :],
                                                                    ct[3 * c - 1:3 * c, :]))
            valid = sub < blk
        rhs = jnp.where(valid, kt * jnp.exp(jnp.minimum(ref_row - ct, 0.0)), 0.0)
        sc = _dot_nt(lhs.astype(BF16), rhs.astype(BF16))
        vz = jnp.where(blk == h, v, 0.0)
        o = o + _dot(sc.astype(BF16), vz.astype(BF16))

    st = st_ref[...]
    o = o + _dot_nt((q * jnp.exp(cum)).astype(BF16), st.astype(BF16))
    end = cum[0:1, :] if rev else cum[n - 1:n, :]
    kp = kk * jnp.exp(end - cum)
    upd = _dot(v.T.astype(BF16), kp.astype(BF16))
    rb = _div(lax.broadcasted_iota(jnp.int32, (BRANCH_W, BRANCH_W), 0), HEAD_DIM)
    cb = _div(lax.broadcasted_iota(jnp.int32, (BRANCH_W, BRANCH_W), 1), HEAD_DIM)
    st_ref[...] = st * jnp.exp(end) + jnp.where(rb == cb, upd, 0.0)
    return o


def _hgrn_kernel(layer, q_f, f_f, v_f, q_b, f_b, v_b, lbf_ref, lbb_ref, o_f, o_b, st_ref):
    @pl.when(pl.program_id(1) == 0)
    def _():
        st_ref[...] = jnp.zeros(st_ref.shape, F32)

    lb_f = _hgrn_lower_bound(lbf_ref, layer)
    lb_b = _hgrn_lower_bound(lbb_ref, layer)
    rb = _div(lax.broadcasted_iota(jnp.int32, (BRANCH_W, BRANCH_W), 0), HEAD_DIM)
    cb = _div(lax.broadcasted_iota(jnp.int32, (BRANCH_W, BRANCH_W), 1), HEAD_DIM)
    ones_seg = (rb == cb).astype(BF16)
    n_chunks = ROW_TILE // HGRN_CHUNK

    def body(i, carry):
        cf = pl.multiple_of(i * HGRN_CHUNK, HGRN_CHUNK)
        cbk = pl.multiple_of((n_chunks - 1 - i) * HGRN_CHUNK, HGRN_CHUNK)
        sl_f = pl.ds(cf, HGRN_CHUNK)
        sl_b = pl.ds(cbk, HGRN_CHUNK)
        o_f[0, sl_f, :] = _hgrn_chunk(False, q_f[0, sl_f, :].astype(F32), f_f[0, sl_f, :].astype(F32),
                                      v_f[0, sl_f, :].astype(F32), lb_f, st_ref.at[0], ones_seg)
        o_b[0, sl_b, :] = _hgrn_chunk(True, q_b[0, sl_b, :].astype(F32), f_b[0, sl_b, :].astype(F32),
                                      v_b[0, sl_b, :].astype(F32), lb_b, st_ref.at[1], ones_seg)
        return carry

    lax.fori_loop(0, n_chunks, body, 0)


def _hgrn_call(p3, lb_fwd, lb_bwd, layer):
    bsz = p3.shape[0]
    base = COL_AQ // BRANCH_W

    def spec(tile, col):
        return pl.BlockSpec((1, ROW_TILE, BRANCH_W), lambda b, j: (b, tile(j), col))

    lb_spec = pl.BlockSpec((DEPTH, BRANCH_W), lambda b, j: (0, 0))
    out = jax.ShapeDtypeStruct((bsz, SEQ_ALL, BRANCH_W), F32)
    return pl.pallas_call(
        functools.partial(_hgrn_kernel, layer),
        grid=(bsz, N_TILES),
        in_specs=[spec(_fwd_tile, base), spec(_fwd_tile, base + 1), spec(_fwd_tile, base + 3),
                  spec(_bwd_tile, base), spec(_bwd_tile, base + 2), spec(_bwd_tile, base + 3),
                  lb_spec, lb_spec],
        out_specs=[spec(_fwd_tile, 0), spec(_bwd_tile, 0)],
        out_shape=[out, out],
        scratch_shapes=[pltpu.VMEM((2, BRANCH_W, BRANCH_W), F32)],
        compiler_params=_cparams(("parallel", "arbitrary")),
        name="hgrn_scan",
    )(p3, p3, p3, p3, p3, p3, lb_fwd, lb_bwd)


def _rope(xn, cos, sin):
    w = xn.shape[1]
    lane = lax.broadcasted_iota(jnp.int32, xn.shape, 1)
    nxt = pltpu.roll(xn, w - HEAD_DIM // 4, axis=1)
    prv = pltpu.roll(xn, HEAD_DIM // 4, axis=1)
    partner = jnp.where(_mod(lane, HEAD_DIM // 2) < HEAD_DIM // 4, nxt, prv)
    return xn * cos + partner * sin


def _qkprep_kernel(q_ref, k_ref, cq_ref, sq_ref, ck_ref, sk_ref, wq_ref, wk_ref, segq_ref, segk_ref,
                   qo_ref, ko_ref):
    q = q_ref[0].astype(F32)
    k = k_ref[0].astype(F32)
    qn = q * lax.rsqrt(_segment_mean(q * q, segq_ref[...]) + LN_EPS) * wq_ref[...]
    kn = k * lax.rsqrt(_segment_mean(k * k, segk_ref[...]) + LN_EPS) * wk_ref[...]
    qo_ref[0] = (_rope(qn, cq_ref[...], sq_ref[...]) * HEAD_DIM ** -0.5).astype(BF16)
    ko_ref[0] = _rope(kn, ck_ref[...], sk_ref[...]).astype(BF16)


def _rope_tables():
    t = np.arange(N_LAT)
    nf = HEAD_DIM // 4
    inv_freq = ROPE_BASE ** (-np.arange(nf, dtype=np.float32) / nf)
    ang_r = (t // GRID_W).astype(np.float32)[:, None] * inv_freq
    ang_c = (t % GRID_W).astype(np.float32)[:, None] * inv_freq
    ang = jnp.asarray(np.concatenate([ang_r, ang_r, ang_c, ang_c], axis=1))
    cos = jnp.cos(ang)
    sin = jnp.sin(ang)
    sign = np.tile(np.concatenate([-np.ones(nf, np.float32), np.ones(nf, np.float32)]), 2)
    sin = sin * sign
    cos = jnp.concatenate([jnp.ones((N_CTX, HEAD_DIM), F32), cos], axis=0)
    sin = jnp.concatenate([jnp.zeros((N_CTX, HEAD_DIM), F32), sin], axis=0)
    return cos, sin


def _q_pad_lanes(a):
    parts = []
    zero = jnp.zeros(a.shape[:-1] + (HEAD_DIM,), a.dtype)
    for h in range(N_HEADS):
        ah = a[..., h * HEAD_DIM:(h + 1) * HEAD_DIM]
        parts += [ah, zero] if h // 2 == 0 else [zero, ah]
    return jnp.concatenate(parts, axis=-1)


def _seg_matrix(width):
    idx = np.arange(width) // HEAD_DIM
    return jnp.asarray((idx[:, None] == idx[None, :]).astype(np.float32) / HEAD_DIM, dtype=BF16)


def _qkprep_call(p3, q_norm, k_norm):
    bsz = p3.shape[0]
    cos, sin = _rope_tables()
    cq = _q_pad_lanes(jnp.tile(cos, (1, N_HEADS)))
    sq = _q_pad_lanes(jnp.tile(sin, (1, N_HEADS)))
    ck = jnp.tile(cos, (1, KV_HEADS))
    sk = jnp.tile(sin, (1, KV_HEADS))
    wq = _q_pad_lanes(jnp.tile(q_norm, N_HEADS)[None, :])
    wk = jnp.tile(k_norm, KV_HEADS)[None, :]
    qw, kw = N_HEADS * 128, KV_HEADS * HEAD_DIM
    tile = lambda w, col: pl.BlockSpec((1, ROW_TILE, w), lambda b, j: (b, j, col))
    tab = lambda w: pl.BlockSpec((ROW_TILE, w), lambda b, j: (j, 0))
    full = lambda r, w: pl.BlockSpec((r, w), lambda b, j: (0, 0))
    return pl.pallas_call(
        _qkprep_kernel,
        grid=(bsz, N_TILES),
        in_specs=[tile(qw, COL_DQ // qw), tile(kw, COL_DK // kw), tab(qw), tab(qw), tab(kw), tab(kw),
                  full(1, qw), full(1, kw), full(qw, qw), full(kw, kw)],
        out_specs=[tile(qw, 0), tile(kw, 0)],
        out_shape=[jax.ShapeDtypeStruct((bsz, SEQ_ALL, qw), BF16),
                   jax.ShapeDtypeStruct((bsz, SEQ_ALL, kw), BF16)],
        compiler_params=_cparams(("parallel", "parallel")),
        name="gqa_qk_prep",
    )(p3, p3, cq, sq, ck, sk, wq, wk, _seg_matrix(qw), _seg_matrix(kw))


GQA_KCHUNK = 1024


def _gqa_attend(q2, k_ref, v_ref, chunks):
    m = l = acc = None
    for (k0, kn) in chunks:
        kc = k_ref[0, k0:k0 + kn, :]
        vc = v_ref[0, k0:k0 + kn, :]
        s = _dot_nt(q2, kc)
        mc = jnp.max(s, axis=-1, keepdims=True)
        if m is None:
            m = mc
            p = jnp.exp(s - m)
            l = jnp.sum(p, axis=-1, keepdims=True)
            acc = _dot(p.astype(BF16), vc)
        else:
            mn = jnp.maximum(m, mc)
            a = jnp.exp(m - mn)
            p = jnp.exp(s - mn)
            l = a * l + jnp.sum(p, axis=-1, keepdims=True)
            acc = a * acc + _dot(p.astype(BF16), vc)
            m = mn
    return acc / l


def _gqa_kernel(q_ref, k_ref, v_ref, o_ref):
    lane = lax.broadcasted_iota(jnp.int32, (2 * ROW_TILE, 128), 1)
    ctx_chunks = [(0, N_CTX)]
    all_chunks = ctx_chunks + [(N_CTX + i * GQA_KCHUNK, GQA_KCHUNK) for i in range(N_LAT // GQA_KCHUNK)]

    def run(chunks):
        for g in range(KV_HEADS):
            q2 = jnp.concatenate([q_ref[0, :, (2 * g) * 128:(2 * g + 1) * 128],
                                  q_ref[0, :, (2 * g + 1) * 128:(2 * g + 2) * 128]], axis=0)
            o2 = _gqa_attend(q2, k_ref, v_ref, chunks)
            o2 = jnp.where(_div(lane, HEAD_DIM) == g, o2, 0.0).astype(BF16)
            o_ref[0, :, (2 * g) * 128:(2 * g + 1) * 128] = o2[:ROW_TILE]
            o_ref[0, :, (2 * g + 1) * 128:(2 * g + 2) * 128] = o2[ROW_TILE:]

    @pl.when(pl.program_id(1) == 0)
    def _():
        run(ctx_chunks)

    @pl.when(pl.program_id(1) > 0)
    def _():
        run(all_chunks)


def _gqa_call(qz, kn, p3):
    bsz = qz.shape[0]
    qw, kw = N_HEADS * 128, KV_HEADS * HEAD_DIM
    return pl.pallas_call(
        _gqa_kernel,
        grid=(bsz, N_TILES),
        in_specs=[
            pl.BlockSpec((1, ROW_TILE, qw), lambda b, j: (b, j, 0)),
            pl.BlockSpec((1, SEQ_ALL, kw), lambda b, j: (b, 0, 0)),
            pl.BlockSpec((1, SEQ_ALL, kw), lambda b, j: (b, 0, COL_DV // kw)),
        ],
        out_specs=pl.BlockSpec((1, ROW_TILE, qw), lambda b, j: (b, j, 0)),
        out_shape=jax.ShapeDtypeStruct((bsz, SEQ_ALL, qw), BF16),
        compiler_params=_cparams(("parallel", "parallel")),
        name="gqa_attention",
    )(qz, kn, p3)


NA_ROWS = ROW_TILE // GRID_W
NA_KTILES = 3
NA_KW = NA_KTILES * ROW_TILE


def _na_bias_tables(rpb):
    rows = N_LAT // GRID_W
    tabs = []
    for jb in (0, 1, LAT_TILES - 1):
        base = min(max(jb - 1, 0), LAT_TILES - NA_KTILES)
        r = jb * NA_ROWS + np.arange(NA_ROWS)
        cq = np.arange(GRID_W)
        kr = base * NA_ROWS + np.arange(NA_KTILES * NA_ROWS)
        kc = np.arange(GRID_W)
        start_r = np.clip(r - NA_WIN_R // 2, 0, rows - NA_WIN_R)
        start_c = np.clip(cq - NA_WIN_C // 2, 0, GRID_W - NA_WIN_C)
        ok_r = (kr[None, :] >= start_r[:, None]) & (kr[None, :] < start_r[:, None] + NA_WIN_R)
        ok_c = (kc[None, :] >= start_c[:, None]) & (kc[None, :] < start_c[:, None] + NA_WIN_C)
        drow = np.clip(kr[None, :] - r[:, None] + NA_WIN_R - 1, 0, 2 * NA_WIN_R - 2)
        dcol = np.clip(kc[None, :] - cq[:, None] + NA_WIN_C - 1, 0, 2 * NA_WIN_C - 2)
        bias = rpb[:, drow[:, None, :, None], dcol[None, :, None, :]]
        ok = ok_r[:, None, :, None] & ok_c[None, :, None, :]
        bias = jnp.where(ok[None], bias, NEG_BIG)
        tabs.append(bias.reshape(N_HEADS, ROW_TILE, NA_KW))
    return jnp.stack(tabs, axis=0).astype(F32)


def _na_kernel(q_ref, k0_ref, k1_ref, k2_ref, kc_ref, v0_ref, v1_ref, v2_ref, vc_ref, bias_ref, o_ref):
    lane = lax.broadcasted_iota(jnp.int32, (ROW_TILE, BRANCH_W), 1)
    blk = _div(lane, HEAD_DIM)
    q = q_ref[0].astype(F32) * HEAD_DIM ** -0.5

    def attend(key_refs, val_refs, bias_of):
        acc = jnp.zeros((ROW_TILE, BRANCH_W), F32)
        for h in range(N_HEADS):
            qz = jnp.where(blk == h, q, 0.0).astype(BF16)
            s = [_dot_nt(qz, kr[0]) for kr in key_refs]
            s = [si if bias_of(h, i) is None else si + bias_of(h, i) for i, si in enumerate(s)]
            m = functools.reduce(jnp.maximum, [jnp.max(si, axis=-1, keepdims=True) for si in s])
            p = [jnp.exp(si - m) for si in s]
            l = functools.reduce(jnp.add, [jnp.sum(pi, axis=-1, keepdims=True) for pi in p])
            o = functools.reduce(jnp.add, [_dot(pi.astype(BF16), vr[0]) for pi, vr in zip(p, val_refs)])
            acc = jnp.where(blk == h, o / l, acc)
        o_ref[0] = acc.astype(BF16)

    @pl.when(pl.program_id(1) == 0)
    def _():
        attend([kc_ref], [vc_ref], lambda h, i: None)

    @pl.when(pl.program_id(1) > 0)
    def _():
        def bias_of(h, i):
            if i == NA_KTILES:
                return None
            return bias_ref[0, h, :, i * ROW_TILE:(i + 1) * ROW_TILE]
        attend([k0_ref, k1_ref, k2_ref, kc_ref], [v0_ref, v1_ref, v2_ref, vc_ref], bias_of)


def _na_call(p3, bias_tabs):
    bsz = p3.shape[0]
    cq = COL_CQ // BRANCH_W

    def key_tile(j, m):
        return 1 + jnp.clip(j - 2, 0, LAT_TILES - NA_KTILES) + m

    def spec_k(col, m):
        return pl.BlockSpec((1, ROW_TILE, BRANCH_W), lambda b, j: (b, key_tile(j, m), col))

    def spec_c(col):
        return pl.BlockSpec((1, ROW_TILE, BRANCH_W), lambda b, j: (b, 0, col))

    def variant(j):
        return jnp.where(j <= 1, 0, jnp.where(j == LAT_TILES, 2, 1))

    return pl.pallas_call(
        _na_kernel,
        grid=(bsz, N_TILES),
        in_specs=[pl.BlockSpec((1, ROW_TILE, BRANCH_W), lambda b, j: (b, j, cq)),
                  spec_k(cq + 1, 0), spec_k(cq + 1, 1), spec_k(cq + 1, 2), spec_c(cq + 1),
                  spec_k(cq + 2, 0), spec_k(cq + 2, 1), spec_k(cq + 2, 2), spec_c(cq + 2),
                  pl.BlockSpec((1, N_HEADS, ROW_TILE, NA_KW), lambda b, j: (variant(j), 0, 0, 0))],
        out_specs=pl.BlockSpec((1, ROW_TILE, BRANCH_W), lambda b, j: (b, j, 0)),
        out_shape=jax.ShapeDtypeStruct((bsz, SEQ_ALL, BRANCH_W), BF16),
        compiler_params=_cparams(("parallel", "parallel")),
        name="na_attention",
    )(p3, p3, p3, p3, p3, p3, p3, p3, p3, bias_tabs)


def _merge_kernel(hf_ref, hb_ref, ag_ref, sf_ref, sb_ref, xs_ref, z_ref, yc_ref, yd_ref,
                  g0_ref, g1_ref, g2_ref, g3_ref, x_ref, gate_ref,
                  hn_w, dsk, sn_w, seg_ref, wa, wb, wc, wd, wo, ln_g, ln_b, o_ref):
    oa = hf_ref[0] + hb_ref[0]
    ag = ag_ref[0].astype(F32)
    oa = oa * lax.rsqrt(_segment_mean(oa * oa, seg_ref[...]) + LN_EPS) * hn_w[...]
    br_a = oa * (ag * _sigmoid(ag))
    z = z_ref[0].astype(F32)
    yb = (sf_ref[0] + sb_ref[0] + xs_ref[0].astype(F32) * dsk[...]) * (z * _sigmoid(z))
    br_b = yb * lax.rsqrt(jnp.mean(yb * yb, axis=-1, keepdims=True) + LN_EPS) * sn_w[...]
    merged = _sigmoid(g0_ref[0].astype(F32)) * _dot(br_a.astype(BF16), wa[...])
    merged += _sigmoid(g1_ref[0].astype(F32)) * _dot(br_b.astype(BF16), wb[...])
    merged += _sigmoid(g2_ref[0].astype(F32)) * _dot(yc_ref[0], wc[...])
    merged += _sigmoid(g3_ref[0].astype(F32)) * _dot(yd_ref[0], wd[...])
    y = _dot(merged.astype(BF16), wo[...])
    o_ref[0] = _layer_norm(ALPHA * x_ref[0] + gate_ref[0] * y) * ln_g[...] + ln_b[...]


def _merge_call(layer, mods, xa, p3, hf, hb, sf, sb, xact, yc, yd, hn_w, dsk, sn_w, wa, wb, wc, wd, wo,
                ln_g, ln_b):
    bsz = xa.shape[0]
    t256 = lambda col: pl.BlockSpec((1, ROW_TILE, BRANCH_W), lambda b, j: (b, j, col))
    t512 = pl.BlockSpec((1, ROW_TILE, 512), lambda b, j: (b, j, 0))
    tgate = lambda n: pl.BlockSpec((1, ROW_TILE, D_MODEL), lambda b, j: (b, j, COL_GATE // D_MODEL + n))
    tx = pl.BlockSpec((1, ROW_TILE, D_MODEL), lambda b, j: (b, j, 0))
    full = lambda a: pl.BlockSpec(a.shape, lambda b, j: (0,) * a.ndim)
    seg = _seg_matrix(BRANCH_W)
    consts = [hn_w, dsk, sn_w, seg, wa, wb, wc, wd, wo, ln_g, ln_b]
    return pl.pallas_call(
        _merge_kernel,
        grid=(bsz, N_TILES),
        in_specs=[t256(0), t256(0), t256(COL_AQ // BRANCH_W + 4), t256(0), t256(0), t256(0),
                  t256(COL_Z // BRANCH_W), t256(0), t512,
                  tgate(0), tgate(1), tgate(2), tgate(3), tx, _mod_spec(layer, 2)]
                 + [full(a) for a in consts],
        out_specs=tx,
        out_shape=jax.ShapeDtypeStruct(xa.shape, F32),
        compiler_params=_cparams(("parallel", "parallel")),
        name="merge_postnorm",
    )(hf, hb, p3, sf, sb, xact, p3, yc, yd, p3, p3, p3, p3, xa, mods, *consts)


FFN_CHUNK = 256


def _ffn_kernel(x_ref, sh_ref, sc_ref, gate_ref, wu_ref, wd_ref, ln_g, ln_b, o_ref):
    x = x_ref[0]
    h = (_layer_norm(x) * (1.0 + sc_ref[0]) + sh_ref[0]).astype(BF16)
    acc = jnp.zeros((ROW_TILE, D_MODEL), F32)
    for c in range(FFN_HIDDEN // FFN_CHUNK):
        c0 = c * FFN_CHUNK
        g = _dot(h, wu_ref[:, c0:c0 + FFN_CHUNK])
        u = _dot(h, wu_ref[:, FFN_HIDDEN + c0:FFN_HIDDEN + c0 + FFN_CHUNK])
        a = (g * _sigmoid(g) * u).astype(BF16)
        acc = acc + _dot(a, wd_ref[c0:c0 + FFN_CHUNK, :])
    o_ref[0] = _layer_norm(ALPHA * x + gate_ref[0] * acc) * ln_g[...] + ln_b[...]


def _ffn_call(layer, mods, x1, w_up, w_down, ln_g, ln_b):
    bsz = x1.shape[0]
    tx = pl.BlockSpec((1, ROW_TILE, D_MODEL), lambda b, j: (b, j, 0))
    resident = lambda a: pl.BlockSpec(a.shape, lambda b, j: (0,) * a.ndim, pipeline_mode=pl.Buffered(1))
    full = lambda a: pl.BlockSpec(a.shape, lambda b, j: (0,) * a.ndim)
    return pl.pallas_call(
        _ffn_kernel,
        grid=(bsz, N_TILES),
        in_specs=[tx, _mod_spec(layer, 3), _mod_spec(layer, 4), _mod_spec(layer, 5),
                  resident(w_up), resident(w_down), full(ln_g), full(ln_b)],
        out_specs=tx,
        out_shape=jax.ShapeDtypeStruct(x1.shape, F32),
        compiler_params=_cparams(("parallel", "parallel")),
        name="ffn_postnorm",
    )(x1, mods, mods, mods, w_up, w_down, ln_g, ln_b)


def _pack_w_in(w):
    sizes = ([BRANCH_W] * 5 + [BRANCH_W, SSD_CONV_CH, N_HEADS, N_HEADS] + [BRANCH_W] * 3
             + [BRANCH_W, KV_HEADS * HEAD_DIM, KV_HEADS * HEAD_DIM] + [N_HEADS * D_MODEL])
    offs = np.concatenate([[0], np.cumsum(sizes)])
    (a_q, a_ff, a_fb, a_v, a_g, b_z, b_xbc, b_dtf, b_dtb, c_q, c_k, c_v, d_q, d_k, d_v, gate) = [
        w[:, int(offs[i]):int(offs[i + 1])] for i in range(len(sizes))]
    zeros = lambda n: jnp.zeros((w.shape[0], n), w.dtype)
    cols = [gate, _q_pad_lanes(d_q), b_xbc, b_z, a_q, a_ff, a_fb, a_v, a_g, c_q, c_k, c_v, d_k, d_v,
            b_dtf, b_dtb, zeros(128 - 2 * N_HEADS)]
    packed = jnp.concatenate(cols, axis=1)
    packed = jnp.concatenate([packed, zeros(P_COLS - packed.shape[1])], axis=1)
    return packed.astype(BF16)


def _pad_rows_d(wd):
    return _q_pad_lanes(wd.T).T


def kernel(x, c, ctx, c_ctx, ada_w, ada_b, w_in, hgrn_lb, hgrn_norm, ssd_conv_w, ssd_conv_b, ssd_dt_bias,
           ssd_a_log, ssd_d, ssd_norm, na_rpb, q_norm, k_norm, w_branch, w_out, ln1_g, ln1_b,
           ffn_w_up, ffn_w_down, ln2_g, ln2_b):
    bsz = x.shape[0]
    c8 = jnp.concatenate([c, c_ctx[None, :], jnp.zeros((8 - bsz - 1, D_MODEL), F32)], axis=0)
    mods = _ada_call(c8, ada_w, ada_b).reshape(DEPTH * 8 * 6, 1, D_MODEL)
    xa = jnp.concatenate([ctx, x], axis=1)
    row = lambda v: v.reshape(1, -1)
    for l in range(DEPTH):
        h = _modulate_call(xa, mods, l)
        wp = _pack_w_in(w_in[l])
        p = _matmul_call(h.reshape(bsz * SEQ_ALL, D_MODEL), wp, 1024, 1024, BF16, "in_proj")
        p3 = p.reshape(bsz, SEQ_ALL, P_COLS)
        hf, hb = _hgrn_call(p3, hgrn_lb[0], hgrn_lb[1], l)
        xact = _conv_call(p3, ssd_conv_w[l], ssd_conv_b[l])
        dt_rows = jnp.swapaxes(p3[:, :, COL_DT:COL_DT + 8], 1, 2).astype(F32)
        sf, sb = _ssd_call(xact, p3, dt_rows, ssd_dt_bias[l], ssd_a_log[l])
        yc = _na_call(p3, _na_bias_tables(na_rpb[l]))
        qz, kn = _qkprep_call(p3, q_norm[l], k_norm[l])
        yd = _gqa_call(qz, kn, p3)
        wb = w_branch[l].astype(BF16)
        x1 = _merge_call(l, mods, xa, p3, hf, hb, sf, sb, xact, yc, yd,
                         row(hgrn_norm[l]), row(jnp.repeat(ssd_d[l], HEAD_DIM)), row(ssd_norm[l]),
                         wb[0], wb[1], wb[2], _pad_rows_d(wb[3]), w_out[l].astype(BF16),
                         row(ln1_g[l]), row(ln1_b[l]))
        xa = _ffn_call(l, mods, x1, ffn_w_up[l].astype(BF16), ffn_w_down[l].astype(BF16),
                       row(ln2_g[l]), row(ln2_b[l]))
    return xa[:, N_CTX:, :]
```

```python
import functools
import math

import jax
import jax.numpy as jnp
import numpy as np
from jax import lax
from jax.experimental import pallas as pl
from jax.experimental.pallas import tpu as pltpu

F32 = jnp.float32
BF16 = jnp.bfloat16
HIGHEST = lax.Precision.HIGHEST

D_MODEL = 1024
DEPTH = 2
GRID_W = 64
N_CTX = 256
N_LAT = 4096
SEQ_ALL = N_CTX + N_LAT
HEAD_DIM = 64
BRANCH_W = 256
N_HEADS = 4
SSD_STATE = 128
SSD_GROUPS = 2
SSD_CONV_K = 5
SSD_CONV_CH = BRANCH_W + 2 * SSD_GROUPS * SSD_STATE
KV_HEADS = 2
NA_WIN_R = 8
NA_WIN_C = 16
ROPE_BASE = 10000.0
LN_EPS = 1e-6
FFN_HIDDEN = 2816
ALPHA = (2.0 * DEPTH) ** 0.25

ROW_TILE = 256
N_TILES = SEQ_ALL // ROW_TILE
LAT_TILES = N_LAT // ROW_TILE
HGRN_CHUNK = 64
HGRN_SUB = 16
SSD_CHUNK = 128
NEG_BIG = -1e30
VMEM_LIMIT = 48 * 1024 * 1024

P_COLS = 8192
COL_GATE = 0
COL_DQ = 4096
COL_XBC = 4608
COL_Z = 5376
COL_AQ = 5632
COL_CQ = 6912
COL_DK = 7680
COL_DV = 7808
COL_DT = 7936


def _cparams(sem):
    return pltpu.CompilerParams(dimension_semantics=sem, vmem_limit_bytes=VMEM_LIMIT)


def _sigmoid(x):
    return 1.0 / (1.0 + jnp.exp(-x))


def _div(x, n):
    return lax.shift_right_logical(x, int(math.log2(n)))


def _mod(x, n):
    return x & (n - 1)


def _softplus(x):
    return jnp.maximum(x, 0.0) + jnp.log(1.0 + jnp.exp(-jnp.abs(x)))


def _layer_norm(x):
    mu = jnp.mean(x, axis=-1, keepdims=True)
    xc = x - mu
    var = jnp.mean(xc * xc, axis=-1, keepdims=True)
    return xc * lax.rsqrt(var + LN_EPS)


def _dot(a, b):
    return jnp.dot(a, b, preferred_element_type=F32)


def _dot_nt(a, b):
    return lax.dot_general(a, b, (((1,), (1,)), ((), ())), preferred_element_type=F32)


def _dot_hi(a, b):
    return jnp.dot(a, b, precision=HIGHEST, preferred_element_type=F32)


def _segment_mean(x, seg):
    hi = x.astype(BF16)
    lo = (x - hi.astype(F32)).astype(BF16)
    return _dot(hi, seg) + _dot(lo, seg)


def _ada_kernel(c_ref, w_ref, b_ref, o_ref):
    cv = c_ref[...]
    a = cv * _sigmoid(cv)
    o_ref[0] = _dot_hi(a, w_ref[0]) + b_ref[0]


def _ada_call(c8, ada_w, ada_b):
    n_l = ada_w.shape[0]
    return pl.pallas_call(
        _ada_kernel,
        grid=(n_l, 6),
        in_specs=[
            pl.BlockSpec((8, D_MODEL), lambda l, j: (0, 0)),
            pl.BlockSpec((1, D_MODEL, D_MODEL), lambda l, j: (l, 0, j)),
            pl.BlockSpec((1, 1, D_MODEL), lambda l, j: (l, 0, j)),
        ],
        out_specs=pl.BlockSpec((1, 8, D_MODEL), lambda l, j: (l, 0, j)),
        out_shape=jax.ShapeDtypeStruct((n_l, 8, 6 * D_MODEL), F32),
        compiler_params=_cparams(("arbitrary", "arbitrary")),
        name="ada_mod",
    )(c8, ada_w, ada_b.reshape(n_l, 1, 6 * D_MODEL))


def _mod_index(layer, which):
    def index(b, j):
        return (layer * 48 + jnp.where(j == 0, 4, b) * 6 + which, 0, 0)
    return index


def _mod_spec(layer, which):
    return pl.BlockSpec((1, 1, D_MODEL), _mod_index(layer, which))


def _modulate_kernel(x_ref, sh_ref, sc_ref, o_ref):
    hn = _layer_norm(x_ref[0])
    o_ref[0] = (hn * (1.0 + sc_ref[0]) + sh_ref[0]).astype(BF16)


def _modulate_call(xa, mods, layer):
    bsz = xa.shape[0]
    tile = pl.BlockSpec((1, ROW_TILE, D_MODEL), lambda b, j: (b, j, 0))
    return pl.pallas_call(
        _modulate_kernel,
        grid=(bsz, N_TILES),
        in_specs=[tile, _mod_spec(layer, 0), _mod_spec(layer, 1)],
        out_specs=tile,
        out_shape=jax.ShapeDtypeStruct(xa.shape, BF16),
        compiler_params=_cparams(("parallel", "parallel")),
        name="modulate",
    )(xa, mods, mods)


def _matmul_kernel(a_ref, w_ref, o_ref):
    o_ref[...] = _dot(a_ref[...], w_ref[...]).astype(o_ref.dtype)


def _matmul_call(a, w, tm, tn, out_dtype, name):
    m, k = a.shape
    n = w.shape[1]
    return pl.pallas_call(
        _matmul_kernel,
        grid=(n // tn, m // tm),
        in_specs=[
            pl.BlockSpec((tm, k), lambda j, i: (i, 0)),
            pl.BlockSpec((k, tn), lambda j, i: (0, j)),
        ],
        out_specs=pl.BlockSpec((tm, tn), lambda j, i: (i, j)),
        out_shape=jax.ShapeDtypeStruct((m, n), out_dtype),
        compiler_params=_cparams(("parallel", "parallel")),
        name=name,
    )(a, w)


CONV_PAD = 8


def _conv_kernel(x_ref, w_ref, b_ref, o_ref, xs_ref):
    zeros = jnp.zeros((CONV_PAD, 128), F32)
    xs_ref[0:CONV_PAD, :] = zeros
    xs_ref[CONV_PAD + SEQ_ALL:, :] = zeros
    xs_ref[CONV_PAD:CONV_PAD + SEQ_ALL, :] = x_ref[0].astype(F32)
    w = w_ref[...]
    row = lax.broadcasted_iota(jnp.int32, (ROW_TILE, 128), 0)
    half = SSD_CONV_K // 2
    for ci in range(N_TILES):
        t0 = ci * ROW_TILE
        acc = jnp.broadcast_to(b_ref[...], (ROW_TILE, 128))
        for k in range(SSD_CONV_K):
            j = k - half
            xk = xs_ref[CONV_PAD + t0 + j:CONV_PAD + t0 + j + ROW_TILE, :]
            if ci == 0 and j > 0:
                xk = jnp.where(row + j < ROW_TILE, xk, 0.0)
            if ci == 1 and j < 0:
                xk = jnp.where(row + j >= 0, xk, 0.0)
            acc = acc + xk * w[k:k + 1, :]
        o_ref[0, t0:t0 + ROW_TILE, :] = (acc * _sigmoid(acc)).astype(BF16)


def _conv_call(p3, conv_w, conv_b):
    bsz = p3.shape[0]
    n_ct = SSD_CONV_CH // 128
    return pl.pallas_call(
        _conv_kernel,
        grid=(bsz, n_ct),
        in_specs=[
            pl.BlockSpec((1, SEQ_ALL, 128), lambda b, c: (b, 0, COL_XBC // 128 + c)),
            pl.BlockSpec((SSD_CONV_K, 128), lambda b, c: (0, c)),
            pl.BlockSpec((1, 128), lambda b, c: (0, c)),
        ],
        out_specs=pl.BlockSpec((1, SEQ_ALL, 128), lambda b, c: (b, 0, c)),
        out_shape=jax.ShapeDtypeStruct((bsz, SEQ_ALL, SSD_CONV_CH), BF16),
        scratch_shapes=[pltpu.VMEM((SEQ_ALL + 2 * CONV_PAD, 128), F32)],
        compiler_params=_cparams(("parallel", "parallel")),
        name="ssd_conv",
    )(p3, conv_w, conv_b.reshape(1, SSD_CONV_CH))


def _fwd_tile(j):
    return j


def _bwd_tile(j):
    return jnp.where(j == 0, 0, N_TILES - j)


def _tri(n, upper):
    r = lax.broadcasted_iota(jnp.int32, (n, n), 0)
    c = lax.broadcasted_iota(jnp.int32, (n, n), 1)
    return (c >= r) if upper else (c <= r)


def _ssd_dir(rev, xa_ref, dtc_ref, dtr_ref, bias_r, alog_r, bias_c, alog_c, s_ref, o_ref):
    d = 1 if rev else 0
    dt_c = _softplus(dtc_ref[0].astype(F32) + bias_r)
    g_c = dt_c * (-jnp.exp(alog_r))
    dt_r = _softplus(dtr_ref[0] + bias_c)
    g_r = dt_r * (-jnp.exp(alog_c))
    n = SSD_CHUNK
    tril = _tri(n, False)
    triu = _tri(n, True)
    m_col = (triu if rev else tril).astype(F32)
    m_row = (tril if rev else triu).astype(F32)
    keep = triu if rev else tril
    lane = lax.broadcasted_iota(jnp.int32, (n, 128), 1)
    first = lane < HEAD_DIM
    n_chunks = ROW_TILE // n
    order = range(n_chunks - 1, -1, -1) if rev else range(n_chunks)
    for ci in order:
        c0 = ci * n
        cum_c = _dot_hi(m_col, g_c[c0:c0 + n, :])
        cum_r = _dot_hi(g_r[:, c0:c0 + n], m_row)
        end_row = 0 if rev else n - 1
        for g in range(SSD_GROUPS):
            bm = xa_ref[0, c0:c0 + n, BRANCH_W + g * SSD_STATE:BRANCH_W + (g + 1) * SSD_STATE]
            cm = xa_ref[0, c0:c0 + n, BRANCH_W + (SSD_GROUPS + g) * SSD_STATE:
                        BRANCH_W + (SSD_GROUPS + g + 1) * SSD_STATE]
            x2 = xa_ref[0, c0:c0 + n, g * 128:(g + 1) * 128].astype(F32)
            gmat = _dot_nt(cm, bm)
            col0 = 4 * d + 2 * g
            bc0 = cum_c[:, col0:col0 + 1]
            bc1 = cum_c[:, col0 + 1:col0 + 2]
            br0 = cum_r[col0:col0 + 1, :]
            br1 = cum_r[col0 + 1:col0 + 2, :]
            p0 = jnp.where(keep, jnp.exp(jnp.minimum(bc0 - br0, 0.0)), 0.0) * gmat
            p1 = jnp.where(keep, jnp.exp(jnp.minimum(bc1 - br1, 0.0)), 0.0) * gmat
            dt2 = jnp.where(first, dt_c[c0:c0 + n, col0:col0 + 1], dt_c[c0:c0 + n, col0 + 1:col0 + 2])
            v2 = x2 * dt2
            v2b = v2.astype(BF16)
            pv = jnp.where(first, _dot(p0.astype(BF16), v2b), _dot(p1.astype(BF16), v2b))
            bc2 = jnp.where(first, bc0, bc1)
            s_old = s_ref[g]
            qs = _dot(cm, s_old.astype(BF16))
            o_ref[0, c0:c0 + n, g * 128:(g + 1) * 128] = pv + jnp.exp(bc2) * qs
            e0 = cum_c[end_row:end_row + 1, col0:col0 + 1]
            e1 = cum_c[end_row:end_row + 1, col0 + 1:col0 + 2]
            end2 = jnp.where(first[0:1, :], e0, e1)
            wv = (v2 * jnp.exp(end2 - bc2)).astype(BF16)
            upd = lax.dot_general(bm, wv, (((0,), (0,)), ((), ())), preferred_element_type=F32)
            s_ref[g] = s_old * jnp.exp(end2) + upd


def _ssd_kernel(xa_f, xa_b, dtc_f, dtc_b, dtr_f, dtr_b, bias_r, alog_r, bias_c, alog_c,
                o_f, o_b, s_ref):
    @pl.when(pl.program_id(1) == 0)
    def _():
        s_ref[...] = jnp.zeros(s_ref.shape, F32)

    br, ar, bc, ac = bias_r[...], alog_r[...], bias_c[...], alog_c[...]
    _ssd_dir(False, xa_f, dtc_f, dtr_f, br, ar, bc, ac, s_ref.at[0], o_f)
    _ssd_dir(True, xa_b, dtc_b, dtr_b, br, ar, bc, ac, s_ref.at[1], o_b)


def _ssd_call(xact, p3, dt_rows, dt_bias, a_log):
    bsz = xact.shape[0]
    bias8 = dt_bias.reshape(8).astype(F32)
    alog8 = a_log.reshape(8).astype(F32)
    pad_row = lambda v: jnp.zeros((1, 128), F32).at[0, :8].set(v)
    col = lambda v: jnp.broadcast_to(v[:, None], (8, ROW_TILE))

    def spec_x(tile):
        return pl.BlockSpec((1, ROW_TILE, SSD_CONV_CH), lambda b, j: (b, tile(j), 0))

    def spec_dtc(tile):
        return pl.BlockSpec((1, ROW_TILE, 128), lambda b, j: (b, tile(j), COL_DT // 128))

    def spec_dtr(tile):
        return pl.BlockSpec((1, 8, ROW_TILE), lambda b, j: (b, 0, tile(j)))

    def spec_o(tile):
        return pl.BlockSpec((1, ROW_TILE, BRANCH_W), lambda b, j: (b, tile(j), 0))

    small_r = pl.BlockSpec((1, 128), lambda b, j: (0, 0))
    small_c = pl.BlockSpec((8, ROW_TILE), lambda b, j: (0, 0))
    out = jax.ShapeDtypeStruct((bsz, SEQ_ALL, BRANCH_W), F32)
    return pl.pallas_call(
        _ssd_kernel,
        grid=(bsz, N_TILES),
        in_specs=[spec_x(_fwd_tile), spec_x(_bwd_tile), spec_dtc(_fwd_tile), spec_dtc(_bwd_tile),
                  spec_dtr(_fwd_tile), spec_dtr(_bwd_tile), small_r, small_r, small_c, small_c],
        out_specs=[spec_o(_fwd_tile), spec_o(_bwd_tile)],
        out_shape=[out, out],
        scratch_shapes=[pltpu.VMEM((2, SSD_GROUPS, SSD_STATE, 128), F32)],
        compiler_params=_cparams(("parallel", "arbitrary")),
        name="ssd_scan",
    )(xact, xact, p3, p3, dt_rows, dt_rows, pad_row(bias8), pad_row(alog8), col(bias8), col(alog8))


def _hgrn_lower_bound(lb_ref, layer):
    p = lb_ref[...]
    e = jnp.exp(p - jnp.max(p, axis=0, keepdims=True))
    sm = e / jnp.sum(e, axis=0, keepdims=True)
    acc = sm[0:1, :]
    for i in range(1, layer + 1):
        acc = acc + sm[i:i + 1, :]
    return acc - sm[0:1, :]


def _tile_head(rolls, h, blk):
    pick = lambda i: rolls[(i - h) % N_HEADS]
    return jnp.where(blk == 0, pick(0), jnp.where(blk == 1, pick(1), jnp.where(blk == 2, pick(2), pick(3))))


def _lane_rolls(x):
    return [x] + [pltpu.roll(x, s * HEAD_DIM, axis=1) for s in range(1, N_HEADS)]


def _hgrn_chunk(rev, q, f, v, lb, st_ref, ones_seg):
    n, c = HGRN_CHUNK, HGRN_SUB
    fg = lb + (1.0 - lb) * _sigmoid(f)
    kk = 1.0 - fg
    lg = jnp.log(fg)
    r_i = lax.broadcasted_iota(jnp.int32, (n, n), 0)
    c_i = lax.broadcasted_iota(jnp.int32, (n, n), 1)
    tri = (c_i >= r_i) if rev else (c_i <= r_i)
    same = _div(r_i, c) == _div(c_i, c)
    cum = _dot_hi(tri.astype(F32), lg)
    cl = _dot_hi((tri & same).astype(F32), lg)
    row = lax.broadcasted_iota(jnp.int32, (n, BRANCH_W), 0)
    lane = lax.broadcasted_iota(jnp.int32, (n, BRANCH_W), 1)
    blk = _div(lane, HEAD_DIM)
    sub = _div(row, c)
    pos = _mod(row, c)

    o = jnp.zeros((n, BRANCH_W), F32)
    for dlt in range(c):
        if dlt == 0:
            ks, cs, vs = kk, cum, v
            ok = None
        else:
            sh = (n - dlt) if rev else dlt
            ks = pltpu.roll(kk, sh, axis=0)
            cs = pltpu.roll(cum, sh, axis=0)
            vs = pltpu.roll(v, sh, axis=0)
            ok = (pos + dlt < c) if rev else (pos >= dlt)
        e = q * ks * jnp.exp(jnp.minimum(cum - cs, 0.0))
        if ok is not None:
            e = jnp.where(ok, e, 0.0)
        o = o + _dot(e.astype(BF16), ones_seg) * vs

    qe = q * jnp.exp(cl)
    r_q, r_k, r_c = _lane_rolls(qe), _lane_rolls(kk), _lane_rolls(cum)
    blk1 = blk[0:1, :]
    for h in range(N_HEADS):
        lhs = jnp.where(sub == blk, _tile_head(r_q, h, blk), 0.0)
        kt = _tile_head(r_k, h, blk)
        ct = _tile_head(r_c, h, blk)
        if rev:
            ref_row = jnp.where(blk1 == 0, ct[c:c + 1, :], jnp.where(blk1 == 1, ct[2 * c:2 * c + 1, :],
                                                                    ct[3 * c:3 * c + 1, :]))
            valid = sub > blk
        else:
            ref_row = jnp.where(blk1 == 1, ct[c - 1:c, :], jnp.where(blk1 == 2, ct[2 * c - 1:2 * c, :],
                                                                    ct[3 * c - 1:3 * c, :]))
            valid = sub < blk
        rhs = jnp.where(valid, kt * jnp.exp(jnp.minimum(ref_row - ct, 0.0)), 0.0)
        sc = _dot_nt(lhs.astype(BF16), rhs.astype(BF16))
        vz = jnp.where(blk == h, v, 0.0)
        o = o + _dot(sc.astype(BF16), vz.astype(BF16))

    st = st_ref[...]
    o = o + _dot_nt((q * jnp.exp(cum)).astype(BF16), st.astype(BF16))
    end = cum[0:1, :] if rev else cum[n - 1:n, :]
    kp = kk * jnp.exp(end - cum)
    upd = _dot(v.T.astype(BF16), kp.astype(BF16))
    rb = _div(lax.broadcasted_iota(jnp.int32, (BRANCH_W, BRANCH_W), 0), HEAD_DIM)
    cb = _div(lax.broadcasted_iota(jnp.int32, (BRANCH_W, BRANCH_W), 1), HEAD_DIM)
    st_ref[...] = st * jnp.exp(end) + jnp.where(rb == cb, upd, 0.0)
    return o


def _hgrn_kernel(layer, q_f, f_f, v_f, q_b, f_b, v_b, lbf_ref, lbb_ref, o_f, o_b, st_ref):
    @pl.when(pl.program_id(1) == 0)
    def _():
        st_ref[...] = jnp.zeros(st_ref.shape, F32)

    lb_f = _hgrn_lower_bound(lbf_ref, layer)
    lb_b = _hgrn_lower_bound(lbb_ref, layer)
    rb = _div(lax.broadcasted_iota(jnp.int32, (BRANCH_W, BRANCH_W), 0), HEAD_DIM)
    cb = _div(lax.broadcasted_iota(jnp.int32, (BRANCH_W, BRANCH_W), 1), HEAD_DIM)
    ones_seg = (rb == cb).astype(BF16)
    n_chunks = ROW_TILE // HGRN_CHUNK

    def body(i, carry):
        cf = pl.multiple_of(i * HGRN_CHUNK, HGRN_CHUNK)
        cbk = pl.multiple_of((n_chunks - 1 - i) * HGRN_CHUNK, HGRN_CHUNK)
        sl_f = pl.ds(cf, HGRN_CHUNK)
        sl_b = pl.ds(cbk, HGRN_CHUNK)
        o_f[0, sl_f, :] = _hgrn_chunk(False, q_f[0, sl_f, :].astype(F32), f_f[0, sl_f, :].astype(F32),
                                      v_f[0, sl_f, :].astype(F32), lb_f, st_ref.at[0], ones_seg)
        o_b[0, sl_b, :] = _hgrn_chunk(True, q_b[0, sl_b, :].astype(F32), f_b[0, sl_b, :].astype(F32),
                                      v_b[0, sl_b, :].astype(F32), lb_b, st_ref.at[1], ones_seg)
        return carry

    lax.fori_loop(0, n_chunks, body, 0)


def _hgrn_call(p3, lb_fwd, lb_bwd, layer):
    bsz = p3.shape[0]
    base = COL_AQ // BRANCH_W

    def spec(tile, col):
        return pl.BlockSpec((1, ROW_TILE, BRANCH_W), lambda b, j: (b, tile(j), col))

    lb_spec = pl.BlockSpec((DEPTH, BRANCH_W), lambda b, j: (0, 0))
    out = jax.ShapeDtypeStruct((bsz, SEQ_ALL, BRANCH_W), F32)
    return pl.pallas_call(
        functools.partial(_hgrn_kernel, layer),
        grid=(bsz, N_TILES),
        in_specs=[spec(_fwd_tile, base), spec(_fwd_tile, base + 1), spec(_fwd_tile, base + 3),
                  spec(_bwd_tile, base), spec(_bwd_tile, base + 2), spec(_bwd_tile, base + 3),
                  lb_spec, lb_spec],
        out_specs=[spec(_fwd_tile, 0), spec(_bwd_tile, 0)],
        out_shape=[out, out],
        scratch_shapes=[pltpu.VMEM((2, BRANCH_W, BRANCH_W), F32)],
        compiler_params=_cparams(("parallel", "arbitrary")),
        name="hgrn_scan",
    )(p3, p3, p3, p3, p3, p3, lb_fwd, lb_bwd)


def _rope(xn, cos, sin):
    w = xn.shape[1]
    lane = lax.broadcasted_iota(jnp.int32, xn.shape, 1)
    nxt = pltpu.roll(xn, w - HEAD_DIM // 4, axis=1)
    prv = pltpu.roll(xn, HEAD_DIM // 4, axis=1)
    partner = jnp.where(_mod(lane, HEAD_DIM // 2) < HEAD_DIM // 4, nxt, prv)
    return xn * cos + partner * sin


def _qkprep_kernel(q_ref, k_ref, cq_ref, sq_ref, ck_ref, sk_ref, wq_ref, wk_ref, segq_ref, segk_ref,
                   qo_ref, ko_ref):
    q = q_ref[0].astype(F32)
    k = k_ref[0].astype(F32)
    qn = q * lax.rsqrt(_segment_mean(q * q, segq_ref[...]) + LN_EPS) * wq_ref[...]
    kn = k * lax.rsqrt(_segment_mean(k * k, segk_ref[...]) + LN_EPS) * wk_ref[...]
    qo_ref[0] = (_rope(qn, cq_ref[...], sq_ref[...]) * HEAD_DIM ** -0.5).astype(BF16)
    ko_ref[0] = _rope(kn, ck_ref[...], sk_ref[...]).astype(BF16)


def _rope_tables():
    t = np.arange(N_LAT)
    nf = HEAD_DIM // 4
    inv_freq = ROPE_BASE ** (-np.arange(nf, dtype=np.float32) / nf)
    ang_r = (t // GRID_W).astype(np.float32)[:, None] * inv_freq
    ang_c = (t % GRID_W).astype(np.float32)[:, None] * inv_freq
    ang = jnp.asarray(np.concatenate([ang_r, ang_r, ang_c, ang_c], axis=1))
    cos = jnp.cos(ang)
    sin = jnp.sin(ang)
    sign = np.tile(np.concatenate([-np.ones(nf, np.float32), np.ones(nf, np.float32)]), 2)
    sin = sin * sign
    cos = jnp.concatenate([jnp.ones((N_CTX, HEAD_DIM), F32), cos], axis=0)
    sin = jnp.concatenate([jnp.zeros((N_CTX, HEAD_DIM), F32), sin], axis=0)
    return cos, sin


def _q_pad_lanes(a):
    parts = []
    zero = jnp.zeros(a.shape[:-1] + (HEAD_DIM,), a.dtype)
    for h in range(N_HEADS):
        ah = a[..., h * HEAD_DIM:(h + 1) * HEAD_DIM]
        parts += [ah, zero] if h // 2 == 0 else [zero, ah]
    return jnp.concatenate(parts, axis=-1)


def _seg_matrix(width):
    idx = np.arange(width) // HEAD_DIM
    return jnp.asarray((idx[:, None] == idx[None, :]).astype(np.float32) / HEAD_DIM, dtype=BF16)


def _qkprep_call(p3, q_norm, k_norm):
    bsz = p3.shape[0]
    cos, sin = _rope_tables()
    cq = _q_pad_lanes(jnp.tile(cos, (1, N_HEADS)))
    sq = _q_pad_lanes(jnp.tile(sin, (1, N_HEADS)))
    ck = jnp.tile(cos, (1, KV_HEADS))
    sk = jnp.tile(sin, (1, KV_HEADS))
    wq = _q_pad_lanes(jnp.tile(q_norm, N_HEADS)[None, :])
    wk = jnp.tile(k_norm, KV_HEADS)[None, :]
    qw, kw = N_HEADS * 128, KV_HEADS * HEAD_DIM
    tile = lambda w, col: pl.BlockSpec((1, ROW_TILE, w), lambda b, j: (b, j, col))
    tab = lambda w: pl.BlockSpec((ROW_TILE, w), lambda b, j: (j, 0))
    full = lambda r, w: pl.BlockSpec((r, w), lambda b, j: (0, 0))
    return pl.pallas_call(
        _qkprep_kernel,
        grid=(bsz, N_TILES),
        in_specs=[tile(qw, COL_DQ // qw), tile(kw, COL_DK // kw), tab(qw), tab(qw), tab(kw), tab(kw),
                  full(1, qw), full(1, kw), full(qw, qw), full(kw, kw)],
        out_specs=[tile(qw, 0), tile(kw, 0)],
        out_shape=[jax.ShapeDtypeStruct((bsz, SEQ_ALL, qw), BF16),
                   jax.ShapeDtypeStruct((bsz, SEQ_ALL, kw), BF16)],
        compiler_params=_cparams(("parallel", "parallel")),
        name="gqa_qk_prep",
    )(p3, p3, cq, sq, ck, sk, wq, wk, _seg_matrix(qw), _seg_matrix(kw))


GQA_KCHUNK = 1024


def _gqa_attend(q2, k_ref, v_ref, chunks):
    m = l = acc = None
    for (k0, kn) in chunks:
        kc = k_ref[0, k0:k0 + kn, :]
        vc = v_ref[0, k0:k0 + kn, :]
        s = _dot_nt(q2, kc)
        mc = jnp.max(s, axis=-1, keepdims=True)
        if m is None:
            m = mc
            p = jnp.exp(s - m)
            l = jnp.sum(p, axis=-1, keepdims=True)
            acc = _dot(p.astype(BF16), vc)
        else:
            mn = jnp.maximum(m, mc)
            a = jnp.exp(m - mn)
            p = jnp.exp(s - mn)
            l = a * l + jnp.sum(p, axis=-1, keepdims=True)
            acc = a * acc + _dot(p.astype(BF16), vc)
            m = mn
    return acc / l


def _gqa_kernel(q_ref, k_ref, v_ref, o_ref):
    lane = lax.broadcasted_iota(jnp.int32, (2 * ROW_TILE, 128), 1)
    ctx_chunks = [(0, N_CTX)]
    all_chunks = ctx_chunks + [(N_CTX + i * GQA_KCHUNK, GQA_KCHUNK) for i in range(N_LAT // GQA_KCHUNK)]

    def run(chunks):
        for g in range(KV_HEADS):
            q2 = jnp.concatenate([q_ref[0, :, (2 * g) * 128:(2 * g + 1) * 128],
                                  q_ref[0, :, (2 * g + 1) * 128:(2 * g + 2) * 128]], axis=0)
            o2 = _gqa_attend(q2, k_ref, v_ref, chunks)
            o2 = jnp.where(_div(lane, HEAD_DIM) == g, o2, 0.0).astype(BF16)
            o_ref[0, :, (2 * g) * 128:(2 * g + 1) * 128] = o2[:ROW_TILE]
            o_ref[0, :, (2 * g + 1) * 128:(2 * g + 2) * 128] = o2[ROW_TILE:]

    @pl.when(pl.program_id(1) == 0)
    def _():
        run(ctx_chunks)

    @pl.when(pl.program_id(1) > 0)
    def _():
        run(all_chunks)


def _gqa_call(qz, kn, p3):
    bsz = qz.shape[0]
    qw, kw = N_HEADS * 128, KV_HEADS * HEAD_DIM
    return pl.pallas_call(
        _gqa_kernel,
        grid=(bsz, N_TILES),
        in_specs=[
            pl.BlockSpec((1, ROW_TILE, qw), lambda b, j: (b, j, 0)),
            pl.BlockSpec((1, SEQ_ALL, kw), lambda b, j: (b, 0, 0)),
            pl.BlockSpec((1, SEQ_ALL, kw), lambda b, j: (b, 0, COL_DV // kw)),
        ],
        out_specs=pl.BlockSpec((1, ROW_TILE, qw), lambda b, j: (b, j, 0)),
        out_shape=jax.ShapeDtypeStruct((bsz, SEQ_ALL, qw), BF16),
        compiler_params=_cparams(("parallel", "parallel")),
        name="gqa_attention",
    )(qz, kn, p3)


NA_ROWS = ROW_TILE // GRID_W
NA_KTILES = 3
NA_KW = NA_KTILES * ROW_TILE


def _na_bias_tables(rpb):
    rows = N_LAT // GRID_W
    tabs = []
    for jb in (0, 1, LAT_TILES - 1):
        base = min(max(jb - 1, 0), LAT_TILES - NA_KTILES)
        r = jb * NA_ROWS + np.arange(NA_ROWS)
        cq = np.arange(GRID_W)
        kr = base * NA_ROWS + np.arange(NA_KTILES * NA_ROWS)
        kc = np.arange(GRID_W)
        start_r = np.clip(r - NA_WIN_R // 2, 0, rows - NA_WIN_R)
        start_c = np.clip(cq - NA_WIN_C // 2, 0, GRID_W - NA_WIN_C)
        ok_r = (kr[None, :] >= start_r[:, None]) & (kr[None, :] < start_r[:, None] + NA_WIN_R)
        ok_c = (kc[None, :] >= start_c[:, None]) & (kc[None, :] < start_c[:, None] + NA_WIN_C)
        drow = np.clip(kr[None, :] - r[:, None] + NA_WIN_R - 1, 0, 2 * NA_WIN_R - 2)
        dcol = np.clip(kc[None, :] - cq[:, None] + NA_WIN_C - 1, 0, 2 * NA_WIN_C - 2)
        pick_r = (drow[:, :, None] == np.arange(2 * NA_WIN_R - 1)).astype(np.float32)
        pick_c = (dcol[:, :, None] == np.arange(2 * NA_WIN_C - 1)).astype(np.float32)
        rows_sel = jnp.einsum('amr,hrd->hamd', pick_r, rpb, precision=HIGHEST)
        bias = jnp.einsum('hamd,ckd->hacmk', rows_sel, pick_c, precision=HIGHEST)
        ok = ok_r[:, None, :, None] & ok_c[None, :, None, :]
        bias = jnp.where(ok[None], bias, NEG_BIG)
        tabs.append(bias.reshape(N_HEADS, ROW_TILE, NA_KW))
    return jnp.stack(tabs, axis=0).astype(F32)


def _na_kernel(q_ref, k0_ref, k1_ref, k2_ref, kc_ref, v0_ref, v1_ref, v2_ref, vc_ref, bias_ref, o_ref):
    lane = lax.broadcasted_iota(jnp.int32, (ROW_TILE, BRANCH_W), 1)
    blk = _div(lane, HEAD_DIM)
    q = q_ref[0].astype(F32) * HEAD_DIM ** -0.5

    def attend(key_refs, val_refs, bias_of):
        acc = jnp.zeros((ROW_TILE, BRANCH_W), F32)
        for h in range(N_HEADS):
            qz = jnp.where(blk == h, q, 0.0).astype(BF16)
            s = [_dot_nt(qz, kr[0]) for kr in key_refs]
            s = [si if bias_of(h, i) is None else si + bias_of(h, i) for i, si in enumerate(s)]
            m = functools.reduce(jnp.maximum, [jnp.max(si, axis=-1, keepdims=True) for si in s])
            p = [jnp.exp(si - m) for si in s]
            l = functools.reduce(jnp.add, [jnp.sum(pi, axis=-1, keepdims=True) for pi in p])
            o = functools.reduce(jnp.add, [_dot(pi.astype(BF16), vr[0]) for pi, vr in zip(p, val_refs)])
            acc = jnp.where(blk == h, o / l, acc)
        o_ref[0] = acc.astype(BF16)

    @pl.when(pl.program_id(1) == 0)
    def _():
        attend([kc_ref], [vc_ref], lambda h, i: None)

    @pl.when(pl.program_id(1) > 0)
    def _():
        def bias_of(h, i):
            if i == NA_KTILES:
                return None
            return bias_ref[0, h, :, i * ROW_TILE:(i + 1) * ROW_TILE]
        attend([k0_ref, k1_ref, k2_ref, kc_ref], [v0_ref, v1_ref, v2_ref, vc_ref], bias_of)


def _na_call(p3, bias_tabs):
    bsz = p3.shape[0]
    cq = COL_CQ // BRANCH_W

    def key_tile(j, m):
        return 1 + jnp.clip(j - 2, 0, LAT_TILES - NA_KTILES) + m

    def spec_k(col, m):
        return pl.BlockSpec((1, ROW_TILE, BRANCH_W), lambda b, j: (b, key_tile(j, m), col))

    def spec_c(col):
        return pl.BlockSpec((1, ROW_TILE, BRANCH_W), lambda b, j: (b, 0, col))

    def variant(j):
        return jnp.where(j <= 1, 0, jnp.where(j == LAT_TILES, 2, 1))

    return pl.pallas_call(
        _na_kernel,
        grid=(bsz, N_TILES),
        in_specs=[pl.BlockSpec((1, ROW_TILE, BRANCH_W), lambda b, j: (b, j, cq)),
                  spec_k(cq + 1, 0), spec_k(cq + 1, 1), spec_k(cq + 1, 2), spec_c(cq + 1),
                  spec_k(cq + 2, 0), spec_k(cq + 2, 1), spec_k(cq + 2, 2), spec_c(cq + 2),
                  pl.BlockSpec((1, N_HEADS, ROW_TILE, NA_KW), lambda b, j: (variant(j), 0, 0, 0))],
        out_specs=pl.BlockSpec((1, ROW_TILE, BRANCH_W), lambda b, j: (b, j, 0)),
        out_shape=jax.ShapeDtypeStruct((bsz, SEQ_ALL, BRANCH_W), BF16),
        compiler_params=_cparams(("parallel", "parallel")),
        name="na_attention",
    )(p3, p3, p3, p3, p3, p3, p3, p3, p3, bias_tabs)


def _merge_kernel(hf_ref, hb_ref, ag_ref, sf_ref, sb_ref, xs_ref, z_ref, yc_ref, yd_ref,
                  g0_ref, g1_ref, g2_ref, g3_ref, x_ref, gate_ref,
                  hn_w, dsk, sn_w, seg_ref, wa, wb, wc, wd, wo, ln_g, ln_b, o_ref):
    oa = hf_ref[0] + hb_ref[0]
    ag = ag_ref[0].astype(F32)
    oa = oa * lax.rsqrt(_segment_mean(oa * oa, seg_ref[...]) + LN_EPS) * hn_w[...]
    br_a = oa * (ag * _sigmoid(ag))
    z = z_ref[0].astype(F32)
    yb = (sf_ref[0] + sb_ref[0] + xs_ref[0].astype(F32) * dsk[...]) * (z * _sigmoid(z))
    br_b = yb * lax.rsqrt(jnp.mean(yb * yb, axis=-1, keepdims=True) + LN_EPS) * sn_w[...]
    merged = _sigmoid(g0_ref[0].astype(F32)) * _dot(br_a.astype(BF16), wa[...])
    merged += _sigmoid(g1_ref[0].astype(F32)) * _dot(br_b.astype(BF16), wb[...])
    merged += _sigmoid(g2_ref[0].astype(F32)) * _dot(yc_ref[0], wc[...])
    merged += _sigmoid(g3_ref[0].astype(F32)) * _dot(yd_ref[0], wd[...])
    y = _dot(merged.astype(BF16), wo[...])
    o_ref[0] = _layer_norm(ALPHA * x_ref[0] + gate_ref[0] * y) * ln_g[...] + ln_b[...]


def _merge_call(layer, mods, xa, p3, hf, hb, sf, sb, xact, yc, yd, hn_w, dsk, sn_w, wa, wb, wc, wd, wo,
                ln_g, ln_b):
    bsz = xa.shape[0]
    t256 = lambda col: pl.BlockSpec((1, ROW_TILE, BRANCH_W), lambda b, j: (b, j, col))
    t512 = pl.BlockSpec((1, ROW_TILE, 512), lambda b, j: (b, j, 0))
    tgate = lambda n: pl.BlockSpec((1, ROW_TILE, D_MODEL), lambda b, j: (b, j, COL_GATE // D_MODEL + n))
    tx = pl.BlockSpec((1, ROW_TILE, D_MODEL), lambda b, j: (b, j, 0))
    full = lambda a: pl.BlockSpec(a.shape, lambda b, j: (0,) * a.ndim)
    seg = _seg_matrix(BRANCH_W)
    consts = [hn_w, dsk, sn_w, seg, wa, wb, wc, wd, wo, ln_g, ln_b]
    return pl.pallas_call(
        _merge_kernel,
        grid=(bsz, N_TILES),
        in_specs=[t256(0), t256(0), t256(COL_AQ // BRANCH_W + 4), t256(0), t256(0), t256(0),
                  t256(COL_Z // BRANCH_W), t256(0), t512,
                  tgate(0), tgate(1), tgate(2), tgate(3), tx, _mod_spec(layer, 2)]
                 + [full(a) for a in consts],
        out_specs=tx,
        out_shape=jax.ShapeDtypeStruct(xa.shape, F32),
        compiler_params=_cparams(("parallel", "parallel")),
        name="merge_postnorm",
    )(hf, hb, p3, sf, sb, xact, p3, yc, yd, p3, p3, p3, p3, xa, mods, *consts)


FFN_CHUNK = 256


def _ffn_kernel(x_ref, sh_ref, sc_ref, gate_ref, wu_ref, wd_ref, ln_g, ln_b, o_ref):
    x = x_ref[0]
    h = (_layer_norm(x) * (1.0 + sc_ref[0]) + sh_ref[0]).astype(BF16)
    acc = jnp.zeros((ROW_TILE, D_MODEL), F32)
    for c in range(FFN_HIDDEN // FFN_CHUNK):
        c0 = c * FFN_CHUNK
        g = _dot(h, wu_ref[:, c0:c0 + FFN_CHUNK])
        u = _dot(h, wu_ref[:, FFN_HIDDEN + c0:FFN_HIDDEN + c0 + FFN_CHUNK])
        a = (g * _sigmoid(g) * u).astype(BF16)
        acc = acc + _dot(a, wd_ref[c0:c0 + FFN_CHUNK, :])
    o_ref[0] = _layer_norm(ALPHA * x + gate_ref[0] * acc) * ln_g[...] + ln_b[...]


def _ffn_call(layer, mods, x1, w_up, w_down, ln_g, ln_b):
    bsz = x1.shape[0]
    tx = pl.BlockSpec((1, ROW_TILE, D_MODEL), lambda b, j: (b, j, 0))
    resident = lambda a: pl.BlockSpec(a.shape, lambda b, j: (0,) * a.ndim, pipeline_mode=pl.Buffered(1))
    full = lambda a: pl.BlockSpec(a.shape, lambda b, j: (0,) * a.ndim)
    return pl.pallas_call(
        _ffn_kernel,
        grid=(bsz, N_TILES),
        in_specs=[tx, _mod_spec(layer, 3), _mod_spec(layer, 4), _mod_spec(layer, 5),
                  resident(w_up), resident(w_down), full(ln_g), full(ln_b)],
        out_specs=tx,
        out_shape=jax.ShapeDtypeStruct(x1.shape, F32),
        compiler_params=_cparams(("parallel", "parallel")),
        name="ffn_postnorm",
    )(x1, mods, mods, mods, w_up, w_down, ln_g, ln_b)


def _pack_w_in(w):
    sizes = ([BRANCH_W] * 5 + [BRANCH_W, SSD_CONV_CH, N_HEADS, N_HEADS] + [BRANCH_W] * 3
             + [BRANCH_W, KV_HEADS * HEAD_DIM, KV_HEADS * HEAD_DIM] + [N_HEADS * D_MODEL])
    offs = np.concatenate([[0], np.cumsum(sizes)])
    (a_q, a_ff, a_fb, a_v, a_g, b_z, b_xbc, b_dtf, b_dtb, c_q, c_k, c_v, d_q, d_k, d_v, gate) = [
        w[:, int(offs[i]):int(offs[i + 1])] for i in range(len(sizes))]
    zeros = lambda n: jnp.zeros((w.shape[0], n), w.dtype)
    cols = [gate, _q_pad_lanes(d_q), b_xbc, b_z, a_q, a_ff, a_fb, a_v, a_g, c_q, c_k, c_v, d_k, d_v,
            b_dtf, b_dtb, zeros(128 - 2 * N_HEADS)]
    packed = jnp.concatenate(cols, axis=1)
    packed = jnp.concatenate([packed, zeros(P_COLS - packed.shape[1])], axis=1)
    return packed.astype(BF16)


def _pad_rows_d(wd):
    return _q_pad_lanes(wd.T).T


def kernel(x, c, ctx, c_ctx, ada_w, ada_b, w_in, hgrn_lb, hgrn_norm, ssd_conv_w, ssd_conv_b, ssd_dt_bias,
           ssd_a_log, ssd_d, ssd_norm, na_rpb, q_norm, k_norm, w_branch, w_out, ln1_g, ln1_b,
           ffn_w_up, ffn_w_down, ln2_g, ln2_b):
    bsz = x.shape[0]
    c8 = jnp.concatenate([c, c_ctx[None, :], jnp.zeros((8 - bsz - 1, D_MODEL), F32)], axis=0)
    mods = _ada_call(c8, ada_w, ada_b).reshape(DEPTH * 8 * 6, 1, D_MODEL)
    xa = jnp.concatenate([ctx, x], axis=1)
    row = lambda v: v.reshape(1, -1)
    for l in range(DEPTH):
        h = _modulate_call(xa, mods, l)
        wp = _pack_w_in(w_in[l])
        p = _matmul_call(h.reshape(bsz * SEQ_ALL, D_MODEL), wp, 1024, 1024, BF16, "in_proj")
        p3 = p.reshape(bsz, SEQ_ALL, P_COLS)
        hf, hb = _hgrn_call(p3, hgrn_lb[0], hgrn_lb[1], l)
        xact = _conv_call(p3, ssd_conv_w[l], ssd_conv_b[l])
        dt_rows = jnp.swapaxes(p3[:, :, COL_DT:COL_DT + 8], 1, 2).astype(F32)
        sf, sb = _ssd_call(xact, p3, dt_rows, ssd_dt_bias[l], ssd_a_log[l])
        yc = _na_call(p3, _na_bias_tables(na_rpb[l]))
        qz, kn = _qkprep_call(p3, q_norm[l], k_norm[l])
        yd = _gqa_call(qz, kn, p3)
        wb = w_branch[l].astype(BF16)
        x1 = _merge_call(l, mods, xa, p3, hf, hb, sf, sb, xact, yc, yd,
                         row(hgrn_norm[l]), row(jnp.repeat(ssd_d[l], HEAD_DIM)), row(ssd_norm[l]),
                         wb[0], wb[1], wb[2], _pad_rows_d(wb[3]), w_out[l].astype(BF16),
                         row(ln1_g[l]), row(ln1_b[l]))
        xa = _ffn_call(l, mods, x1, ffn_w_up[l].astype(BF16), ffn_w_down[l].astype(BF16),
                       row(ln2_g[l]), row(ln2_b[l]))
    return xa[:, N_CTX:, :]
```

```python
import functools
import math

import jax
import jax.numpy as jnp
import numpy as np
from jax import lax
from jax.experimental import pallas as pl
from jax.experimental.pallas import tpu as pltpu

F32 = jnp.float32
BF16 = jnp.bfloat16
HIGHEST = lax.Precision.HIGHEST

D_MODEL = 1024
DEPTH = 2
GRID_W = 64
N_CTX = 256
N_LAT = 4096
SEQ_ALL = N_CTX + N_LAT
HEAD_DIM = 64
BRANCH_W = 256
N_HEADS = 4
SSD_STATE = 128
SSD_GROUPS = 2
SSD_CONV_K = 5
SSD_CONV_CH = BRANCH_W + 2 * SSD_GROUPS * SSD_STATE
KV_HEADS = 2
NA_WIN_R = 8
NA_WIN_C = 16
ROPE_BASE = 10000.0
LN_EPS = 1e-6
FFN_HIDDEN = 2816
ALPHA = (2.0 * DEPTH) ** 0.25

ROW_TILE = 256
N_TILES = SEQ_ALL // ROW_TILE
LAT_TILES = N_LAT // ROW_TILE
HGRN_CHUNK = 64
HGRN_SUB = 16
SSD_CHUNK = 128
NEG_BIG = -1e30
VMEM_LIMIT = 48 * 1024 * 1024

P_COLS = 8192
COL_GATE = 0
COL_DQ = 4096
COL_XBC = 4608
COL_Z = 5376
COL_AQ = 5632
COL_CQ = 6912
COL_DK = 7680
COL_DV = 7808
COL_DT = 7936


def _cparams(sem):
    return pltpu.CompilerParams(dimension_semantics=sem, vmem_limit_bytes=VMEM_LIMIT)


def _sigmoid(x):
    return 0.5 * jnp.tanh(0.5 * x) + 0.5


def _div(x, n):
    return lax.shift_right_logical(x, int(math.log2(n)))


def _mod(x, n):
    return x & (n - 1)


def _softplus(x):
    return jnp.maximum(x, 0.0) + jnp.log(1.0 + jnp.exp(-jnp.abs(x)))


def _layer_norm(x):
    mu = jnp.mean(x, axis=-1, keepdims=True)
    xc = x - mu
    var = jnp.mean(xc * xc, axis=-1, keepdims=True)
    return xc * lax.rsqrt(var + LN_EPS)


def _dot(a, b):
    return jnp.dot(a, b, preferred_element_type=F32)


def _dot_nt(a, b):
    return lax.dot_general(a, b, (((1,), (1,)), ((), ())), preferred_element_type=F32)


def _dot_hi(a, b):
    return jnp.dot(a, b, precision=HIGHEST, preferred_element_type=F32)


def _segment_mean(x, seg):
    hi = x.astype(BF16)
    lo = (x - hi.astype(F32)).astype(BF16)
    return _dot(hi, seg) + _dot(lo, seg)


def _ada_kernel(c_ref, w_ref, b_ref, o_ref):
    cv = c_ref[...]
    a = cv * _sigmoid(cv)
    o_ref[0] = _dot_hi(a, w_ref[0]) + b_ref[0]


def _ada_call(c8, ada_w, ada_b):
    n_l = ada_w.shape[0]
    return pl.pallas_call(
        _ada_kernel,
        grid=(n_l, 6),
        in_specs=[
            pl.BlockSpec((8, D_MODEL), lambda l, j: (0, 0)),
            pl.BlockSpec((1, D_MODEL, D_MODEL), lambda l, j: (l, 0, j)),
            pl.BlockSpec((1, 1, D_MODEL), lambda l, j: (l, 0, j)),
        ],
        out_specs=pl.BlockSpec((1, 8, D_MODEL), lambda l, j: (l, 0, j)),
        out_shape=jax.ShapeDtypeStruct((n_l, 8, 6 * D_MODEL), F32),
        compiler_params=_cparams(("arbitrary", "arbitrary")),
        name="ada_mod",
    )(c8, ada_w, ada_b.reshape(n_l, 1, 6 * D_MODEL))


def _mod_index(layer, which):
    def index(b, j):
        return (layer * 48 + jnp.where(j == 0, 4, b) * 6 + which, 0, 0)
    return index


def _flat_mod_spec(layer, which, sub_tiles, u, seq_tiles, first_tile):
    def index(i, *_):
        t = i * sub_tiles + u
        b = t // seq_tiles
        j = t % seq_tiles + first_tile
        return (layer * 48 + jnp.where(j == 0, 4, b) * 6 + which, 0, 0)
    return pl.BlockSpec((1, 1, D_MODEL), index)


PROJ_SUB = 4
PROJ_TN = 1024


def _inproj_kernel(x_ref, *refs):
    mod_refs, (w_ref, o_ref, h_ref) = refs[:2 * PROJ_SUB], refs[2 * PROJ_SUB:]

    @pl.when(pl.program_id(1) == 0)
    def _():
        for u in range(PROJ_SUB):
            rows = slice(u * ROW_TILE, (u + 1) * ROW_TILE)
            sh, sc = mod_refs[2 * u][0], mod_refs[2 * u + 1][0]
            h_ref[rows, :] = (_layer_norm(x_ref[rows, :]) * (1.0 + sc) + sh).astype(BF16)

    o_ref[...] = _dot(h_ref[...], w_ref[...]).astype(BF16)


def _inproj_call(xa_flat, mods, wp, layer):
    m = xa_flat.shape[0]
    tm = PROJ_SUB * ROW_TILE
    mod_specs = []
    for u in range(PROJ_SUB):
        mod_specs += [_flat_mod_spec(layer, 0, PROJ_SUB, u, N_TILES, 0),
                      _flat_mod_spec(layer, 1, PROJ_SUB, u, N_TILES, 0)]
    return pl.pallas_call(
        _inproj_kernel,
        grid=(m // tm, P_COLS // PROJ_TN),
        in_specs=[pl.BlockSpec((tm, D_MODEL), lambda i, n: (i, 0))] + mod_specs
                 + [pl.BlockSpec((D_MODEL, PROJ_TN), lambda i, n: (0, n))],
        out_specs=pl.BlockSpec((tm, PROJ_TN), lambda i, n: (i, n)),
        out_shape=jax.ShapeDtypeStruct((m, P_COLS), BF16),
        scratch_shapes=[pltpu.VMEM((tm, D_MODEL), BF16)],
        compiler_params=_cparams(("parallel", "arbitrary")),
        name="in_proj",
    )(xa_flat, *([mods] * (2 * PROJ_SUB)), wp)


CONV_PAD = 8


def _conv_kernel(x_ref, w_ref, b_ref, o_ref, xs_ref):
    zeros = jnp.zeros((CONV_PAD, 128), F32)
    xs_ref[0:CONV_PAD, :] = zeros
    xs_ref[CONV_PAD + SEQ_ALL:, :] = zeros
    xs_ref[CONV_PAD:CONV_PAD + SEQ_ALL, :] = x_ref[0].astype(F32)
    w = w_ref[...]
    row = lax.broadcasted_iota(jnp.int32, (ROW_TILE, 128), 0)
    half = SSD_CONV_K // 2
    for ci in range(N_TILES):
        t0 = ci * ROW_TILE
        acc = jnp.broadcast_to(b_ref[...], (ROW_TILE, 128))
        for k in range(SSD_CONV_K):
            j = k - half
            xk = xs_ref[CONV_PAD + t0 + j:CONV_PAD + t0 + j + ROW_TILE, :]
            if ci == 0 and j > 0:
                xk = jnp.where(row + j < ROW_TILE, xk, 0.0)
            if ci == 1 and j < 0:
                xk = jnp.where(row + j >= 0, xk, 0.0)
            acc = acc + xk * w[k:k + 1, :]
        o_ref[0, t0:t0 + ROW_TILE, :] = (acc * _sigmoid(acc)).astype(BF16)


def _conv_call(p3, conv_w, conv_b):
    bsz = p3.shape[0]
    n_ct = SSD_CONV_CH // 128
    return pl.pallas_call(
        _conv_kernel,
        grid=(bsz, n_ct),
        in_specs=[
            pl.BlockSpec((1, SEQ_ALL, 128), lambda b, c: (b, 0, COL_XBC // 128 + c)),
            pl.BlockSpec((SSD_CONV_K, 128), lambda b, c: (0, c)),
            pl.BlockSpec((1, 128), lambda b, c: (0, c)),
        ],
        out_specs=pl.BlockSpec((1, SEQ_ALL, 128), lambda b, c: (b, 0, c)),
        out_shape=jax.ShapeDtypeStruct((bsz, SEQ_ALL, SSD_CONV_CH), BF16),
        scratch_shapes=[pltpu.VMEM((SEQ_ALL + 2 * CONV_PAD, 128), F32)],
        compiler_params=_cparams(("parallel", "parallel")),
        name="ssd_conv",
    )(p3, conv_w, conv_b.reshape(1, SSD_CONV_CH))


def _fwd_tile(j):
    return j


def _bwd_tile(j):
    return jnp.where(j == 0, 0, N_TILES - j)


def _tri(n, upper):
    r = lax.broadcasted_iota(jnp.int32, (n, n), 0)
    c = lax.broadcasted_iota(jnp.int32, (n, n), 1)
    return (c >= r) if upper else (c <= r)


def _ssd_dir(rev, xa_ref, dtc_ref, dtr_ref, bias_r, alog_r, bias_c, alog_c, s_ref, o_ref):
    d = 1 if rev else 0
    dt_c = _softplus(dtc_ref[0].astype(F32) + bias_r)
    g_c = dt_c * (-jnp.exp(alog_r))
    dt_r = _softplus(dtr_ref[0] + bias_c)
    g_r = dt_r * (-jnp.exp(alog_c))
    n = SSD_CHUNK
    tril = _tri(n, False)
    triu = _tri(n, True)
    m_col = (triu if rev else tril).astype(F32)
    m_row = (tril if rev else triu).astype(F32)
    keep = triu if rev else tril
    lane = lax.broadcasted_iota(jnp.int32, (n, 128), 1)
    first = lane < HEAD_DIM
    n_chunks = ROW_TILE // n
    order = range(n_chunks - 1, -1, -1) if rev else range(n_chunks)
    for ci in order:
        c0 = ci * n
        cum_c = _dot_hi(m_col, g_c[c0:c0 + n, :])
        cum_r = _dot_hi(g_r[:, c0:c0 + n], m_row)
        end_row = 0 if rev else n - 1
        for g in range(SSD_GROUPS):
            bm = xa_ref[0, c0:c0 + n, BRANCH_W + g * SSD_STATE:BRANCH_W + (g + 1) * SSD_STATE]
            cm = xa_ref[0, c0:c0 + n, BRANCH_W + (SSD_GROUPS + g) * SSD_STATE:
                        BRANCH_W + (SSD_GROUPS + g + 1) * SSD_STATE]
            x2 = xa_ref[0, c0:c0 + n, g * 128:(g + 1) * 128].astype(F32)
            gmat = _dot_nt(cm, bm)
            col0 = 4 * d + 2 * g
            bc0 = cum_c[:, col0:col0 + 1]
            bc1 = cum_c[:, col0 + 1:col0 + 2]
            br0 = cum_r[col0:col0 + 1, :]
            br1 = cum_r[col0 + 1:col0 + 2, :]
            p0 = jnp.where(keep, jnp.exp(jnp.minimum(bc0 - br0, 0.0)), 0.0) * gmat
            p1 = jnp.where(keep, jnp.exp(jnp.minimum(bc1 - br1, 0.0)), 0.0) * gmat
            dt2 = jnp.where(first, dt_c[c0:c0 + n, col0:col0 + 1], dt_c[c0:c0 + n, col0 + 1:col0 + 2])
            v2 = x2 * dt2
            v2b = v2.astype(BF16)
            pv = jnp.where(first, _dot(p0.astype(BF16), v2b), _dot(p1.astype(BF16), v2b))
            bc2 = jnp.where(first, bc0, bc1)
            s_old = s_ref[g]
            qs = _dot(cm, s_old.astype(BF16))
            o_ref[0, c0:c0 + n, g * 128:(g + 1) * 128] = pv + jnp.exp(bc2) * qs
            e0 = cum_c[end_row:end_row + 1, col0:col0 + 1]
            e1 = cum_c[end_row:end_row + 1, col0 + 1:col0 + 2]
            end2 = jnp.where(first[0:1, :], e0, e1)
            wv = (v2 * jnp.exp(end2 - bc2)).astype(BF16)
            upd = lax.dot_general(bm, wv, (((0,), (0,)), ((), ())), preferred_element_type=F32)
            s_ref[g] = s_old * jnp.exp(end2) + upd


def _ssd_kernel(xa_f, xa_b, dtc_f, dtc_b, dtr_f, dtr_b, bias_r, alog_r, bias_c, alog_c,
                o_f, o_b, s_ref):
    @pl.when(pl.program_id(1) == 0)
    def _():
        s_ref[...] = jnp.zeros(s_ref.shape, F32)

    br, ar, bc, ac = bias_r[...], alog_r[...], bias_c[...], alog_c[...]
    _ssd_dir(False, xa_f, dtc_f, dtr_f, br, ar, bc, ac, s_ref.at[0], o_f)
    _ssd_dir(True, xa_b, dtc_b, dtr_b, br, ar, bc, ac, s_ref.at[1], o_b)


def _ssd_call(xact, p3, dt_rows, dt_bias, a_log):
    bsz = xact.shape[0]
    bias8 = dt_bias.reshape(8).astype(F32)
    alog8 = a_log.reshape(8).astype(F32)
    pad_row = lambda v: jnp.zeros((1, 128), F32).at[0, :8].set(v)
    col = lambda v: jnp.broadcast_to(v[:, None], (8, ROW_TILE))

    def spec_x(tile):
        return pl.BlockSpec((1, ROW_TILE, SSD_CONV_CH), lambda b, j: (b, tile(j), 0))

    def spec_dtc(tile):
        return pl.BlockSpec((1, ROW_TILE, 128), lambda b, j: (b, tile(j), COL_DT // 128))

    def spec_dtr(tile):
        return pl.BlockSpec((1, 8, ROW_TILE), lambda b, j: (b, 0, tile(j)))

    def spec_o(tile):
        return pl.BlockSpec((1, ROW_TILE, BRANCH_W), lambda b, j: (b, tile(j), 0))

    small_r = pl.BlockSpec((1, 128), lambda b, j: (0, 0))
    small_c = pl.BlockSpec((8, ROW_TILE), lambda b, j: (0, 0))
    out = jax.ShapeDtypeStruct((bsz, SEQ_ALL, BRANCH_W), F32)
    return pl.pallas_call(
        _ssd_kernel,
        grid=(bsz, N_TILES),
        in_specs=[spec_x(_fwd_tile), spec_x(_bwd_tile), spec_dtc(_fwd_tile), spec_dtc(_bwd_tile),
                  spec_dtr(_fwd_tile), spec_dtr(_bwd_tile), small_r, small_r, small_c, small_c],
        out_specs=[spec_o(_fwd_tile), spec_o(_bwd_tile)],
        out_shape=[out, out],
        scratch_shapes=[pltpu.VMEM((2, SSD_GROUPS, SSD_STATE, 128), F32)],
        compiler_params=_cparams(("parallel", "arbitrary")),
        name="ssd_scan",
    )(xact, xact, p3, p3, dt_rows, dt_rows, pad_row(bias8), pad_row(alog8), col(bias8), col(alog8))


def _hgrn_lower_bound(lb_ref, layer):
    p = lb_ref[...]
    e = jnp.exp(p - jnp.max(p, axis=0, keepdims=True))
    sm = e / jnp.sum(e, axis=0, keepdims=True)
    acc = sm[0:1, :]
    for i in range(1, layer + 1):
        acc = acc + sm[i:i + 1, :]
    return acc - sm[0:1, :]


LOG2E = 1.4426950408889634


def _cumsum_rows(x, rev):
    n = x.shape[0]
    r_i = lax.broadcasted_iota(jnp.int32, (n, n), 0)
    c_i = lax.broadcasted_iota(jnp.int32, (n, n), 1)
    tri = ((c_i >= r_i) if rev else (c_i <= r_i)).astype(BF16)
    hi = x.astype(BF16)
    rest = x - hi.astype(F32)
    mid = rest.astype(BF16)
    lo = (rest - mid.astype(F32)).astype(BF16)
    return _dot(tri, hi) + _dot(tri, mid) + _dot(tri, lo)


def _hgrn_chunk(rev, q, f, v, lb, st_ref, ones_seg, bd_mask):
    n, c = HGRN_CHUNK, HGRN_SUB
    fg = lb + (1.0 - lb) * _sigmoid(f)
    kk = 1.0 - fg
    lg2 = jnp.log(fg) * LOG2E
    cum2 = _cumsum_rows(lg2, rev)
    lk2 = jnp.log(kk) * LOG2E - cum2
    row = lax.broadcasted_iota(jnp.int32, (n, BRANCH_W), 0)
    lane = lax.broadcasted_iota(jnp.int32, (n, BRANCH_W), 1)
    s_idx = _mod(lane, HEAD_DIM)
    pos = _mod(row, c)

    es = []
    for dlt in range(c):
        if dlt == 0:
            arg = cum2 + lk2
        else:
            lks = pltpu.roll(lk2, (n - dlt) if rev else dlt, axis=0)
            ok = (pos + dlt < c) if rev else (pos >= dlt)
            arg = jnp.where(ok, cum2 + lks, NEG_BIG)
        es.append((q * jnp.exp2(arg)).astype(BF16))
    seg = _dot(jnp.concatenate(es, axis=0), ones_seg)
    sc = jnp.zeros((n, BRANCH_W), F32)
    for dlt in range(c):
        place = (s_idx == row + dlt) if rev else (s_idx == row - dlt)
        sc = jnp.where(place, seg[dlt * n:(dlt + 1) * n, :], sc)

    cl_parts = []
    for i in range(n // c):
        blk_rows = cum2[i * c:(i + 1) * c, :]
        if rev:
            cl_parts.append(blk_rows - cum2[(i + 1) * c:(i + 1) * c + 1, :] if i < n // c - 1 else blk_rows)
        else:
            cl_parts.append(blk_rows - cum2[i * c - 1:i * c, :] if i > 0 else blk_rows)
    qe = (q * jnp.exp2(jnp.concatenate(cl_parts, axis=0))).astype(BF16)
    parts = []
    for i in range(n // c):
        if (rev and i == n // c - 1) or (not rev and i == 0):
            parts.append(jnp.zeros((c, BRANCH_W), F32))
            continue
        if rev:
            ref = cum2[(i + 1) * c:(i + 1) * c + 1, :]
            valid = row >= (i + 1) * c
        else:
            ref = cum2[i * c - 1:i * c, :]
            valid = row < i * c
        rhs = jnp.exp2(jnp.where(valid, lk2 + ref, NEG_BIG))
        rhs_bd = jnp.where(bd_mask, jnp.concatenate([rhs] * N_HEADS, axis=0), 0.0).astype(BF16)
        parts.append(_dot_nt(qe[i * c:(i + 1) * c, :], rhs_bd))
    sc = sc + jnp.concatenate(parts, axis=0)
    v_bd = jnp.where(bd_mask, jnp.concatenate([v] * N_HEADS, axis=0), 0.0).astype(BF16)
    o = _dot(sc.astype(BF16), v_bd)

    st = st_ref[...]
    o = o + _dot_nt((q * jnp.exp2(cum2)).astype(BF16), st.astype(BF16))
    end = cum2[0:1, :] if rev else cum2[n - 1:n, :]
    kp = jnp.exp2(lk2 + end)
    upd = _dot(v.T.astype(BF16), kp.astype(BF16))
    st_ref[...] = st * jnp.exp2(end) + jnp.where(bd_mask, upd, 0.0)
    return o


def _hgrn_kernel(layer, q_f, f_f, v_f, q_b, f_b, v_b, lbf_ref, lbb_ref, o_f, o_b, st_ref):
    @pl.when(pl.program_id(1) == 0)
    def _():
        st_ref[...] = jnp.zeros(st_ref.shape, F32)

    lb_f = _hgrn_lower_bound(lbf_ref, layer)
    lb_b = _hgrn_lower_bound(lbb_ref, layer)
    rb = _div(lax.broadcasted_iota(jnp.int32, (BRANCH_W, BRANCH_W), 0), HEAD_DIM)
    cb = _div(lax.broadcasted_iota(jnp.int32, (BRANCH_W, BRANCH_W), 1), HEAD_DIM)
    bd_mask = rb == cb
    ones_seg = bd_mask.astype(BF16)
    n_chunks = ROW_TILE // HGRN_CHUNK

    def body(i, carry):
        cf = pl.multiple_of(i * HGRN_CHUNK, HGRN_CHUNK)
        cbk = pl.multiple_of((n_chunks - 1 - i) * HGRN_CHUNK, HGRN_CHUNK)
        sl_f = pl.ds(cf, HGRN_CHUNK)
        sl_b = pl.ds(cbk, HGRN_CHUNK)
        o_f[0, sl_f, :] = _hgrn_chunk(False, q_f[0, sl_f, :].astype(F32), f_f[0, sl_f, :].astype(F32),
                                      v_f[0, sl_f, :].astype(F32), lb_f, st_ref.at[0], ones_seg, bd_mask)
        o_b[0, sl_b, :] = _hgrn_chunk(True, q_b[0, sl_b, :].astype(F32), f_b[0, sl_b, :].astype(F32),
                                      v_b[0, sl_b, :].astype(F32), lb_b, st_ref.at[1], ones_seg, bd_mask)
        return carry

    lax.fori_loop(0, n_chunks, body, 0)


def _hgrn_call(p3, lb_fwd, lb_bwd, layer):
    bsz = p3.shape[0]
    base = COL_AQ // BRANCH_W

    def spec(tile, col):
        return pl.BlockSpec((1, ROW_TILE, BRANCH_W), lambda b, j: (b, tile(j), col))

    lb_spec = pl.BlockSpec((DEPTH, BRANCH_W), lambda b, j: (0, 0))
    out = jax.ShapeDtypeStruct((bsz, SEQ_ALL, BRANCH_W), F32)
    return pl.pallas_call(
        functools.partial(_hgrn_kernel, layer),
        grid=(bsz, N_TILES),
        in_specs=[spec(_fwd_tile, base), spec(_fwd_tile, base + 1), spec(_fwd_tile, base + 3),
                  spec(_bwd_tile, base), spec(_bwd_tile, base + 2), spec(_bwd_tile, base + 3),
                  lb_spec, lb_spec],
        out_specs=[spec(_fwd_tile, 0), spec(_bwd_tile, 0)],
        out_shape=[out, out],
        scratch_shapes=[pltpu.VMEM((2, BRANCH_W, BRANCH_W), F32)],
        compiler_params=_cparams(("parallel", "arbitrary")),
        name="hgrn_scan",
    )(p3, p3, p3, p3, p3, p3, lb_fwd, lb_bwd)


def _rope(xn, cos, sin):
    w = xn.shape[1]
    lane = lax.broadcasted_iota(jnp.int32, xn.shape, 1)
    nxt = pltpu.roll(xn, w - HEAD_DIM // 4, axis=1)
    prv = pltpu.roll(xn, HEAD_DIM // 4, axis=1)
    partner = jnp.where(_mod(lane, HEAD_DIM // 2) < HEAD_DIM // 4, nxt, prv)
    return xn * cos + partner * sin


def _qkprep_kernel(q_ref, k_ref, v_ref, cq_ref, sq_ref, ck_ref, sk_ref, wq_ref, wk_ref, segq_ref, segk_ref,
                   qo_ref, ko_ref, vo_ref):
    q = q_ref[0].astype(F32)
    k = k_ref[0].astype(F32)
    qn = q * lax.rsqrt(_segment_mean(q * q, segq_ref[...]) + LN_EPS) * wq_ref[...]
    kn = k * lax.rsqrt(_segment_mean(k * k, segk_ref[...]) + LN_EPS) * wk_ref[...]
    qo_ref[0] = (_rope(qn, cq_ref[...], sq_ref[...]) * (HEAD_DIM ** -0.5 * LOG2E)).astype(BF16)
    ko_ref[0] = _rope(kn, ck_ref[...], sk_ref[...]).astype(BF16)
    v = v_ref[0].astype(F32)
    lane = lax.broadcasted_iota(jnp.int32, v.shape, 1)
    vo_ref[0] = jnp.concatenate([jnp.where(lane < HEAD_DIM, v, 1.0), jnp.where(lane < HEAD_DIM, 1.0, v)],
                                axis=1).astype(BF16)


def _rope_tables():
    t = np.arange(N_LAT)
    nf = HEAD_DIM // 4
    inv_freq = ROPE_BASE ** (-np.arange(nf, dtype=np.float32) / nf)
    ang_r = (t // GRID_W).astype(np.float32)[:, None] * inv_freq
    ang_c = (t % GRID_W).astype(np.float32)[:, None] * inv_freq
    ang = jnp.asarray(np.concatenate([ang_r, ang_r, ang_c, ang_c], axis=1))
    cos = jnp.cos(ang)
    sin = jnp.sin(ang)
    sign = np.tile(np.concatenate([-np.ones(nf, np.float32), np.ones(nf, np.float32)]), 2)
    sin = sin * sign
    cos = jnp.concatenate([jnp.ones((N_CTX, HEAD_DIM), F32), cos], axis=0)
    sin = jnp.concatenate([jnp.zeros((N_CTX, HEAD_DIM), F32), sin], axis=0)
    return cos, sin


def _q_pad_lanes(a):
    parts = []
    zero = jnp.zeros(a.shape[:-1] + (HEAD_DIM,), a.dtype)
    for h in range(N_HEADS):
        ah = a[..., h * HEAD_DIM:(h + 1) * HEAD_DIM]
        parts += [ah, zero] if h // 2 == 0 else [zero, ah]
    return jnp.concatenate(parts, axis=-1)


def _seg_matrix(width):
    idx = np.arange(width) // HEAD_DIM
    return jnp.asarray((idx[:, None] == idx[None, :]).astype(np.float32) / HEAD_DIM, dtype=BF16)


def _qkprep_call(p3, q_norm, k_norm):
    bsz = p3.shape[0]
    cos, sin = _rope_tables()
    cq = _q_pad_lanes(jnp.tile(cos, (1, N_HEADS)))
    sq = _q_pad_lanes(jnp.tile(sin, (1, N_HEADS)))
    ck = jnp.tile(cos, (1, KV_HEADS))
    sk = jnp.tile(sin, (1, KV_HEADS))
    wq = _q_pad_lanes(jnp.tile(q_norm, N_HEADS)[None, :])
    wk = jnp.tile(k_norm, KV_HEADS)[None, :]
    qw, kw = N_HEADS * 128, KV_HEADS * HEAD_DIM
    tile = lambda w, col: pl.BlockSpec((1, ROW_TILE, w), lambda j, b: (b, j, col))
    tab = lambda w: pl.BlockSpec((ROW_TILE, w), lambda j, b: (j, 0))
    full = lambda r, w: pl.BlockSpec((r, w), lambda j, b: (0, 0))
    return pl.pallas_call(
        _qkprep_kernel,
        grid=(N_TILES, bsz),
        in_specs=[tile(qw, COL_DQ // qw), tile(kw, COL_DK // kw), tile(kw, COL_DV // kw),
                  tab(qw), tab(qw), tab(kw), tab(kw),
                  full(1, qw), full(1, kw), full(qw, qw), full(kw, kw)],
        out_specs=[tile(qw, 0), tile(kw, 0), tile(KV_HEADS * kw, 0)],
        out_shape=[jax.ShapeDtypeStruct((bsz, SEQ_ALL, qw), BF16),
                   jax.ShapeDtypeStruct((bsz, SEQ_ALL, kw), BF16),
                   jax.ShapeDtypeStruct((bsz, SEQ_ALL, KV_HEADS * kw), BF16)],
        compiler_params=_cparams(("parallel", "parallel")),
        name="gqa_qk_prep",
    )(p3, p3, p3, cq, sq, ck, sk, wq, wk, _seg_matrix(qw), _seg_matrix(kw))


GQA_KCHUNK = 1024


def _gqa_attend(q2, k_ref, v_ref, g, chunks):
    m = acc = None
    for (k0, kn) in chunks:
        kc = k_ref[0, k0:k0 + kn, :]
        vc = v_ref[0, k0:k0 + kn, g * 128:(g + 1) * 128]
        s = _dot_nt(q2, kc)
        mc = jnp.max(s, axis=-1, keepdims=True)
        if m is None:
            m = mc
            acc = _dot(jnp.exp2(s - m).astype(BF16), vc)
        else:
            mn = jnp.maximum(m, mc)
            acc = jnp.exp2(m - mn) * acc + _dot(jnp.exp2(s - mn).astype(BF16), vc)
            m = mn
    sum_lane = (1 - g) * HEAD_DIM
    return acc / acc[:, sum_lane:sum_lane + 1]


def _gqa_kernel(q_ref, k_ref, v_ref, o_ref):
    lane = lax.broadcasted_iota(jnp.int32, (2 * ROW_TILE, 128), 1)
    ctx_chunks = [(0, N_CTX)]
    all_chunks = ctx_chunks + [(N_CTX + i * GQA_KCHUNK, GQA_KCHUNK) for i in range(N_LAT // GQA_KCHUNK)]

    def run(chunks):
        for g in range(KV_HEADS):
            q2 = jnp.concatenate([q_ref[0, :, (2 * g) * 128:(2 * g + 1) * 128],
                                  q_ref[0, :, (2 * g + 1) * 128:(2 * g + 2) * 128]], axis=0)
            o2 = _gqa_attend(q2, k_ref, v_ref, g, chunks)
            o2 = jnp.where(_div(lane, HEAD_DIM) == g, o2, 0.0).astype(BF16)
            o_ref[0, :, (2 * g) * 128:(2 * g + 1) * 128] = o2[:ROW_TILE]
            o_ref[0, :, (2 * g + 1) * 128:(2 * g + 2) * 128] = o2[ROW_TILE:]

    @pl.when(pl.program_id(1) == 0)
    def _():
        run(ctx_chunks)

    @pl.when(pl.program_id(1) > 0)
    def _():
        run(all_chunks)


def _gqa_call(qz, kn, va):
    bsz = qz.shape[0]
    qw, kw = N_HEADS * 128, KV_HEADS * HEAD_DIM
    return pl.pallas_call(
        _gqa_kernel,
        grid=(bsz, N_TILES),
        in_specs=[
            pl.BlockSpec((1, ROW_TILE, qw), lambda b, j: (b, j, 0)),
            pl.BlockSpec((1, SEQ_ALL, kw), lambda b, j: (b, 0, 0)),
            pl.BlockSpec((1, SEQ_ALL, KV_HEADS * kw), lambda b, j: (b, 0, 0)),
        ],
        out_specs=pl.BlockSpec((1, ROW_TILE, qw), lambda b, j: (b, j, 0)),
        out_shape=jax.ShapeDtypeStruct((bsz, SEQ_ALL, qw), BF16),
        compiler_params=_cparams(("parallel", "parallel")),
        name="gqa_attention",
    )(qz, kn, va)


NA_ROWS = ROW_TILE // GRID_W
NA_KTILES = 3
NA_KW = NA_KTILES * ROW_TILE


def _na_bias_tables(rpb):
    rows = N_LAT // GRID_W
    tabs = []
    for jb in (0, 1, LAT_TILES - 1):
        base = min(max(jb - 1, 0), LAT_TILES - NA_KTILES)
        r = jb * NA_ROWS + np.arange(NA_ROWS)
        cq = np.arange(GRID_W)
        kr = base * NA_ROWS + np.arange(NA_KTILES * NA_ROWS)
        kc = np.arange(GRID_W)
        start_r = np.clip(r - NA_WIN_R // 2, 0, rows - NA_WIN_R)
        start_c = np.clip(cq - NA_WIN_C // 2, 0, GRID_W - NA_WIN_C)
        ok_r = (kr[None, :] >= start_r[:, None]) & (kr[None, :] < start_r[:, None] + NA_WIN_R)
        ok_c = (kc[None, :] >= start_c[:, None]) & (kc[None, :] < start_c[:, None] + NA_WIN_C)
        drow = np.clip(kr[None, :] - r[:, None] + NA_WIN_R - 1, 0, 2 * NA_WIN_R - 2)
        dcol = np.clip(kc[None, :] - cq[:, None] + NA_WIN_C - 1, 0, 2 * NA_WIN_C - 2)
        pick_r = (drow[:, :, None] == np.arange(2 * NA_WIN_R - 1)).astype(np.float32)
        pick_c = (dcol[:, :, None] == np.arange(2 * NA_WIN_C - 1)).astype(np.float32)
        rows_sel = jnp.einsum('amr,hrd->hamd', pick_r, rpb, precision=HIGHEST)
        bias = jnp.einsum('hamd,ckd->hacmk', rows_sel, pick_c, precision=HIGHEST)
        ok = ok_r[:, None, :, None] & ok_c[None, :, None, :]
        bias = jnp.where(ok[None], bias, NEG_BIG)
        tabs.append(bias.reshape(N_HEADS, ROW_TILE, NA_KW))
    return jnp.stack(tabs, axis=0).astype(F32)


def _na_kernel(q_ref, k0_ref, k1_ref, k2_ref, kc_ref, v0_ref, v1_ref, v2_ref, vc_ref, bias_ref, o_ref):
    lane = lax.broadcasted_iota(jnp.int32, (ROW_TILE, BRANCH_W), 1)
    blk = _div(lane, HEAD_DIM)
    q = q_ref[0].astype(F32) * HEAD_DIM ** -0.5

    def attend(key_refs, val_refs, bias_of):
        acc = jnp.zeros((ROW_TILE, BRANCH_W), F32)
        for h in range(N_HEADS):
            qz = jnp.where(blk == h, q, 0.0).astype(BF16)
            s = [_dot_nt(qz, kr[0]) for kr in key_refs]
            s = [si if bias_of(h, i) is None else si + bias_of(h, i) for i, si in enumerate(s)]
            m = functools.reduce(jnp.maximum, [jnp.max(si, axis=-1, keepdims=True) for si in s])
            p = [jnp.exp(si - m) for si in s]
            l = functools.reduce(jnp.add, [jnp.sum(pi, axis=-1, keepdims=True) for pi in p])
            o = functools.reduce(jnp.add, [_dot(pi.astype(BF16), vr[0]) for pi, vr in zip(p, val_refs)])
            acc = jnp.where(blk == h, o / l, acc)
        o_ref[0] = acc.astype(BF16)

    @pl.when(pl.program_id(1) == 0)
    def _():
        attend([kc_ref], [vc_ref], lambda h, i: None)

    @pl.when(pl.program_id(1) > 0)
    def _():
        def bias_of(h, i):
            if i == NA_KTILES:
                return None
            return bias_ref[0, h, :, i * ROW_TILE:(i + 1) * ROW_TILE]
        attend([k0_ref, k1_ref, k2_ref, kc_ref], [v0_ref, v1_ref, v2_ref, vc_ref], bias_of)


def _na_call(p3, bias_tabs):
    bsz = p3.shape[0]
    cq = COL_CQ // BRANCH_W

    def key_tile(j, m):
        return 1 + jnp.clip(j - 2, 0, LAT_TILES - NA_KTILES) + m

    def spec_k(col, m):
        return pl.BlockSpec((1, ROW_TILE, BRANCH_W), lambda b, j: (b, key_tile(j, m), col))

    def spec_c(col):
        return pl.BlockSpec((1, ROW_TILE, BRANCH_W), lambda b, j: (b, 0, col))

    def variant(j):
        return jnp.where(j <= 1, 0, jnp.where(j == LAT_TILES, 2, 1))

    return pl.pallas_call(
        _na_kernel,
        grid=(bsz, N_TILES),
        in_specs=[pl.BlockSpec((1, ROW_TILE, BRANCH_W), lambda b, j: (b, j, cq)),
                  spec_k(cq + 1, 0), spec_k(cq + 1, 1), spec_k(cq + 1, 2), spec_c(cq + 1),
                  spec_k(cq + 2, 0), spec_k(cq + 2, 1), spec_k(cq + 2, 2), spec_c(cq + 2),
                  pl.BlockSpec((1, N_HEADS, ROW_TILE, NA_KW), lambda b, j: (variant(j), 0, 0, 0))],
        out_specs=pl.BlockSpec((1, ROW_TILE, BRANCH_W), lambda b, j: (b, j, 0)),
        out_shape=jax.ShapeDtypeStruct((bsz, SEQ_ALL, BRANCH_W), BF16),
        compiler_params=_cparams(("parallel", "parallel")),
        name="na_attention",
    )(p3, p3, p3, p3, p3, p3, p3, p3, p3, bias_tabs)


def _merge_kernel(hf_ref, hb_ref, ag_ref, sf_ref, sb_ref, xs_ref, z_ref, yc_ref, yd_ref,
                  g0_ref, g1_ref, g2_ref, g3_ref, x_ref, gate_ref,
                  hn_w, dsk, sn_w, seg_ref, wa, wb, wc, wd, wo, ln_g, ln_b, o_ref):
    oa = hf_ref[0] + hb_ref[0]
    ag = ag_ref[0].astype(F32)
    oa = oa * lax.rsqrt(_segment_mean(oa * oa, seg_ref[...]) + LN_EPS) * hn_w[...]
    br_a = oa * (ag * _sigmoid(ag))
    z = z_ref[0].astype(F32)
    yb = (sf_ref[0] + sb_ref[0] + xs_ref[0].astype(F32) * dsk[...]) * (z * _sigmoid(z))
    br_b = yb * lax.rsqrt(jnp.mean(yb * yb, axis=-1, keepdims=True) + LN_EPS) * sn_w[...]
    merged = _sigmoid(g0_ref[0].astype(F32)) * _dot(br_a.astype(BF16), wa[...])
    merged += _sigmoid(g1_ref[0].astype(F32)) * _dot(br_b.astype(BF16), wb[...])
    merged += _sigmoid(g2_ref[0].astype(F32)) * _dot(yc_ref[0], wc[...])
    merged += _sigmoid(g3_ref[0].astype(F32)) * _dot(yd_ref[0], wd[...])
    y = _dot(merged.astype(BF16), wo[...])
    o_ref[0] = _layer_norm(ALPHA * x_ref[0] + gate_ref[0] * y) * ln_g[...] + ln_b[...]


def _merge_call(layer, mods, xa, p3, hf, hb, sf, sb, xact, yc, yd, hn_w, dsk, sn_w, wa, wb, wc, wd, wo,
                ln_g, ln_b, with_ctx):
    bsz = xa.shape[0]
    t0 = 0 if with_ctx else 1
    tiles = N_TILES - t0
    t256 = lambda col: pl.BlockSpec((1, ROW_TILE, BRANCH_W), lambda b, j: (b, j + t0, col))
    t512 = pl.BlockSpec((1, ROW_TILE, 512), lambda b, j: (b, j + t0, 0))
    tgate = lambda n: pl.BlockSpec((1, ROW_TILE, D_MODEL), lambda b, j: (b, j + t0, COL_GATE // D_MODEL + n))
    tx = pl.BlockSpec((1, ROW_TILE, D_MODEL), lambda b, j: (b, j + t0, 0))
    gate = pl.BlockSpec((1, 1, D_MODEL), lambda b, j: _mod_index(layer, 2)(b, j + t0))
    full = lambda a: pl.BlockSpec(a.shape, lambda b, j: (0,) * a.ndim)
    seg = _seg_matrix(BRANCH_W)
    consts = [hn_w, dsk, sn_w, seg, wa, wb, wc, wd, wo, ln_g, ln_b]
    return pl.pallas_call(
        _merge_kernel,
        grid=(bsz, tiles),
        in_specs=[t256(0), t256(0), t256(COL_AQ // BRANCH_W + 4), t256(0), t256(0), t256(0),
                  t256(COL_Z // BRANCH_W), t256(0), t512,
                  tgate(0), tgate(1), tgate(2), tgate(3), tx, gate]
                 + [full(a) for a in consts],
        out_specs=pl.BlockSpec((1, ROW_TILE, D_MODEL), lambda b, j: (b, j, 0)),
        out_shape=jax.ShapeDtypeStruct((bsz, tiles * ROW_TILE, D_MODEL), F32),
        compiler_params=_cparams(("parallel", "parallel")),
        name="merge_postnorm",
    )(hf, hb, p3, sf, sb, xact, p3, yc, yd, p3, p3, p3, p3, xa, mods, *consts)


FFN_CHUNK = 256


FFN_SUB = 2


def _ffn_kernel(x_ref, *refs):
    mod_refs, (wu_ref, wd_ref, ln_g, ln_b, o_ref) = refs[:3 * FFN_SUB], refs[3 * FFN_SUB:]
    tm = FFN_SUB * ROW_TILE
    x = x_ref[...]
    ln = _layer_norm(x)
    h = jnp.concatenate(
        [ln[u * ROW_TILE:(u + 1) * ROW_TILE] * (1.0 + mod_refs[3 * u + 1][0]) + mod_refs[3 * u][0]
         for u in range(FFN_SUB)], axis=0).astype(BF16)
    acc = jnp.zeros((tm, D_MODEL), F32)
    for c in range(FFN_HIDDEN // FFN_CHUNK):
        c0 = c * FFN_CHUNK
        g = _dot(h, wu_ref[:, c0:c0 + FFN_CHUNK])
        u = _dot(h, wu_ref[:, FFN_HIDDEN + c0:FFN_HIDDEN + c0 + FFN_CHUNK])
        a = (g * _sigmoid(g) * u).astype(BF16)
        acc = acc + _dot(a, wd_ref[c0:c0 + FFN_CHUNK, :])
    for u in range(FFN_SUB):
        rows = slice(u * ROW_TILE, (u + 1) * ROW_TILE)
        y = ALPHA * x[rows] + mod_refs[3 * u + 2][0] * acc[rows]
        o_ref[rows, :] = _layer_norm(y) * ln_g[...] + ln_b[...]


def _ffn_call(layer, mods, x1_flat, w_up, w_down, ln_g, ln_b, with_ctx):
    m = x1_flat.shape[0]
    tm = FFN_SUB * ROW_TILE
    seq_tiles, first = (N_TILES, 0) if with_ctx else (LAT_TILES, 1)
    mod_specs = [_flat_mod_spec(layer, 3 + which, FFN_SUB, u, seq_tiles, first)
                 for u in range(FFN_SUB) for which in range(3)]
    tx = pl.BlockSpec((tm, D_MODEL), lambda i: (i, 0))
    resident = lambda a: pl.BlockSpec(a.shape, lambda i: (0,) * a.ndim, pipeline_mode=pl.Buffered(1))
    full = lambda a: pl.BlockSpec(a.shape, lambda i: (0,) * a.ndim)
    return pl.pallas_call(
        _ffn_kernel,
        grid=(m // tm,),
        in_specs=[tx] + mod_specs + [resident(w_up), resident(w_down), full(ln_g), full(ln_b)],
        out_specs=tx,
        out_shape=jax.ShapeDtypeStruct(x1_flat.shape, F32),
        compiler_params=_cparams(("parallel",)),
        name="ffn_postnorm",
    )(x1_flat, *([mods] * (3 * FFN_SUB)), w_up, w_down, ln_g, ln_b)


def _pack_w_in(w):
    sizes = ([BRANCH_W] * 5 + [BRANCH_W, SSD_CONV_CH, N_HEADS, N_HEADS] + [BRANCH_W] * 3
             + [BRANCH_W, KV_HEADS * HEAD_DIM, KV_HEADS * HEAD_DIM] + [N_HEADS * D_MODEL])
    offs = np.concatenate([[0], np.cumsum(sizes)])
    (a_q, a_ff, a_fb, a_v, a_g, b_z, b_xbc, b_dtf, b_dtb, c_q, c_k, c_v, d_q, d_k, d_v, gate) = [
        w[:, int(offs[i]):int(offs[i + 1])] for i in range(len(sizes))]
    zeros = lambda n: jnp.zeros((w.shape[0], n), w.dtype)
    cols = [gate, _q_pad_lanes(d_q), b_xbc, b_z, a_q, a_ff, a_fb, a_v, a_g, c_q, c_k, c_v, d_k, d_v,
            b_dtf, b_dtb, zeros(128 - 2 * N_HEADS)]
    packed = jnp.concatenate(cols, axis=1)
    packed = jnp.concatenate([packed, zeros(P_COLS - packed.shape[1])], axis=1)
    return packed.astype(BF16)


def _pad_rows_d(wd):
    return _q_pad_lanes(wd.T).T


def kernel(x, c, ctx, c_ctx, ada_w, ada_b, w_in, hgrn_lb, hgrn_norm, ssd_conv_w, ssd_conv_b, ssd_dt_bias,
           ssd_a_log, ssd_d, ssd_norm, na_rpb, q_norm, k_norm, w_branch, w_out, ln1_g, ln1_b,
           ffn_w_up, ffn_w_down, ln2_g, ln2_b):
    bsz = x.shape[0]
    c8 = jnp.concatenate([c, c_ctx[None, :], jnp.zeros((8 - bsz - 1, D_MODEL), F32)], axis=0)
    mods = _ada_call(c8, ada_w, ada_b).reshape(DEPTH * 8 * 6, 1, D_MODEL)
    xa = jnp.concatenate([ctx, x], axis=1)
    row = lambda v: v.reshape(1, -1)
    for l in range(DEPTH):
        last = l == DEPTH - 1
        p = _inproj_call(xa.reshape(bsz * SEQ_ALL, D_MODEL), mods, _pack_w_in(w_in[l]), l)
        p3 = p.reshape(bsz, SEQ_ALL, P_COLS)
        hf, hb = _hgrn_call(p3, hgrn_lb[0], hgrn_lb[1], l)
        xact = _conv_call(p3, ssd_conv_w[l], ssd_conv_b[l])
        dt_rows = jnp.swapaxes(p3[:, :, COL_DT:COL_DT + 8], 1, 2).astype(F32)
        sf, sb = _ssd_call(xact, p3, dt_rows, ssd_dt_bias[l], ssd_a_log[l])
        yc = _na_call(p3, _na_bias_tables(na_rpb[l]))
        qz, kn, va = _qkprep_call(p3, q_norm[l], k_norm[l])
        yd = _gqa_call(qz, kn, va)
        wb = w_branch[l].astype(BF16)
        x1 = _merge_call(l, mods, xa, p3, hf, hb, sf, sb, xact, yc, yd,
                         row(hgrn_norm[l]), row(jnp.repeat(ssd_d[l], HEAD_DIM)), row(ssd_norm[l]),
                         wb[0], wb[1], wb[2], _pad_rows_d(wb[3]), w_out[l].astype(BF16),
                         row(ln1_g[l]), row(ln1_b[l]), with_ctx=not last)
        xa = _ffn_call(l, mods, x1.reshape(-1, D_MODEL), ffn_w_up[l].astype(BF16), ffn_w_down[l].astype(BF16),
                       row(ln2_g[l]), row(ln2_b[l]), with_ctx=not last).reshape(bsz, -1, D_MODEL)
    return xa
```

```python
import functools
import math

import jax
import jax.numpy as jnp
import numpy as np
from jax import lax
from jax.experimental import pallas as pl
from jax.experimental.pallas import tpu as pltpu

F32 = jnp.float32
BF16 = jnp.bfloat16
HIGHEST = lax.Precision.HIGHEST

D_MODEL = 1024
DEPTH = 2
GRID_W = 64
N_CTX = 256
N_LAT = 4096
SEQ_ALL = N_CTX + N_LAT
HEAD_DIM = 64
BRANCH_W = 256
N_HEADS = 4
SSD_STATE = 128
SSD_GROUPS = 2
SSD_CONV_K = 5
SSD_CONV_CH = BRANCH_W + 2 * SSD_GROUPS * SSD_STATE
KV_HEADS = 2
NA_WIN_R = 8
NA_WIN_C = 16
ROPE_BASE = 10000.0
LN_EPS = 1e-6
FFN_HIDDEN = 2816
ALPHA = (2.0 * DEPTH) ** 0.25

ROW_TILE = 256
N_TILES = SEQ_ALL // ROW_TILE
LAT_TILES = N_LAT // ROW_TILE
HGRN_CHUNK = 64
HGRN_SUB = 16
SSD_CHUNK = 128
NEG_BIG = -1e30
VMEM_LIMIT = 48 * 1024 * 1024

P_COLS = 8192
COL_GATE = 0
COL_DQ = 4096
COL_XBC = 4608
COL_Z = 5376
COL_AQ = 5632
COL_CQ = 6912
COL_DK = 7680
COL_DV = 7808
COL_DT = 7936


def _cparams(sem):
    return pltpu.CompilerParams(dimension_semantics=sem, vmem_limit_bytes=VMEM_LIMIT)


def _sigmoid(x):
    return 0.5 * jnp.tanh(0.5 * x) + 0.5


def _div(x, n):
    return lax.shift_right_logical(x, int(math.log2(n)))


def _mod(x, n):
    return x & (n - 1)


def _softplus(x):
    return jnp.maximum(x, 0.0) + jnp.log(1.0 + jnp.exp(-jnp.abs(x)))


def _layer_norm(x):
    mu = jnp.mean(x, axis=-1, keepdims=True)
    xc = x - mu
    var = jnp.mean(xc * xc, axis=-1, keepdims=True)
    return xc * lax.rsqrt(var + LN_EPS)


def _dot(a, b):
    return jnp.dot(a, b, preferred_element_type=F32)


def _dot_nt(a, b):
    return lax.dot_general(a, b, (((1,), (1,)), ((), ())), preferred_element_type=F32)


def _dot_hi(a, b):
    return jnp.dot(a, b, precision=HIGHEST, preferred_element_type=F32)


def _segment_mean(x, seg):
    hi = x.astype(BF16)
    lo = (x - hi.astype(F32)).astype(BF16)
    return _dot(hi, seg) + _dot(lo, seg)


def _ada_kernel(c_ref, w_ref, b_ref, o_ref):
    cv = c_ref[...]
    a = cv * _sigmoid(cv)
    o_ref[0] = _dot_hi(a, w_ref[0]) + b_ref[0]


def _ada_call(c8, ada_w, ada_b):
    n_l = ada_w.shape[0]
    return pl.pallas_call(
        _ada_kernel,
        grid=(n_l, 6),
        in_specs=[
            pl.BlockSpec((8, D_MODEL), lambda l, j: (0, 0)),
            pl.BlockSpec((1, D_MODEL, D_MODEL), lambda l, j: (l, 0, j)),
            pl.BlockSpec((1, 1, D_MODEL), lambda l, j: (l, 0, j)),
        ],
        out_specs=pl.BlockSpec((1, 8, D_MODEL), lambda l, j: (l, 0, j)),
        out_shape=jax.ShapeDtypeStruct((n_l, 8, 6 * D_MODEL), F32),
        compiler_params=_cparams(("arbitrary", "arbitrary")),
        name="ada_mod",
    )(c8, ada_w, ada_b.reshape(n_l, 1, 6 * D_MODEL))


def _mod_index(layer, which):
    def index(b, j):
        return (layer * 48 + jnp.where(j == 0, 4, b) * 6 + which, 0, 0)
    return index


def _flat_mod_spec(layer, which, sub_tiles, u, seq_tiles, first_tile):
    def index(i, *_):
        t = i * sub_tiles + u
        b = t // seq_tiles
        j = t % seq_tiles + first_tile
        return (layer * 48 + jnp.where(j == 0, 4, b) * 6 + which, 0, 0)
    return pl.BlockSpec((1, 1, D_MODEL), index)


PROJ_SUB = 4
PROJ_TN = 1024


def _inproj_kernel(x_ref, *refs):
    mod_refs, (w_ref, o_ref, h_ref) = refs[:2 * PROJ_SUB], refs[2 * PROJ_SUB:]

    @pl.when(pl.program_id(1) == 0)
    def _():
        for u in range(PROJ_SUB):
            rows = slice(u * ROW_TILE, (u + 1) * ROW_TILE)
            sh, sc = mod_refs[2 * u][0], mod_refs[2 * u + 1][0]
            h_ref[rows, :] = (_layer_norm(x_ref[rows, :]) * (1.0 + sc) + sh).astype(BF16)

    o_ref[...] = _dot(h_ref[...], w_ref[...]).astype(BF16)


def _inproj_call(xa_flat, mods, wp, layer):
    m = xa_flat.shape[0]
    tm = PROJ_SUB * ROW_TILE
    mod_specs = []
    for u in range(PROJ_SUB):
        mod_specs += [_flat_mod_spec(layer, 0, PROJ_SUB, u, N_TILES, 0),
                      _flat_mod_spec(layer, 1, PROJ_SUB, u, N_TILES, 0)]
    return pl.pallas_call(
        _inproj_kernel,
        grid=(m // tm, P_COLS // PROJ_TN),
        in_specs=[pl.BlockSpec((tm, D_MODEL), lambda i, n: (i, 0))] + mod_specs
                 + [pl.BlockSpec((D_MODEL, PROJ_TN), lambda i, n: (0, n))],
        out_specs=pl.BlockSpec((tm, PROJ_TN), lambda i, n: (i, n)),
        out_shape=jax.ShapeDtypeStruct((m, P_COLS), BF16),
        scratch_shapes=[pltpu.VMEM((tm, D_MODEL), BF16)],
        compiler_params=_cparams(("parallel", "arbitrary")),
        name="in_proj",
    )(xa_flat, *([mods] * (2 * PROJ_SUB)), wp)


CONV_PAD = 8


def _conv_kernel(x_ref, w_ref, b_ref, o_ref, xs_ref):
    zeros = jnp.zeros((CONV_PAD, 128), F32)
    xs_ref[0:CONV_PAD, :] = zeros
    xs_ref[CONV_PAD + SEQ_ALL:, :] = zeros
    xs_ref[CONV_PAD:CONV_PAD + SEQ_ALL, :] = x_ref[0].astype(F32)
    w = w_ref[...]
    row = lax.broadcasted_iota(jnp.int32, (ROW_TILE, 128), 0)
    half = SSD_CONV_K // 2
    for ci in range(N_TILES):
        t0 = ci * ROW_TILE
        acc = jnp.broadcast_to(b_ref[...], (ROW_TILE, 128))
        for k in range(SSD_CONV_K):
            j = k - half
            xk = xs_ref[CONV_PAD + t0 + j:CONV_PAD + t0 + j + ROW_TILE, :]
            if ci == 0 and j > 0:
                xk = jnp.where(row + j < ROW_TILE, xk, 0.0)
            if ci == 1 and j < 0:
                xk = jnp.where(row + j >= 0, xk, 0.0)
            acc = acc + xk * w[k:k + 1, :]
        o_ref[0, t0:t0 + ROW_TILE, :] = (acc * _sigmoid(acc)).astype(BF16)


def _conv_call(p3, conv_w, conv_b):
    bsz = p3.shape[0]
    n_ct = SSD_CONV_CH // 128
    return pl.pallas_call(
        _conv_kernel,
        grid=(bsz, n_ct),
        in_specs=[
            pl.BlockSpec((1, SEQ_ALL, 128), lambda b, c: (b, 0, COL_XBC // 128 + c)),
            pl.BlockSpec((SSD_CONV_K, 128), lambda b, c: (0, c)),
            pl.BlockSpec((1, 128), lambda b, c: (0, c)),
        ],
        out_specs=pl.BlockSpec((1, SEQ_ALL, 128), lambda b, c: (b, 0, c)),
        out_shape=jax.ShapeDtypeStruct((bsz, SEQ_ALL, SSD_CONV_CH), BF16),
        scratch_shapes=[pltpu.VMEM((SEQ_ALL + 2 * CONV_PAD, 128), F32)],
        compiler_params=_cparams(("parallel", "parallel")),
        name="ssd_conv",
    )(p3, conv_w, conv_b.reshape(1, SSD_CONV_CH))


def _fwd_tile(j):
    return j


def _bwd_tile(j):
    return jnp.where(j == 0, 0, N_TILES - j)


def _tri(n, upper):
    r = lax.broadcasted_iota(jnp.int32, (n, n), 0)
    c = lax.broadcasted_iota(jnp.int32, (n, n), 1)
    return (c >= r) if upper else (c <= r)


def _ssd_dir(rev, xa_ref, dtc_ref, dtr_ref, bias_r, alog_r, bias_c, alog_c, s_ref, o_ref):
    d = 1 if rev else 0
    dt_c = _softplus(dtc_ref[0].astype(F32) + bias_r)
    g_c = dt_c * (-jnp.exp(alog_r))
    dt_r = _softplus(dtr_ref[0] + bias_c)
    g_r = dt_r * (-jnp.exp(alog_c))
    n = SSD_CHUNK
    tril = _tri(n, False)
    triu = _tri(n, True)
    m_col = (triu if rev else tril).astype(F32)
    m_row = (tril if rev else triu).astype(F32)
    keep = triu if rev else tril
    lane = lax.broadcasted_iota(jnp.int32, (n, 128), 1)
    first = lane < HEAD_DIM
    n_chunks = ROW_TILE // n
    order = range(n_chunks - 1, -1, -1) if rev else range(n_chunks)
    for ci in order:
        c0 = ci * n
        cum_c = _dot_hi(m_col, g_c[c0:c0 + n, :])
        cum_r = _dot_hi(g_r[:, c0:c0 + n], m_row)
        end_row = 0 if rev else n - 1
        for g in range(SSD_GROUPS):
            bm = xa_ref[0, c0:c0 + n, BRANCH_W + g * SSD_STATE:BRANCH_W + (g + 1) * SSD_STATE]
            cm = xa_ref[0, c0:c0 + n, BRANCH_W + (SSD_GROUPS + g) * SSD_STATE:
                        BRANCH_W + (SSD_GROUPS + g + 1) * SSD_STATE]
            x2 = xa_ref[0, c0:c0 + n, g * 128:(g + 1) * 128].astype(F32)
            gmat = _dot_nt(cm, bm)
            col0 = 4 * d + 2 * g
            bc0 = cum_c[:, col0:col0 + 1]
            bc1 = cum_c[:, col0 + 1:col0 + 2]
            br0 = cum_r[col0:col0 + 1, :]
            br1 = cum_r[col0 + 1:col0 + 2, :]
            p0 = jnp.where(keep, jnp.exp(jnp.minimum(bc0 - br0, 0.0)), 0.0) * gmat
            p1 = jnp.where(keep, jnp.exp(jnp.minimum(bc1 - br1, 0.0)), 0.0) * gmat
            dt2 = jnp.where(first, dt_c[c0:c0 + n, col0:col0 + 1], dt_c[c0:c0 + n, col0 + 1:col0 + 2])
            v2 = x2 * dt2
            v2b = v2.astype(BF16)
            pv = jnp.where(first, _dot(p0.astype(BF16), v2b), _dot(p1.astype(BF16), v2b))
            bc2 = jnp.where(first, bc0, bc1)
            s_old = s_ref[g]
            qs = _dot(cm, s_old.astype(BF16))
            o_ref[0, c0:c0 + n, g * 128:(g + 1) * 128] = pv + jnp.exp(bc2) * qs
            e0 = cum_c[end_row:end_row + 1, col0:col0 + 1]
            e1 = cum_c[end_row:end_row + 1, col0 + 1:col0 + 2]
            end2 = jnp.where(first[0:1, :], e0, e1)
            wv = (v2 * jnp.exp(end2 - bc2)).astype(BF16)
            upd = lax.dot_general(bm, wv, (((0,), (0,)), ((), ())), preferred_element_type=F32)
            s_ref[g] = s_old * jnp.exp(end2) + upd


def _ssd_kernel(xa_f, xa_b, dtc_f, dtc_b, dtr_f, dtr_b, bias_r, alog_r, bias_c, alog_c,
                o_f, o_b, s_ref):
    @pl.when(pl.program_id(1) == 0)
    def _():
        s_ref[...] = jnp.zeros(s_ref.shape, F32)

    br, ar, bc, ac = bias_r[...], alog_r[...], bias_c[...], alog_c[...]
    _ssd_dir(False, xa_f, dtc_f, dtr_f, br, ar, bc, ac, s_ref.at[0], o_f)
    _ssd_dir(True, xa_b, dtc_b, dtr_b, br, ar, bc, ac, s_ref.at[1], o_b)


def _ssd_call(xact, p3, dt_rows, dt_bias, a_log):
    bsz = xact.shape[0]
    bias8 = dt_bias.reshape(8).astype(F32)
    alog8 = a_log.reshape(8).astype(F32)
    pad_row = lambda v: jnp.zeros((1, 128), F32).at[0, :8].set(v)
    col = lambda v: jnp.broadcast_to(v[:, None], (8, ROW_TILE))

    def spec_x(tile):
        return pl.BlockSpec((1, ROW_TILE, SSD_CONV_CH), lambda b, j: (b, tile(j), 0))

    def spec_dtc(tile):
        return pl.BlockSpec((1, ROW_TILE, 128), lambda b, j: (b, tile(j), COL_DT // 128))

    def spec_dtr(tile):
        return pl.BlockSpec((1, 8, ROW_TILE), lambda b, j: (b, 0, tile(j)))

    def spec_o(tile):
        return pl.BlockSpec((1, ROW_TILE, BRANCH_W), lambda b, j: (b, tile(j), 0))

    small_r = pl.BlockSpec((1, 128), lambda b, j: (0, 0))
    small_c = pl.BlockSpec((8, ROW_TILE), lambda b, j: (0, 0))
    out = jax.ShapeDtypeStruct((bsz, SEQ_ALL, BRANCH_W), F32)
    return pl.pallas_call(
        _ssd_kernel,
        grid=(bsz, N_TILES),
        in_specs=[spec_x(_fwd_tile), spec_x(_bwd_tile), spec_dtc(_fwd_tile), spec_dtc(_bwd_tile),
                  spec_dtr(_fwd_tile), spec_dtr(_bwd_tile), small_r, small_r, small_c, small_c],
        out_specs=[spec_o(_fwd_tile), spec_o(_bwd_tile)],
        out_shape=[out, out],
        scratch_shapes=[pltpu.VMEM((2, SSD_GROUPS, SSD_STATE, 128), F32)],
        compiler_params=_cparams(("parallel", "arbitrary")),
        name="ssd_scan",
    )(xact, xact, p3, p3, dt_rows, dt_rows, pad_row(bias8), pad_row(alog8), col(bias8), col(alog8))


def _hgrn_lower_bound(lb_ref, layer):
    p = lb_ref[...]
    e = jnp.exp(p - jnp.max(p, axis=0, keepdims=True))
    sm = e / jnp.sum(e, axis=0, keepdims=True)
    acc = sm[0:1, :]
    for i in range(1, layer + 1):
        acc = acc + sm[i:i + 1, :]
    return acc - sm[0:1, :]


LOG2E = 1.4426950408889634


def _cumsum_rows(x, rev, block):
    n = x.shape[0]
    r_i = lax.broadcasted_iota(jnp.int32, (n, n), 0)
    c_i = lax.broadcasted_iota(jnp.int32, (n, n), 1)
    tri = ((c_i >= r_i) if rev else (c_i <= r_i)) & (_div(r_i, block) == _div(c_i, block))
    tri = tri.astype(BF16)
    hi = x.astype(BF16)
    rest = x - hi.astype(F32)
    mid = rest.astype(BF16)
    lo = (rest - mid.astype(F32)).astype(BF16)
    return _dot(tri, hi) + _dot(tri, mid) + _dot(tri, lo)


def _hgrn_decay_logs(f, lb, rev):
    fg = lb + (1.0 - lb) * _sigmoid(f)
    cum2 = _cumsum_rows(jnp.log(fg) * LOG2E, rev, HGRN_CHUNK)
    return cum2, jnp.log(1.0 - fg) * LOG2E - cum2


def _hgrn_chunk(rev, q, v, cum2, lk2, st_ref, ones_seg, bd_mask):
    n, c = HGRN_CHUNK, HGRN_SUB
    row = lax.broadcasted_iota(jnp.int32, (n, BRANCH_W), 0)
    lane = lax.broadcasted_iota(jnp.int32, (n, BRANCH_W), 1)
    offset = (_mod(lane, HEAD_DIM) - row) if rev else (row - _mod(lane, HEAD_DIM))
    pos = _mod(row, c)

    es = []
    for dlt in range(c):
        if dlt == 0:
            arg = cum2 + lk2
        else:
            lks = pltpu.roll(lk2, (n - dlt) if rev else dlt, axis=0)
            ok = (pos + dlt < c) if rev else (pos >= dlt)
            arg = jnp.where(ok, cum2 + lks, NEG_BIG)
        es.append((q * jnp.exp2(arg)).astype(BF16))
    seg = _dot(jnp.concatenate(es, axis=0), ones_seg)
    sc = jnp.zeros((n, BRANCH_W), F32)
    for dlt in range(c):
        sc = jnp.where(offset == dlt, seg[dlt * n:(dlt + 1) * n, :], sc)

    cl_parts = []
    for i in range(n // c):
        blk_rows = cum2[i * c:(i + 1) * c, :]
        if rev:
            cl_parts.append(blk_rows - cum2[(i + 1) * c:(i + 1) * c + 1, :] if i < n // c - 1 else blk_rows)
        else:
            cl_parts.append(blk_rows - cum2[i * c - 1:i * c, :] if i > 0 else blk_rows)
    qe = (q * jnp.exp2(jnp.concatenate(cl_parts, axis=0))).astype(BF16)
    parts = []
    for i in range(n // c):
        if (rev and i == n // c - 1) or (not rev and i == 0):
            parts.append(jnp.zeros((c, BRANCH_W), F32))
            continue
        if rev:
            ref = cum2[(i + 1) * c:(i + 1) * c + 1, :]
            valid = row >= (i + 1) * c
        else:
            ref = cum2[i * c - 1:i * c, :]
            valid = row < i * c
        rhs = jnp.exp2(jnp.where(valid, lk2 + ref, NEG_BIG))
        rhs_bd = jnp.where(bd_mask, jnp.concatenate([rhs] * N_HEADS, axis=0), 0.0).astype(BF16)
        parts.append(_dot_nt(qe[i * c:(i + 1) * c, :], rhs_bd))
    sc = sc + jnp.concatenate(parts, axis=0)
    v_bd = jnp.where(bd_mask, jnp.concatenate([v] * N_HEADS, axis=0), 0.0).astype(BF16)
    o = _dot(sc.astype(BF16), v_bd)

    st = st_ref[...]
    o = o + _dot_nt((q * jnp.exp2(cum2)).astype(BF16), st.astype(BF16))
    end = cum2[0:1, :] if rev else cum2[n - 1:n, :]
    kp = jnp.exp2(lk2 + end)
    upd = _dot(v.T.astype(BF16), kp.astype(BF16))
    st_ref[...] = st * jnp.exp2(end) + jnp.where(bd_mask, upd, 0.0)
    return o


def _hgrn_kernel(layer, q_f, f_f, v_f, q_b, f_b, v_b, lbf_ref, lbb_ref, o_f, o_b, st_ref, cum_ref, lk_ref):
    @pl.when(pl.program_id(1) == 0)
    def _():
        st_ref[...] = jnp.zeros(st_ref.shape, F32)

    for d, (f_ref, lb_ref) in enumerate(((f_f, lbf_ref), (f_b, lbb_ref))):
        cum2, lk2 = _hgrn_decay_logs(f_ref[0].astype(F32), _hgrn_lower_bound(lb_ref, layer), d == 1)
        cum_ref[d] = cum2
        lk_ref[d] = lk2
    rb = _div(lax.broadcasted_iota(jnp.int32, (BRANCH_W, BRANCH_W), 0), HEAD_DIM)
    cb = _div(lax.broadcasted_iota(jnp.int32, (BRANCH_W, BRANCH_W), 1), HEAD_DIM)
    bd_mask = rb == cb
    ones_seg = bd_mask.astype(BF16)
    n_chunks = ROW_TILE // HGRN_CHUNK

    def body(i, carry):
        cf = pl.multiple_of(i * HGRN_CHUNK, HGRN_CHUNK)
        cbk = pl.multiple_of((n_chunks - 1 - i) * HGRN_CHUNK, HGRN_CHUNK)
        sl_f = pl.ds(cf, HGRN_CHUNK)
        sl_b = pl.ds(cbk, HGRN_CHUNK)
        o_f[0, sl_f, :] = _hgrn_chunk(False, q_f[0, sl_f, :].astype(F32), v_f[0, sl_f, :].astype(F32),
                                      cum_ref[0, sl_f, :], lk_ref[0, sl_f, :], st_ref.at[0], ones_seg, bd_mask)
        o_b[0, sl_b, :] = _hgrn_chunk(True, q_b[0, sl_b, :].astype(F32), v_b[0, sl_b, :].astype(F32),
                                      cum_ref[1, sl_b, :], lk_ref[1, sl_b, :], st_ref.at[1], ones_seg, bd_mask)
        return carry

    lax.fori_loop(0, n_chunks, body, 0, unroll=2)


def _hgrn_call(p3, lb_fwd, lb_bwd, layer):
    bsz = p3.shape[0]
    base = COL_AQ // BRANCH_W

    def spec(tile, col):
        return pl.BlockSpec((1, ROW_TILE, BRANCH_W), lambda b, j: (b, tile(j), col))

    lb_spec = pl.BlockSpec((DEPTH, BRANCH_W), lambda b, j: (0, 0))
    out = jax.ShapeDtypeStruct((bsz, SEQ_ALL, BRANCH_W), F32)
    return pl.pallas_call(
        functools.partial(_hgrn_kernel, layer),
        grid=(bsz, N_TILES),
        in_specs=[spec(_fwd_tile, base), spec(_fwd_tile, base + 1), spec(_fwd_tile, base + 3),
                  spec(_bwd_tile, base), spec(_bwd_tile, base + 2), spec(_bwd_tile, base + 3),
                  lb_spec, lb_spec],
        out_specs=[spec(_fwd_tile, 0), spec(_bwd_tile, 0)],
        out_shape=[out, out],
        scratch_shapes=[pltpu.VMEM((2, BRANCH_W, BRANCH_W), F32),
                        pltpu.VMEM((2, ROW_TILE, BRANCH_W), F32),
                        pltpu.VMEM((2, ROW_TILE, BRANCH_W), F32)],
        compiler_params=_cparams(("parallel", "arbitrary")),
        name="hgrn_scan",
    )(p3, p3, p3, p3, p3, p3, lb_fwd, lb_bwd)


def _rope(xn, cos, sin):
    w = xn.shape[1]
    lane = lax.broadcasted_iota(jnp.int32, xn.shape, 1)
    nxt = pltpu.roll(xn, w - HEAD_DIM // 4, axis=1)
    prv = pltpu.roll(xn, HEAD_DIM // 4, axis=1)
    partner = jnp.where(_mod(lane, HEAD_DIM // 2) < HEAD_DIM // 4, nxt, prv)
    return xn * cos + partner * sin


def _qkprep_kernel(q_ref, k_ref, v_ref, cq_ref, sq_ref, ck_ref, sk_ref, wq_ref, wk_ref, segq_ref, segk_ref,
                   qo_ref, ko_ref, vo_ref):
    q = q_ref[0].astype(F32)
    k = k_ref[0].astype(F32)
    qn = q * lax.rsqrt(_segment_mean(q * q, segq_ref[...]) + LN_EPS) * wq_ref[...]
    kn = k * lax.rsqrt(_segment_mean(k * k, segk_ref[...]) + LN_EPS) * wk_ref[...]
    qo_ref[0] = (_rope(qn, cq_ref[...], sq_ref[...]) * (HEAD_DIM ** -0.5 * LOG2E)).astype(BF16)
    ko_ref[0] = _rope(kn, ck_ref[...], sk_ref[...]).astype(BF16)
    v = v_ref[0].astype(F32)
    lane = lax.broadcasted_iota(jnp.int32, v.shape, 1)
    vo_ref[0] = jnp.concatenate([jnp.where(lane < HEAD_DIM, v, 1.0), jnp.where(lane < HEAD_DIM, 1.0, v)],
                                axis=1).astype(BF16)


def _rope_tables():
    t = np.arange(N_LAT)
    nf = HEAD_DIM // 4
    inv_freq = ROPE_BASE ** (-np.arange(nf, dtype=np.float32) / nf)
    ang_r = (t // GRID_W).astype(np.float32)[:, None] * inv_freq
    ang_c = (t % GRID_W).astype(np.float32)[:, None] * inv_freq
    ang = jnp.asarray(np.concatenate([ang_r, ang_r, ang_c, ang_c], axis=1))
    cos = jnp.cos(ang)
    sin = jnp.sin(ang)
    sign = np.tile(np.concatenate([-np.ones(nf, np.float32), np.ones(nf, np.float32)]), 2)
    sin = sin * sign
    cos = jnp.concatenate([jnp.ones((N_CTX, HEAD_DIM), F32), cos], axis=0)
    sin = jnp.concatenate([jnp.zeros((N_CTX, HEAD_DIM), F32), sin], axis=0)
    return cos, sin


def _q_pad_lanes(a):
    parts = []
    zero = jnp.zeros(a.shape[:-1] + (HEAD_DIM,), a.dtype)
    for h in range(N_HEADS):
        ah = a[..., h * HEAD_DIM:(h + 1) * HEAD_DIM]
        parts += [ah, zero] if h // 2 == 0 else [zero, ah]
    return jnp.concatenate(parts, axis=-1)


def _seg_matrix(width):
    idx = np.arange(width) // HEAD_DIM
    return jnp.asarray((idx[:, None] == idx[None, :]).astype(np.float32) / HEAD_DIM, dtype=BF16)


def _qkprep_call(p3, q_norm, k_norm):
    bsz = p3.shape[0]
    cos, sin = _rope_tables()
    cq = _q_pad_lanes(jnp.tile(cos, (1, N_HEADS)))
    sq = _q_pad_lanes(jnp.tile(sin, (1, N_HEADS)))
    ck = jnp.tile(cos, (1, KV_HEADS))
    sk = jnp.tile(sin, (1, KV_HEADS))
    wq = _q_pad_lanes(jnp.tile(q_norm, N_HEADS)[None, :])
    wk = jnp.tile(k_norm, KV_HEADS)[None, :]
    qw, kw = N_HEADS * 128, KV_HEADS * HEAD_DIM
    tile = lambda w, col: pl.BlockSpec((1, ROW_TILE, w), lambda j, b: (b, j, col))
    tab = lambda w: pl.BlockSpec((ROW_TILE, w), lambda j, b: (j, 0))
    full = lambda r, w: pl.BlockSpec((r, w), lambda j, b: (0, 0))
    return pl.pallas_call(
        _qkprep_kernel,
        grid=(N_TILES, bsz),
        in_specs=[tile(qw, COL_DQ // qw), tile(kw, COL_DK // kw), tile(kw, COL_DV // kw),
                  tab(qw), tab(qw), tab(kw), tab(kw),
                  full(1, qw), full(1, kw), full(qw, qw), full(kw, kw)],
        out_specs=[tile(qw, 0), tile(kw, 0), tile(KV_HEADS * kw, 0)],
        out_shape=[jax.ShapeDtypeStruct((bsz, SEQ_ALL, qw), BF16),
                   jax.ShapeDtypeStruct((bsz, SEQ_ALL, kw), BF16),
                   jax.ShapeDtypeStruct((bsz, SEQ_ALL, KV_HEADS * kw), BF16)],
        compiler_params=_cparams(("parallel", "parallel")),
        name="gqa_qk_prep",
    )(p3, p3, p3, cq, sq, ck, sk, wq, wk, _seg_matrix(qw), _seg_matrix(kw))


GQA_KCHUNK = 1024


def _gqa_attend(q2, k_ref, v_ref, g, chunks):
    m = acc = None
    for (k0, kn) in chunks:
        kc = k_ref[0, k0:k0 + kn, :]
        vc = v_ref[0, k0:k0 + kn, g * 128:(g + 1) * 128]
        s = _dot_nt(q2, kc)
        mc = jnp.max(s, axis=-1, keepdims=True)
        if m is None:
            m = mc
            acc = _dot(jnp.exp2(s - m).astype(BF16), vc)
        else:
            mn = jnp.maximum(m, mc)
            acc = jnp.exp2(m - mn) * acc + _dot(jnp.exp2(s - mn).astype(BF16), vc)
            m = mn
    sum_lane = (1 - g) * HEAD_DIM
    return acc / acc[:, sum_lane:sum_lane + 1]


def _gqa_kernel(q_ref, k_ref, v_ref, o_ref):
    lane = lax.broadcasted_iota(jnp.int32, (2 * ROW_TILE, 128), 1)
    ctx_chunks = [(0, N_CTX)]
    all_chunks = ctx_chunks + [(N_CTX + i * GQA_KCHUNK, GQA_KCHUNK) for i in range(N_LAT // GQA_KCHUNK)]

    def run(chunks):
        for g in range(KV_HEADS):
            q2 = jnp.concatenate([q_ref[0, :, (2 * g) * 128:(2 * g + 1) * 128],
                                  q_ref[0, :, (2 * g + 1) * 128:(2 * g + 2) * 128]], axis=0)
            o2 = _gqa_attend(q2, k_ref, v_ref, g, chunks)
            o2 = jnp.where(_div(lane, HEAD_DIM) == g, o2, 0.0).astype(BF16)
            o_ref[0, :, (2 * g) * 128:(2 * g + 1) * 128] = o2[:ROW_TILE]
            o_ref[0, :, (2 * g + 1) * 128:(2 * g + 2) * 128] = o2[ROW_TILE:]

    @pl.when(pl.program_id(1) == 0)
    def _():
        run(ctx_chunks)

    @pl.when(pl.program_id(1) > 0)
    def _():
        run(all_chunks)


def _gqa_call(qz, kn, va):
    bsz = qz.shape[0]
    qw, kw = N_HEADS * 128, KV_HEADS * HEAD_DIM
    return pl.pallas_call(
        _gqa_kernel,
        grid=(bsz, N_TILES),
        in_specs=[
            pl.BlockSpec((1, ROW_TILE, qw), lambda b, j: (b, j, 0)),
            pl.BlockSpec((1, SEQ_ALL, kw), lambda b, j: (b, 0, 0)),
            pl.BlockSpec((1, SEQ_ALL, KV_HEADS * kw), lambda b, j: (b, 0, 0)),
        ],
        out_specs=pl.BlockSpec((1, ROW_TILE, qw), lambda b, j: (b, j, 0)),
        out_shape=jax.ShapeDtypeStruct((bsz, SEQ_ALL, qw), BF16),
        compiler_params=_cparams(("parallel", "parallel")),
        name="gqa_attention",
    )(qz, kn, va)


NA_ROWS = ROW_TILE // GRID_W
NA_KTILES = 3
NA_KW = NA_KTILES * ROW_TILE


def _na_bias_tables(rpb):
    rows = N_LAT // GRID_W
    tabs = []
    for jb in (0, 1, LAT_TILES - 1):
        base = min(max(jb - 1, 0), LAT_TILES - NA_KTILES)
        r = jb * NA_ROWS + np.arange(NA_ROWS)
        cq = np.arange(GRID_W)
        kr = base * NA_ROWS + np.arange(NA_KTILES * NA_ROWS)
        kc = np.arange(GRID_W)
        start_r = np.clip(r - NA_WIN_R // 2, 0, rows - NA_WIN_R)
        start_c = np.clip(cq - NA_WIN_C // 2, 0, GRID_W - NA_WIN_C)
        ok_r = (kr[None, :] >= start_r[:, None]) & (kr[None, :] < start_r[:, None] + NA_WIN_R)
        ok_c = (kc[None, :] >= start_c[:, None]) & (kc[None, :] < start_c[:, None] + NA_WIN_C)
        drow = np.clip(kr[None, :] - r[:, None] + NA_WIN_R - 1, 0, 2 * NA_WIN_R - 2)
        dcol = np.clip(kc[None, :] - cq[:, None] + NA_WIN_C - 1, 0, 2 * NA_WIN_C - 2)
        pick_r = (drow[:, :, None] == np.arange(2 * NA_WIN_R - 1)).astype(np.float32)
        pick_c = (dcol[:, :, None] == np.arange(2 * NA_WIN_C - 1)).astype(np.float32)
        rows_sel = jnp.einsum('amr,hrd->hamd', pick_r, rpb, precision=HIGHEST)
        bias = jnp.einsum('hamd,ckd->hacmk', rows_sel, pick_c, precision=HIGHEST)
        ok = ok_r[:, None, :, None] & ok_c[None, :, None, :]
        bias = jnp.where(ok[None], bias * LOG2E, NEG_BIG)
        tabs.append(bias.reshape(N_HEADS, ROW_TILE, NA_KW))
    return jnp.stack(tabs, axis=0).astype(F32)


def _na_kernel(q_ref, k0_ref, k1_ref, k2_ref, kc_ref, v0_ref, v1_ref, v2_ref, vc_ref, bias_ref, o_ref):
    lane = lax.broadcasted_iota(jnp.int32, (ROW_TILE, BRANCH_W), 1)
    blk = _div(lane, HEAD_DIM)
    q = q_ref[0].astype(F32) * (HEAD_DIM ** -0.5 * LOG2E)

    def attend(key_refs, val_refs, bias_of):
        acc = jnp.zeros((ROW_TILE, BRANCH_W), F32)
        for h in range(N_HEADS):
            qz = jnp.where(blk == h, q, 0.0).astype(BF16)
            s = [_dot_nt(qz, kr[0]) for kr in key_refs]
            s = [si if bias_of(h, i) is None else si + bias_of(h, i) for i, si in enumerate(s)]
            m = functools.reduce(jnp.maximum, [jnp.max(si, axis=-1, keepdims=True) for si in s])
            p = [jnp.exp2(si - m) for si in s]
            l = functools.reduce(jnp.add, [jnp.sum(pi, axis=-1, keepdims=True) for pi in p])
            o = functools.reduce(jnp.add, [_dot(pi.astype(BF16), vr[0]) for pi, vr in zip(p, val_refs)])
            acc = jnp.where(blk == h, o / l, acc)
        o_ref[0] = acc.astype(BF16)

    @pl.when(pl.program_id(1) == 0)
    def _():
        attend([kc_ref], [vc_ref], lambda h, i: None)

    @pl.when(pl.program_id(1) > 0)
    def _():
        def bias_of(h, i):
            if i == NA_KTILES:
                return None
            return bias_ref[0, h, :, i * ROW_TILE:(i + 1) * ROW_TILE]
        attend([k0_ref, k1_ref, k2_ref, kc_ref], [v0_ref, v1_ref, v2_ref, vc_ref], bias_of)


def _na_call(p3, bias_tabs):
    bsz = p3.shape[0]
    cq = COL_CQ // BRANCH_W

    def key_tile(j, m):
        return 1 + jnp.clip(j - 2, 0, LAT_TILES - NA_KTILES) + m

    def spec_k(col, m):
        return pl.BlockSpec((1, ROW_TILE, BRANCH_W), lambda b, j: (b, key_tile(j, m), col))

    def spec_c(col):
        return pl.BlockSpec((1, ROW_TILE, BRANCH_W), lambda b, j: (b, 0, col))

    def variant(j):
        return jnp.where(j <= 1, 0, jnp.where(j == LAT_TILES, 2, 1))

    return pl.pallas_call(
        _na_kernel,
        grid=(bsz, N_TILES),
        in_specs=[pl.BlockSpec((1, ROW_TILE, BRANCH_W), lambda b, j: (b, j, cq)),
                  spec_k(cq + 1, 0), spec_k(cq + 1, 1), spec_k(cq + 1, 2), spec_c(cq + 1),
                  spec_k(cq + 2, 0), spec_k(cq + 2, 1), spec_k(cq + 2, 2), spec_c(cq + 2),
                  pl.BlockSpec((1, N_HEADS, ROW_TILE, NA_KW), lambda b, j: (variant(j), 0, 0, 0))],
        out_specs=pl.BlockSpec((1, ROW_TILE, BRANCH_W), lambda b, j: (b, j, 0)),
        out_shape=jax.ShapeDtypeStruct((bsz, SEQ_ALL, BRANCH_W), BF16),
        compiler_params=_cparams(("parallel", "parallel")),
        name="na_attention",
    )(p3, p3, p3, p3, p3, p3, p3, p3, p3, bias_tabs)


def _merge_kernel(hf_ref, hb_ref, ag_ref, sf_ref, sb_ref, xs_ref, z_ref, yc_ref, yd_ref,
                  g0_ref, g1_ref, g2_ref, g3_ref, x_ref, gate_ref,
                  hn_w, dsk, sn_w, seg_ref, wa, wb, wc, wd, wo, ln_g, ln_b, o_ref):
    oa = hf_ref[0] + hb_ref[0]
    ag = ag_ref[0].astype(F32)
    oa = oa * lax.rsqrt(_segment_mean(oa * oa, seg_ref[...]) + LN_EPS) * hn_w[...]
    br_a = oa * (ag * _sigmoid(ag))
    z = z_ref[0].astype(F32)
    yb = (sf_ref[0] + sb_ref[0] + xs_ref[0].astype(F32) * dsk[...]) * (z * _sigmoid(z))
    br_b = yb * lax.rsqrt(jnp.mean(yb * yb, axis=-1, keepdims=True) + LN_EPS) * sn_w[...]
    merged = (1.0 + jnp.tanh(g0_ref[0].astype(F32))) * _dot(br_a.astype(BF16), wa[...])
    merged += (1.0 + jnp.tanh(g1_ref[0].astype(F32))) * _dot(br_b.astype(BF16), wb[...])
    merged += (1.0 + jnp.tanh(g2_ref[0].astype(F32))) * _dot(yc_ref[0], wc[...])
    merged += (1.0 + jnp.tanh(g3_ref[0].astype(F32))) * _dot(yd_ref[0], wd[...])
    y = _dot(merged.astype(BF16), wo[...])
    o_ref[0] = _layer_norm(ALPHA * x_ref[0] + gate_ref[0] * y) * ln_g[...] + ln_b[...]


def _merge_call(layer, mods, xa, p3, hf, hb, sf, sb, xact, yc, yd, hn_w, dsk, sn_w, wa, wb, wc, wd, wo,
                ln_g, ln_b, with_ctx):
    bsz = xa.shape[0]
    t0 = 0 if with_ctx else 1
    tiles = N_TILES - t0
    t256 = lambda col: pl.BlockSpec((1, ROW_TILE, BRANCH_W), lambda b, j: (b, j + t0, col))
    t512 = pl.BlockSpec((1, ROW_TILE, 512), lambda b, j: (b, j + t0, 0))
    tgate = lambda n: pl.BlockSpec((1, ROW_TILE, D_MODEL), lambda b, j: (b, j + t0, COL_GATE // D_MODEL + n))
    tx = pl.BlockSpec((1, ROW_TILE, D_MODEL), lambda b, j: (b, j + t0, 0))
    gate = pl.BlockSpec((1, 1, D_MODEL), lambda b, j: _mod_index(layer, 2)(b, j + t0))
    full = lambda a: pl.BlockSpec(a.shape, lambda b, j: (0,) * a.ndim)
    seg = _seg_matrix(BRANCH_W)
    consts = [hn_w, dsk, sn_w, seg, wa, wb, wc, wd, wo, ln_g, ln_b]
    return pl.pallas_call(
        _merge_kernel,
        grid=(bsz, tiles),
        in_specs=[t256(0), t256(0), t256(COL_AQ // BRANCH_W + 4), t256(0), t256(0), t256(0),
                  t256(COL_Z // BRANCH_W), t256(0), t512,
                  tgate(0), tgate(1), tgate(2), tgate(3), tx, gate]
                 + [full(a) for a in consts],
        out_specs=pl.BlockSpec((1, ROW_TILE, D_MODEL), lambda b, j: (b, j, 0)),
        out_shape=jax.ShapeDtypeStruct((bsz, tiles * ROW_TILE, D_MODEL), F32),
        compiler_params=_cparams(("parallel", "parallel")),
        name="merge_postnorm",
    )(hf, hb, p3, sf, sb, xact, p3, yc, yd, p3, p3, p3, p3, xa, mods, *consts)


FFN_CHUNK = 256


FFN_SUB = 2


def _ffn_kernel(x_ref, *refs):
    mod_refs, (wu_ref, wd_ref, ln_g, ln_b, o_ref) = refs[:3 * FFN_SUB], refs[3 * FFN_SUB:]
    tm = FFN_SUB * ROW_TILE
    x = x_ref[...]
    ln = _layer_norm(x)
    h = jnp.concatenate(
        [ln[u * ROW_TILE:(u + 1) * ROW_TILE] * (1.0 + mod_refs[3 * u + 1][0]) + mod_refs[3 * u][0]
         for u in range(FFN_SUB)], axis=0).astype(BF16)
    acc = jnp.zeros((tm, D_MODEL), F32)
    for c in range(FFN_HIDDEN // FFN_CHUNK):
        c0 = c * FFN_CHUNK
        g = _dot(h, wu_ref[:, c0:c0 + FFN_CHUNK])
        u = _dot(h, wu_ref[:, FFN_HIDDEN + c0:FFN_HIDDEN + c0 + FFN_CHUNK])
        a = (g * _sigmoid(g) * u).astype(BF16)
        acc = acc + _dot(a, wd_ref[c0:c0 + FFN_CHUNK, :])
    for u in range(FFN_SUB):
        rows = slice(u * ROW_TILE, (u + 1) * ROW_TILE)
        y = ALPHA * x[rows] + mod_refs[3 * u + 2][0] * acc[rows]
        o_ref[rows, :] = _layer_norm(y) * ln_g[...] + ln_b[...]


def _ffn_call(layer, mods, x1_flat, w_up, w_down, ln_g, ln_b, with_ctx):
    m = x1_flat.shape[0]
    tm = FFN_SUB * ROW_TILE
    seq_tiles, first = (N_TILES, 0) if with_ctx else (LAT_TILES, 1)
    mod_specs = [_flat_mod_spec(layer, 3 + which, FFN_SUB, u, seq_tiles, first)
                 for u in range(FFN_SUB) for which in range(3)]
    tx = pl.BlockSpec((tm, D_MODEL), lambda i: (i, 0))
    resident = lambda a: pl.BlockSpec(a.shape, lambda i: (0,) * a.ndim, pipeline_mode=pl.Buffered(1))
    full = lambda a: pl.BlockSpec(a.shape, lambda i: (0,) * a.ndim)
    return pl.pallas_call(
        _ffn_kernel,
        grid=(m // tm,),
        in_specs=[tx] + mod_specs + [resident(w_up), resident(w_down), full(ln_g), full(ln_b)],
        out_specs=tx,
        out_shape=jax.ShapeDtypeStruct(x1_flat.shape, F32),
        compiler_params=_cparams(("parallel",)),
        name="ffn_postnorm",
    )(x1_flat, *([mods] * (3 * FFN_SUB)), w_up, w_down, ln_g, ln_b)


def _pack_w_in(w):
    sizes = ([BRANCH_W] * 5 + [BRANCH_W, SSD_CONV_CH, N_HEADS, N_HEADS] + [BRANCH_W] * 3
             + [BRANCH_W, KV_HEADS * HEAD_DIM, KV_HEADS * HEAD_DIM] + [N_HEADS * D_MODEL])
    offs = np.concatenate([[0], np.cumsum(sizes)])
    (a_q, a_ff, a_fb, a_v, a_g, b_z, b_xbc, b_dtf, b_dtb, c_q, c_k, c_v, d_q, d_k, d_v, gate) = [
        w[:, int(offs[i]):int(offs[i + 1])] for i in range(len(sizes))]
    zeros = lambda n: jnp.zeros((w.shape[0], n), w.dtype)
    cols = [0.5 * gate, _q_pad_lanes(d_q), b_xbc, b_z, a_q, a_ff, a_fb, a_v, a_g, c_q, c_k, c_v, d_k, d_v,
            b_dtf, b_dtb, zeros(128 - 2 * N_HEADS)]
    packed = jnp.concatenate(cols, axis=1)
    packed = jnp.concatenate([packed, zeros(P_COLS - packed.shape[1])], axis=1)
    return packed.astype(BF16)


def _pad_rows_d(wd):
    return _q_pad_lanes(wd.T).T


def kernel(x, c, ctx, c_ctx, ada_w, ada_b, w_in, hgrn_lb, hgrn_norm, ssd_conv_w, ssd_conv_b, ssd_dt_bias,
           ssd_a_log, ssd_d, ssd_norm, na_rpb, q_norm, k_norm, w_branch, w_out, ln1_g, ln1_b,
           ffn_w_up, ffn_w_down, ln2_g, ln2_b):
    bsz = x.shape[0]
    c8 = jnp.concatenate([c, c_ctx[None, :], jnp.zeros((8 - bsz - 1, D_MODEL), F32)], axis=0)
    mods = _ada_call(c8, ada_w, ada_b).reshape(DEPTH * 8 * 6, 1, D_MODEL)
    xa = jnp.concatenate([ctx, x], axis=1)
    row = lambda v: v.reshape(1, -1)
    for l in range(DEPTH):
        last = l == DEPTH - 1
        p = _inproj_call(xa.reshape(bsz * SEQ_ALL, D_MODEL), mods, _pack_w_in(w_in[l]), l)
        p3 = p.reshape(bsz, SEQ_ALL, P_COLS)
        hf, hb = _hgrn_call(p3, hgrn_lb[0], hgrn_lb[1], l)
        xact = _conv_call(p3, ssd_conv_w[l], ssd_conv_b[l])
        dt_rows = jnp.swapaxes(p3[:, :, COL_DT:COL_DT + 8], 1, 2).astype(F32)
        sf, sb = _ssd_call(xact, p3, dt_rows, ssd_dt_bias[l], ssd_a_log[l])
        yc = _na_call(p3, _na_bias_tables(na_rpb[l]))
        qz, kn, va = _qkprep_call(p3, q_norm[l], k_norm[l])
        yd = _gqa_call(qz, kn, va)
        wb = w_branch[l].astype(BF16)
        x1 = _merge_call(l, mods, xa, p3, hf, hb, sf, sb, xact, yc, yd,
                         row(hgrn_norm[l]), row(jnp.repeat(ssd_d[l], HEAD_DIM)), row(ssd_norm[l]),
                         wb[0], wb[1], wb[2], _pad_rows_d(wb[3]), (0.5 * w_out[l]).astype(BF16),
                         row(ln1_g[l]), row(ln1_b[l]), with_ctx=not last)
        xa = _ffn_call(l, mods, x1.reshape(-1, D_MODEL), ffn_w_up[l].astype(BF16), ffn_w_down[l].astype(BF16),
                       row(ln2_g[l]), row(ln2_b[l]), with_ctx=not last).reshape(bsz, -1, D_MODEL)
    return xa
```

```python
import functools
import math

import jax
import jax.numpy as jnp
import numpy as np
from jax import lax
from jax.experimental import pallas as pl
from jax.experimental.pallas import tpu as pltpu

F32 = jnp.float32
BF16 = jnp.bfloat16
HIGHEST = lax.Precision.HIGHEST

D_MODEL = 1024
DEPTH = 2
GRID_W = 64
N_CTX = 256
N_LAT = 4096
SEQ_ALL = N_CTX + N_LAT
HEAD_DIM = 64
BRANCH_W = 256
N_HEADS = 4
SSD_STATE = 128
SSD_GROUPS = 2
SSD_CONV_K = 5
SSD_CONV_CH = BRANCH_W + 2 * SSD_GROUPS * SSD_STATE
KV_HEADS = 2
NA_WIN_R = 8
NA_WIN_C = 16
ROPE_BASE = 10000.0
LN_EPS = 1e-6
FFN_HIDDEN = 2816
ALPHA = (2.0 * DEPTH) ** 0.25

ROW_TILE = 256
N_TILES = SEQ_ALL // ROW_TILE
LAT_TILES = N_LAT // ROW_TILE
HGRN_CHUNK = 64
HGRN_SUB = 16
SSD_CHUNK = 128
NEG_BIG = -1e30
VMEM_LIMIT = 48 * 1024 * 1024

P_COLS = 8192
COL_AQ = 0
COL_Z = 1280
COL_XBC = 1536
COL_CQ = 2304
COL_DQ = 3072
COL_DK = 3584
COL_DV = 3712
COL_DT = 3840
COL_GATE = 4096


def _cparams(sem):
    return pltpu.CompilerParams(dimension_semantics=sem, vmem_limit_bytes=VMEM_LIMIT)


def _sigmoid(x):
    return 0.5 * jnp.tanh(0.5 * x) + 0.5


def _div(x, n):
    return lax.shift_right_logical(x, int(math.log2(n)))


def _mod(x, n):
    return x & (n - 1)


def _softplus(x):
    return jnp.maximum(x, 0.0) + jnp.log(1.0 + jnp.exp(-jnp.abs(x)))


def _layer_norm(x):
    mu = jnp.mean(x, axis=-1, keepdims=True)
    xc = x - mu
    var = jnp.mean(xc * xc, axis=-1, keepdims=True)
    return xc * lax.rsqrt(var + LN_EPS)


def _dot(a, b):
    return jnp.dot(a, b, preferred_element_type=F32)


def _dot_nt(a, b):
    return lax.dot_general(a, b, (((1,), (1,)), ((), ())), preferred_element_type=F32)


def _dot_hi(a, b):
    return jnp.dot(a, b, precision=HIGHEST, preferred_element_type=F32)


def _segment_mean(x, seg):
    hi = x.astype(BF16)
    lo = (x - hi.astype(F32)).astype(BF16)
    return _dot(hi, seg) + _dot(lo, seg)


def _ada_kernel(c_ref, w_ref, b_ref, o_ref):
    cv = c_ref[...]
    a = cv * _sigmoid(cv)
    o_ref[0] = _dot_hi(a, w_ref[0]) + b_ref[0]


def _ada_call(c8, ada_w, ada_b):
    n_l = ada_w.shape[0]
    return pl.pallas_call(
        _ada_kernel,
        grid=(n_l, 6),
        in_specs=[
            pl.BlockSpec((8, D_MODEL), lambda l, j: (0, 0)),
            pl.BlockSpec((1, D_MODEL, D_MODEL), lambda l, j: (l, 0, j)),
            pl.BlockSpec((1, 1, D_MODEL), lambda l, j: (l, 0, j)),
        ],
        out_specs=pl.BlockSpec((1, 8, D_MODEL), lambda l, j: (l, 0, j)),
        out_shape=jax.ShapeDtypeStruct((n_l, 8, 6 * D_MODEL), F32),
        compiler_params=_cparams(("arbitrary", "arbitrary")),
        name="ada_mod",
    )(c8, ada_w, ada_b.reshape(n_l, 1, 6 * D_MODEL))


def _mod_index(layer, which):
    def index(b, j):
        return (layer * 48 + jnp.where(j == 0, 4, b) * 6 + which, 0, 0)
    return index


def _flat_mod_spec(layer, which, sub_tiles, u, seq_tiles, first_tile):
    def index(i, *_):
        t = i * sub_tiles + u
        b = t // seq_tiles
        j = t % seq_tiles + first_tile
        return (layer * 48 + jnp.where(j == 0, 4, b) * 6 + which, 0, 0)
    return pl.BlockSpec((1, 1, D_MODEL), index)


PROJ_SUB = 2
PROJ_TN = 1024


def _inproj_kernel(x_ref, *refs):
    mod_refs, (w_ref, o_ref) = refs[:2 * PROJ_SUB], refs[2 * PROJ_SUB:]
    parts = []
    for u in range(PROJ_SUB):
        rows = slice(u * ROW_TILE, (u + 1) * ROW_TILE)
        sh, sc = mod_refs[2 * u][0], mod_refs[2 * u + 1][0]
        parts.append((_layer_norm(x_ref[rows, :]) * (1.0 + sc) + sh).astype(BF16))
    h = jnp.concatenate(parts, axis=0)
    for n in range(P_COLS // PROJ_TN):
        cols = slice(n * PROJ_TN, (n + 1) * PROJ_TN)
        o_ref[:, cols] = _dot(h, w_ref[:, cols]).astype(BF16)


def _inproj_call(xa_flat, mods, wp, layer):
    m = xa_flat.shape[0]
    tm = PROJ_SUB * ROW_TILE
    mod_specs = []
    for u in range(PROJ_SUB):
        mod_specs += [_flat_mod_spec(layer, 0, PROJ_SUB, u, N_TILES, 0),
                      _flat_mod_spec(layer, 1, PROJ_SUB, u, N_TILES, 0)]
    return pl.pallas_call(
        _inproj_kernel,
        grid=(m // tm,),
        in_specs=[pl.BlockSpec((tm, D_MODEL), lambda i: (i, 0))] + mod_specs
                 + [pl.BlockSpec((D_MODEL, P_COLS), lambda i: (0, 0), pipeline_mode=pl.Buffered(1))],
        out_specs=pl.BlockSpec((tm, P_COLS), lambda i: (i, 0)),
        out_shape=jax.ShapeDtypeStruct((m, P_COLS), BF16),
        compiler_params=_cparams(("parallel",)),
        name="in_proj",
    )(xa_flat, *([mods] * (2 * PROJ_SUB)), wp)


CONV_PAD = 8


def _conv_kernel(x_ref, w_ref, b_ref, o_ref, xs_ref):
    zeros = jnp.zeros((CONV_PAD, 128), F32)
    xs_ref[0:CONV_PAD, :] = zeros
    xs_ref[CONV_PAD + SEQ_ALL:, :] = zeros
    xs_ref[CONV_PAD:CONV_PAD + SEQ_ALL, :] = x_ref[0].astype(F32)
    w = w_ref[...]
    row = lax.broadcasted_iota(jnp.int32, (ROW_TILE, 128), 0)
    half = SSD_CONV_K // 2
    for ci in range(N_TILES):
        t0 = ci * ROW_TILE
        acc = jnp.broadcast_to(b_ref[...], (ROW_TILE, 128))
        for k in range(SSD_CONV_K):
            j = k - half
            xk = xs_ref[CONV_PAD + t0 + j:CONV_PAD + t0 + j + ROW_TILE, :]
            if ci == 0 and j > 0:
                xk = jnp.where(row + j < ROW_TILE, xk, 0.0)
            if ci == 1 and j < 0:
                xk = jnp.where(row + j >= 0, xk, 0.0)
            acc = acc + xk * w[k:k + 1, :]
        o_ref[0, t0:t0 + ROW_TILE, :] = (acc * _sigmoid(acc)).astype(BF16)


def _conv_call(p3, conv_w, conv_b):
    bsz = p3.shape[0]
    n_ct = SSD_CONV_CH // 128
    return pl.pallas_call(
        _conv_kernel,
        grid=(bsz, n_ct),
        in_specs=[
            pl.BlockSpec((1, SEQ_ALL, 128), lambda b, c: (b, 0, COL_XBC // 128 + c)),
            pl.BlockSpec((SSD_CONV_K, 128), lambda b, c: (0, c)),
            pl.BlockSpec((1, 128), lambda b, c: (0, c)),
        ],
        out_specs=pl.BlockSpec((1, SEQ_ALL, 128), lambda b, c: (b, 0, c)),
        out_shape=jax.ShapeDtypeStruct((bsz, SEQ_ALL, SSD_CONV_CH), BF16),
        scratch_shapes=[pltpu.VMEM((SEQ_ALL + 2 * CONV_PAD, 128), F32)],
        compiler_params=_cparams(("parallel", "parallel")),
        name="ssd_conv",
    )(p3, conv_w, conv_b.reshape(1, SSD_CONV_CH))


def _fwd_tile(j):
    return j


def _bwd_tile(j):
    return jnp.where(j == 0, 0, N_TILES - j)


def _tri(n, upper):
    r = lax.broadcasted_iota(jnp.int32, (n, n), 0)
    c = lax.broadcasted_iota(jnp.int32, (n, n), 1)
    return (c >= r) if upper else (c <= r)


def _ssd_dir(rev, xa_ref, dtc_ref, dtr_ref, bias_r, alog_r, bias_c, alog_c, s_ref, o_ref):
    d = 1 if rev else 0
    dt_c = _softplus(dtc_ref[0].astype(F32) + bias_r)
    g_c = dt_c * (-jnp.exp(alog_r))
    dt_r = _softplus(dtr_ref[0] + bias_c)
    g_r = dt_r * (-jnp.exp(alog_c))
    n = SSD_CHUNK
    tril = _tri(n, False)
    triu = _tri(n, True)
    m_col = (triu if rev else tril).astype(F32)
    m_row = (tril if rev else triu).astype(F32)
    keep = triu if rev else tril
    lane = lax.broadcasted_iota(jnp.int32, (n, 128), 1)
    first = lane < HEAD_DIM
    n_chunks = ROW_TILE // n
    order = range(n_chunks - 1, -1, -1) if rev else range(n_chunks)
    for ci in order:
        c0 = ci * n
        cum_c = _dot_hi(m_col, g_c[c0:c0 + n, :])
        cum_r = _dot_hi(g_r[:, c0:c0 + n], m_row)
        end_row = 0 if rev else n - 1
        for g in range(SSD_GROUPS):
            bm = xa_ref[0, c0:c0 + n, BRANCH_W + g * SSD_STATE:BRANCH_W + (g + 1) * SSD_STATE]
            cm = xa_ref[0, c0:c0 + n, BRANCH_W + (SSD_GROUPS + g) * SSD_STATE:
                        BRANCH_W + (SSD_GROUPS + g + 1) * SSD_STATE]
            x2 = xa_ref[0, c0:c0 + n, g * 128:(g + 1) * 128].astype(F32)
            gmat = _dot_nt(cm, bm)
            col0 = 4 * d + 2 * g
            bc0 = cum_c[:, col0:col0 + 1]
            bc1 = cum_c[:, col0 + 1:col0 + 2]
            br0 = cum_r[col0:col0 + 1, :]
            br1 = cum_r[col0 + 1:col0 + 2, :]
            p0 = jnp.where(keep, jnp.exp(jnp.minimum(bc0 - br0, 0.0)), 0.0) * gmat
            p1 = jnp.where(keep, jnp.exp(jnp.minimum(bc1 - br1, 0.0)), 0.0) * gmat
            dt2 = jnp.where(first, dt_c[c0:c0 + n, col0:col0 + 1], dt_c[c0:c0 + n, col0 + 1:col0 + 2])
            v2 = x2 * dt2
            v2b = v2.astype(BF16)
            pv = jnp.where(first, _dot(p0.astype(BF16), v2b), _dot(p1.astype(BF16), v2b))
            bc2 = jnp.where(first, bc0, bc1)
            s_old = s_ref[g]
            qs = _dot(cm, s_old.astype(BF16))
            o_ref[0, c0:c0 + n, g * 128:(g + 1) * 128] = pv + jnp.exp(bc2) * qs
            e0 = cum_c[end_row:end_row + 1, col0:col0 + 1]
            e1 = cum_c[end_row:end_row + 1, col0 + 1:col0 + 2]
            end2 = jnp.where(first[0:1, :], e0, e1)
            wv = (v2 * jnp.exp(end2 - bc2)).astype(BF16)
            upd = lax.dot_general(bm, wv, (((0,), (0,)), ((), ())), preferred_element_type=F32)
            s_ref[g] = s_old * jnp.exp(end2) + upd


def _ssd_kernel(xa_f, xa_b, dtc_f, dtc_b, dtr_f, dtr_b, bias_r, alog_r, bias_c, alog_c,
                o_f, o_b, s_ref):
    @pl.when(pl.program_id(1) == 0)
    def _():
        s_ref[...] = jnp.zeros(s_ref.shape, F32)

    br, ar, bc, ac = bias_r[...], alog_r[...], bias_c[...], alog_c[...]
    _ssd_dir(False, xa_f, dtc_f, dtr_f, br, ar, bc, ac, s_ref.at[0], o_f)
    _ssd_dir(True, xa_b, dtc_b, dtr_b, br, ar, bc, ac, s_ref.at[1], o_b)


def _ssd_call(xact, p3, dt_rows, dt_bias, a_log):
    bsz = xact.shape[0]
    bias8 = dt_bias.reshape(8).astype(F32)
    alog8 = a_log.reshape(8).astype(F32)
    pad_row = lambda v: jnp.zeros((1, 128), F32).at[0, :8].set(v)
    col = lambda v: jnp.broadcast_to(v[:, None], (8, ROW_TILE))

    def spec_x(tile):
        return pl.BlockSpec((1, ROW_TILE, SSD_CONV_CH), lambda b, j: (b, tile(j), 0))

    def spec_dtc(tile):
        return pl.BlockSpec((1, ROW_TILE, 128), lambda b, j: (b, tile(j), COL_DT // 128))

    def spec_dtr(tile):
        return pl.BlockSpec((1, 8, ROW_TILE), lambda b, j: (b, 0, tile(j)))

    def spec_o(tile):
        return pl.BlockSpec((1, ROW_TILE, BRANCH_W), lambda b, j: (b, tile(j), 0))

    small_r = pl.BlockSpec((1, 128), lambda b, j: (0, 0))
    small_c = pl.BlockSpec((8, ROW_TILE), lambda b, j: (0, 0))
    out = jax.ShapeDtypeStruct((bsz, SEQ_ALL, BRANCH_W), F32)
    return pl.pallas_call(
        _ssd_kernel,
        grid=(bsz, N_TILES),
        in_specs=[spec_x(_fwd_tile), spec_x(_bwd_tile), spec_dtc(_fwd_tile), spec_dtc(_bwd_tile),
                  spec_dtr(_fwd_tile), spec_dtr(_bwd_tile), small_r, small_r, small_c, small_c],
        out_specs=[spec_o(_fwd_tile), spec_o(_bwd_tile)],
        out_shape=[out, out],
        scratch_shapes=[pltpu.VMEM((2, SSD_GROUPS, SSD_STATE, 128), F32)],
        compiler_params=_cparams(("parallel", "arbitrary")),
        name="ssd_scan",
    )(xact, xact, p3, p3, dt_rows, dt_rows, pad_row(bias8), pad_row(alog8), col(bias8), col(alog8))


def _hgrn_lower_bound(lb_ref, layer):
    p = lb_ref[...]
    e = jnp.exp(p - jnp.max(p, axis=0, keepdims=True))
    sm = e / jnp.sum(e, axis=0, keepdims=True)
    acc = sm[0:1, :]
    for i in range(1, layer + 1):
        acc = acc + sm[i:i + 1, :]
    return acc - sm[0:1, :]


LOG2E = 1.4426950408889634


def _cumsum_rows(x, rev, block):
    n = x.shape[0]
    r_i = lax.broadcasted_iota(jnp.int32, (n, n), 0)
    c_i = lax.broadcasted_iota(jnp.int32, (n, n), 1)
    tri = ((c_i >= r_i) if rev else (c_i <= r_i)) & (_div(r_i, block) == _div(c_i, block))
    tri = tri.astype(BF16)
    hi = x.astype(BF16)
    rest = x - hi.astype(F32)
    mid = rest.astype(BF16)
    lo = (rest - mid.astype(F32)).astype(BF16)
    return _dot(tri, hi) + _dot(tri, mid) + _dot(tri, lo)


def _hgrn_decay_logs(f, lb, rev):
    fg = lb + (1.0 - lb) * _sigmoid(f)
    cum2 = _cumsum_rows(jnp.log(fg) * LOG2E, rev, HGRN_CHUNK)
    return cum2, jnp.log(1.0 - fg) * LOG2E - cum2


def _hgrn_chunk(rev, q, v, cum2, lk2, st_ref, ones_seg, bd_mask):
    n, c = HGRN_CHUNK, HGRN_SUB
    row = lax.broadcasted_iota(jnp.int32, (n, BRANCH_W), 0)
    lane = lax.broadcasted_iota(jnp.int32, (n, BRANCH_W), 1)
    offset = (_mod(lane, HEAD_DIM) - row) if rev else (row - _mod(lane, HEAD_DIM))
    pos = _mod(row, c)

    es = []
    for dlt in range(c):
        if dlt == 0:
            arg = cum2 + lk2
        else:
            lks = pltpu.roll(lk2, (n - dlt) if rev else dlt, axis=0)
            ok = (pos + dlt < c) if rev else (pos >= dlt)
            arg = jnp.where(ok, cum2 + lks, NEG_BIG)
        es.append((q * jnp.exp2(arg)).astype(BF16))
    seg = _dot(jnp.concatenate(es, axis=0), ones_seg)
    sc = jnp.zeros((n, BRANCH_W), F32)
    for dlt in range(c):
        sc = jnp.where(offset == dlt, seg[dlt * n:(dlt + 1) * n, :], sc)

    cl_parts = []
    for i in range(n // c):
        blk_rows = cum2[i * c:(i + 1) * c, :]
        if rev:
            cl_parts.append(blk_rows - cum2[(i + 1) * c:(i + 1) * c + 1, :] if i < n // c - 1 else blk_rows)
        else:
            cl_parts.append(blk_rows - cum2[i * c - 1:i * c, :] if i > 0 else blk_rows)
    qe = (q * jnp.exp2(jnp.concatenate(cl_parts, axis=0))).astype(BF16)
    parts = []
    for i in range(n // c):
        if (rev and i == n // c - 1) or (not rev and i == 0):
            parts.append(jnp.zeros((c, BRANCH_W), F32))
            continue
        if rev:
            ref = cum2[(i + 1) * c:(i + 1) * c + 1, :]
            valid = row >= (i + 1) * c
        else:
            ref = cum2[i * c - 1:i * c, :]
            valid = row < i * c
        rhs = jnp.exp2(jnp.where(valid, lk2 + ref, NEG_BIG))
        rhs_bd = jnp.where(bd_mask, jnp.concatenate([rhs] * N_HEADS, axis=0), 0.0).astype(BF16)
        parts.append(_dot_nt(qe[i * c:(i + 1) * c, :], rhs_bd))
    sc = sc + jnp.concatenate(parts, axis=0)
    v_bd = jnp.where(bd_mask, jnp.concatenate([v] * N_HEADS, axis=0), 0.0).astype(BF16)
    o = _dot(sc.astype(BF16), v_bd)

    st = st_ref[...]
    o = o + _dot_nt((q * jnp.exp2(cum2)).astype(BF16), st.astype(BF16))
    end = cum2[0:1, :] if rev else cum2[n - 1:n, :]
    kp = jnp.exp2(lk2 + end)
    upd = _dot(v.T.astype(BF16), kp.astype(BF16))
    st_ref[...] = st * jnp.exp2(end) + jnp.where(bd_mask, upd, 0.0)
    return o


def _hgrn_kernel(layer, q_f, f_f, v_f, q_b, f_b, v_b, lbf_ref, lbb_ref, o_f, o_b, st_ref, cum_ref, lk_ref):
    @pl.when(pl.program_id(1) == 0)
    def _():
        st_ref[...] = jnp.zeros(st_ref.shape, F32)

    for d, (f_ref, lb_ref) in enumerate(((f_f, lbf_ref), (f_b, lbb_ref))):
        cum2, lk2 = _hgrn_decay_logs(f_ref[0].astype(F32), _hgrn_lower_bound(lb_ref, layer), d == 1)
        cum_ref[d] = cum2
        lk_ref[d] = lk2
    rb = _div(lax.broadcasted_iota(jnp.int32, (BRANCH_W, BRANCH_W), 0), HEAD_DIM)
    cb = _div(lax.broadcasted_iota(jnp.int32, (BRANCH_W, BRANCH_W), 1), HEAD_DIM)
    bd_mask = rb == cb
    ones_seg = bd_mask.astype(BF16)
    n_chunks = ROW_TILE // HGRN_CHUNK

    def body(i, carry):
        cf = pl.multiple_of(i * HGRN_CHUNK, HGRN_CHUNK)
        cbk = pl.multiple_of((n_chunks - 1 - i) * HGRN_CHUNK, HGRN_CHUNK)
        sl_f = pl.ds(cf, HGRN_CHUNK)
        sl_b = pl.ds(cbk, HGRN_CHUNK)
        o_f[0, sl_f, :] = _hgrn_chunk(False, q_f[0, sl_f, :].astype(F32), v_f[0, sl_f, :].astype(F32),
                                      cum_ref[0, sl_f, :], lk_ref[0, sl_f, :], st_ref.at[0], ones_seg, bd_mask)
        o_b[0, sl_b, :] = _hgrn_chunk(True, q_b[0, sl_b, :].astype(F32), v_b[0, sl_b, :].astype(F32),
                                      cum_ref[1, sl_b, :], lk_ref[1, sl_b, :], st_ref.at[1], ones_seg, bd_mask)
        return carry

    lax.fori_loop(0, n_chunks, body, 0, unroll=2)


def _hgrn_call(p3, lb_fwd, lb_bwd, layer):
    bsz = p3.shape[0]
    base = COL_AQ // BRANCH_W

    def spec(tile, col):
        return pl.BlockSpec((1, ROW_TILE, BRANCH_W), lambda b, j: (b, tile(j), col))

    lb_spec = pl.BlockSpec((DEPTH, BRANCH_W), lambda b, j: (0, 0))
    out = jax.ShapeDtypeStruct((bsz, SEQ_ALL, BRANCH_W), F32)
    return pl.pallas_call(
        functools.partial(_hgrn_kernel, layer),
        grid=(bsz, N_TILES),
        in_specs=[spec(_fwd_tile, base), spec(_fwd_tile, base + 1), spec(_fwd_tile, base + 3),
                  spec(_bwd_tile, base), spec(_bwd_tile, base + 2), spec(_bwd_tile, base + 3),
                  lb_spec, lb_spec],
        out_specs=[spec(_fwd_tile, 0), spec(_bwd_tile, 0)],
        out_shape=[out, out],
        scratch_shapes=[pltpu.VMEM((2, BRANCH_W, BRANCH_W), F32),
                        pltpu.VMEM((2, ROW_TILE, BRANCH_W), F32),
                        pltpu.VMEM((2, ROW_TILE, BRANCH_W), F32)],
        compiler_params=_cparams(("parallel", "arbitrary")),
        name="hgrn_scan",
    )(p3, p3, p3, p3, p3, p3, lb_fwd, lb_bwd)


def _rope(xn, cos, sin):
    w = xn.shape[1]
    lane = lax.broadcasted_iota(jnp.int32, xn.shape, 1)
    nxt = pltpu.roll(xn, w - HEAD_DIM // 4, axis=1)
    prv = pltpu.roll(xn, HEAD_DIM // 4, axis=1)
    partner = jnp.where(_mod(lane, HEAD_DIM // 2) < HEAD_DIM // 4, nxt, prv)
    return xn * cos + partner * sin


def _qkprep_kernel(q_ref, k_ref, v_ref, cq_ref, sq_ref, ck_ref, sk_ref, wq_ref, wk_ref, segq_ref, segk_ref,
                   qo_ref, ko_ref, vo_ref):
    q = q_ref[0].astype(F32)
    k = k_ref[0].astype(F32)
    qn = q * lax.rsqrt(_segment_mean(q * q, segq_ref[...]) + LN_EPS) * wq_ref[...]
    kn = k * lax.rsqrt(_segment_mean(k * k, segk_ref[...]) + LN_EPS) * wk_ref[...]
    qo_ref[0] = (_rope(qn, cq_ref[...], sq_ref[...]) * (HEAD_DIM ** -0.5 * LOG2E)).astype(BF16)
    ko_ref[0] = _rope(kn, ck_ref[...], sk_ref[...]).astype(BF16)
    v = v_ref[0].astype(F32)
    lane = lax.broadcasted_iota(jnp.int32, v.shape, 1)
    vo_ref[0] = jnp.concatenate([jnp.where(lane < HEAD_DIM, v, 1.0), jnp.where(lane < HEAD_DIM, 1.0, v)],
                                axis=1).astype(BF16)


def _rope_tables():
    t = np.arange(N_LAT)
    nf = HEAD_DIM // 4
    inv_freq = ROPE_BASE ** (-np.arange(nf, dtype=np.float32) / nf)
    ang_r = (t // GRID_W).astype(np.float32)[:, None] * inv_freq
    ang_c = (t % GRID_W).astype(np.float32)[:, None] * inv_freq
    ang = jnp.asarray(np.concatenate([ang_r, ang_r, ang_c, ang_c], axis=1))
    cos = jnp.cos(ang)
    sin = jnp.sin(ang)
    sign = np.tile(np.concatenate([-np.ones(nf, np.float32), np.ones(nf, np.float32)]), 2)
    sin = sin * sign
    cos = jnp.concatenate([jnp.ones((N_CTX, HEAD_DIM), F32), cos], axis=0)
    sin = jnp.concatenate([jnp.zeros((N_CTX, HEAD_DIM), F32), sin], axis=0)
    return cos, sin


def _q_pad_lanes(a):
    parts = []
    zero = jnp.zeros(a.shape[:-1] + (HEAD_DIM,), a.dtype)
    for h in range(N_HEADS):
        ah = a[..., h * HEAD_DIM:(h + 1) * HEAD_DIM]
        parts += [ah, zero] if h // 2 == 0 else [zero, ah]
    return jnp.concatenate(parts, axis=-1)


def _seg_matrix(width):
    idx = np.arange(width) // HEAD_DIM
    return jnp.asarray((idx[:, None] == idx[None, :]).astype(np.float32) / HEAD_DIM, dtype=BF16)


def _qkprep_call(p3, q_norm, k_norm):
    bsz = p3.shape[0]
    cos, sin = _rope_tables()
    cq = _q_pad_lanes(jnp.tile(cos, (1, N_HEADS)))
    sq = _q_pad_lanes(jnp.tile(sin, (1, N_HEADS)))
    ck = jnp.tile(cos, (1, KV_HEADS))
    sk = jnp.tile(sin, (1, KV_HEADS))
    wq = _q_pad_lanes(jnp.tile(q_norm, N_HEADS)[None, :])
    wk = jnp.tile(k_norm, KV_HEADS)[None, :]
    qw, kw = N_HEADS * 128, KV_HEADS * HEAD_DIM
    tile = lambda w, col: pl.BlockSpec((1, ROW_TILE, w), lambda j, b: (b, j, col))
    tab = lambda w: pl.BlockSpec((ROW_TILE, w), lambda j, b: (j, 0))
    full = lambda r, w: pl.BlockSpec((r, w), lambda j, b: (0, 0))
    return pl.pallas_call(
        _qkprep_kernel,
        grid=(N_TILES, bsz),
        in_specs=[tile(qw, COL_DQ // qw), tile(kw, COL_DK // kw), tile(kw, COL_DV // kw),
                  tab(qw), tab(qw), tab(kw), tab(kw),
                  full(1, qw), full(1, kw), full(qw, qw), full(kw, kw)],
        out_specs=[tile(qw, 0), tile(kw, 0), tile(KV_HEADS * kw, 0)],
        out_shape=[jax.ShapeDtypeStruct((bsz, SEQ_ALL, qw), BF16),
                   jax.ShapeDtypeStruct((bsz, SEQ_ALL, kw), BF16),
                   jax.ShapeDtypeStruct((bsz, SEQ_ALL, KV_HEADS * kw), BF16)],
        compiler_params=_cparams(("parallel", "parallel")),
        name="gqa_qk_prep",
    )(p3, p3, p3, cq, sq, ck, sk, wq, wk, _seg_matrix(qw), _seg_matrix(kw))


GQA_KCHUNK = 2048


def _gqa_attend(q2, k_ref, v_ref, g, chunks):
    m = acc = None
    for (k0, kn) in chunks:
        kc = k_ref[0, k0:k0 + kn, :]
        vc = v_ref[0, k0:k0 + kn, g * 128:(g + 1) * 128]
        s = _dot_nt(q2, kc)
        mc = jnp.max(s, axis=-1, keepdims=True)
        if m is None:
            m = mc
            acc = _dot(jnp.exp2(s - m).astype(BF16), vc)
        else:
            mn = jnp.maximum(m, mc)
            acc = jnp.exp2(m - mn) * acc + _dot(jnp.exp2(s - mn).astype(BF16), vc)
            m = mn
    sum_lane = (1 - g) * HEAD_DIM
    return acc / acc[:, sum_lane:sum_lane + 1]


def _gqa_kernel(q_ref, k_ref, v_ref, o_ref):
    lane = lax.broadcasted_iota(jnp.int32, (2 * ROW_TILE, 128), 1)
    ctx_chunks = [(0, N_CTX)]
    all_chunks = ctx_chunks + [(N_CTX + i * GQA_KCHUNK, GQA_KCHUNK) for i in range(N_LAT // GQA_KCHUNK)]

    def run(chunks):
        for g in range(KV_HEADS):
            q2 = jnp.concatenate([q_ref[0, :, (2 * g) * 128:(2 * g + 1) * 128],
                                  q_ref[0, :, (2 * g + 1) * 128:(2 * g + 2) * 128]], axis=0)
            o2 = _gqa_attend(q2, k_ref, v_ref, g, chunks)
            o2 = jnp.where(_div(lane, HEAD_DIM) == g, o2, 0.0).astype(BF16)
            o_ref[0, :, (2 * g) * 128:(2 * g + 1) * 128] = o2[:ROW_TILE]
            o_ref[0, :, (2 * g + 1) * 128:(2 * g + 2) * 128] = o2[ROW_TILE:]

    @pl.when(pl.program_id(1) == 0)
    def _():
        run(ctx_chunks)

    @pl.when(pl.program_id(1) > 0)
    def _():
        run(all_chunks)


def _gqa_call(qz, kn, va):
    bsz = qz.shape[0]
    qw, kw = N_HEADS * 128, KV_HEADS * HEAD_DIM
    return pl.pallas_call(
        _gqa_kernel,
        grid=(bsz, N_TILES),
        in_specs=[
            pl.BlockSpec((1, ROW_TILE, qw), lambda b, j: (b, j, 0)),
            pl.BlockSpec((1, SEQ_ALL, kw), lambda b, j: (b, 0, 0)),
            pl.BlockSpec((1, SEQ_ALL, KV_HEADS * kw), lambda b, j: (b, 0, 0)),
        ],
        out_specs=pl.BlockSpec((1, ROW_TILE, qw), lambda b, j: (b, j, 0)),
        out_shape=jax.ShapeDtypeStruct((bsz, SEQ_ALL, qw), BF16),
        compiler_params=_cparams(("parallel", "parallel")),
        name="gqa_attention",
    )(qz, kn, va)


NA_ROWS = ROW_TILE // GRID_W
NA_KTILES = 3
NA_KW = NA_KTILES * ROW_TILE


def _na_bias_tables(rpb):
    rows = N_LAT // GRID_W
    cq = np.arange(GRID_W)
    kc = np.arange(GRID_W)
    start_c = np.clip(cq - NA_WIN_C // 2, 0, GRID_W - NA_WIN_C)
    ok_c = (kc[None, :] >= start_c[:, None]) & (kc[None, :] < start_c[:, None] + NA_WIN_C)
    dcol = np.clip(kc[None, :] - cq[:, None] + NA_WIN_C - 1, 0, 2 * NA_WIN_C - 2)
    pick_c = (dcol[:, :, None] == np.arange(2 * NA_WIN_C - 1)).astype(np.float32) * LOG2E
    pick_r, ok_r = [], []
    for jb in (0, 1, LAT_TILES - 1):
        base = min(max(jb - 1, 0), LAT_TILES - NA_KTILES)
        r = jb * NA_ROWS + np.arange(NA_ROWS)
        kr = base * NA_ROWS + np.arange(NA_KTILES * NA_ROWS)
        start_r = np.clip(r - NA_WIN_R // 2, 0, rows - NA_WIN_R)
        ok_r.append((kr[None, :] >= start_r[:, None]) & (kr[None, :] < start_r[:, None] + NA_WIN_R))
        drow = np.clip(kr[None, :] - r[:, None] + NA_WIN_R - 1, 0, 2 * NA_WIN_R - 2)
        pick_r.append((drow[:, :, None] == np.arange(2 * NA_WIN_R - 1)).astype(np.float32))
    pick_r, ok_r = np.stack(pick_r), np.stack(ok_r)
    rows_sel = jnp.einsum('vamr,hrd->vhamd', pick_r, rpb, precision=HIGHEST)
    bias = jnp.einsum('vhamd,ckd->vhacmk', rows_sel, pick_c, precision=HIGHEST)
    ok = ok_r[:, None, :, None, :, None] & ok_c[None, None, None, :, None, :]
    bias = bias + np.where(ok, 0.0, NEG_BIG).astype(np.float32)
    return bias.reshape(3, N_HEADS, ROW_TILE, NA_KW)


def _na_kernel(q_ref, k0_ref, k1_ref, k2_ref, kc_ref, v0_ref, v1_ref, v2_ref, vc_ref, bias_ref, o_ref):
    lane = lax.broadcasted_iota(jnp.int32, (ROW_TILE, BRANCH_W), 1)
    blk = _div(lane, HEAD_DIM)
    q = q_ref[0].astype(F32) * (HEAD_DIM ** -0.5 * LOG2E)

    def attend(key_refs, val_refs, bias_of):
        acc = jnp.zeros((ROW_TILE, BRANCH_W), F32)
        for h in range(N_HEADS):
            qz = jnp.where(blk == h, q, 0.0).astype(BF16)
            s = [_dot_nt(qz, kr[0]) for kr in key_refs]
            s = [si if bias_of(h, i) is None else si + bias_of(h, i) for i, si in enumerate(s)]
            m = functools.reduce(jnp.maximum, [jnp.max(si, axis=-1, keepdims=True) for si in s])
            p = [jnp.exp2(si - m) for si in s]
            l = functools.reduce(jnp.add, [jnp.sum(pi, axis=-1, keepdims=True) for pi in p])
            o = functools.reduce(jnp.add, [_dot(pi.astype(BF16), vr[0]) for pi, vr in zip(p, val_refs)])
            acc = jnp.where(blk == h, o / l, acc)
        o_ref[0] = acc.astype(BF16)

    @pl.when(pl.program_id(1) == 0)
    def _():
        attend([kc_ref], [vc_ref], lambda h, i: None)

    @pl.when(pl.program_id(1) > 0)
    def _():
        def bias_of(h, i):
            if i == NA_KTILES:
                return None
            return bias_ref[0, h, :, i * ROW_TILE:(i + 1) * ROW_TILE]
        attend([k0_ref, k1_ref, k2_ref, kc_ref], [v0_ref, v1_ref, v2_ref, vc_ref], bias_of)


def _na_call(p3, bias_tabs):
    bsz = p3.shape[0]
    cq = COL_CQ // BRANCH_W

    def key_tile(j, m):
        return 1 + jnp.clip(j - 2, 0, LAT_TILES - NA_KTILES) + m

    def spec_k(col, m):
        return pl.BlockSpec((1, ROW_TILE, BRANCH_W), lambda b, j: (b, key_tile(j, m), col))

    def spec_c(col):
        return pl.BlockSpec((1, ROW_TILE, BRANCH_W), lambda b, j: (b, 0, col))

    def variant(j):
        return jnp.where(j <= 1, 0, jnp.where(j == LAT_TILES, 2, 1))

    return pl.pallas_call(
        _na_kernel,
        grid=(bsz, N_TILES),
        in_specs=[pl.BlockSpec((1, ROW_TILE, BRANCH_W), lambda b, j: (b, j, cq)),
                  spec_k(cq + 1, 0), spec_k(cq + 1, 1), spec_k(cq + 1, 2), spec_c(cq + 1),
                  spec_k(cq + 2, 0), spec_k(cq + 2, 1), spec_k(cq + 2, 2), spec_c(cq + 2),
                  pl.BlockSpec((1, N_HEADS, ROW_TILE, NA_KW), lambda b, j: (variant(j), 0, 0, 0))],
        out_specs=pl.BlockSpec((1, ROW_TILE, BRANCH_W), lambda b, j: (b, j, 0)),
        out_shape=jax.ShapeDtypeStruct((bsz, SEQ_ALL, BRANCH_W), BF16),
        compiler_params=_cparams(("parallel", "parallel")),
        name="na_attention",
    )(p3, p3, p3, p3, p3, p3, p3, p3, p3, bias_tabs)


def _merge_kernel(hf_ref, hb_ref, ag_ref, sf_ref, sb_ref, xs_ref, z_ref, yc_ref, yd_ref,
                  g0_ref, g1_ref, g2_ref, g3_ref, x_ref, gate_ref,
                  hn_w, dsk, sn_w, seg_ref, wa, wb, wc, wd, wo, ln_g, ln_b, o_ref):
    oa = hf_ref[0] + hb_ref[0]
    ag = ag_ref[0].astype(F32)
    oa = oa * lax.rsqrt(_segment_mean(oa * oa, seg_ref[...]) + LN_EPS) * hn_w[...]
    br_a = oa * (ag * _sigmoid(ag))
    z = z_ref[0].astype(F32)
    yb = (sf_ref[0] + sb_ref[0] + xs_ref[0].astype(F32) * dsk[...]) * (z * _sigmoid(z))
    br_b = yb * lax.rsqrt(jnp.mean(yb * yb, axis=-1, keepdims=True) + LN_EPS) * sn_w[...]
    merged = (1.0 + jnp.tanh(g0_ref[0].astype(F32))) * _dot(br_a.astype(BF16), wa[...])
    merged += (1.0 + jnp.tanh(g1_ref[0].astype(F32))) * _dot(br_b.astype(BF16), wb[...])
    merged += (1.0 + jnp.tanh(g2_ref[0].astype(F32))) * _dot(yc_ref[0], wc[...])
    merged += (1.0 + jnp.tanh(g3_ref[0].astype(F32))) * _dot(yd_ref[0], wd[...])
    y = _dot(merged.astype(BF16), wo[...])
    o_ref[0] = _layer_norm(ALPHA * x_ref[0] + gate_ref[0] * y) * ln_g[...] + ln_b[...]


def _merge_call(layer, mods, xa, p3, hf, hb, sf, sb, xact, yc, yd, hn_w, dsk, sn_w, wa, wb, wc, wd, wo,
                ln_g, ln_b, with_ctx):
    bsz = xa.shape[0]
    t0 = 0 if with_ctx else 1
    tiles = N_TILES - t0
    t256 = lambda col: pl.BlockSpec((1, ROW_TILE, BRANCH_W), lambda b, j: (b, j + t0, col))
    t512 = pl.BlockSpec((1, ROW_TILE, 512), lambda b, j: (b, j + t0, 0))
    tgate = lambda n: pl.BlockSpec((1, ROW_TILE, D_MODEL), lambda b, j: (b, j + t0, COL_GATE // D_MODEL + n))
    tx = pl.BlockSpec((1, ROW_TILE, D_MODEL), lambda b, j: (b, j + t0, 0))
    gate = pl.BlockSpec((1, 1, D_MODEL), lambda b, j: _mod_index(layer, 2)(b, j + t0))
    full = lambda a: pl.BlockSpec(a.shape, lambda b, j: (0,) * a.ndim)
    seg = _seg_matrix(BRANCH_W)
    consts = [hn_w, dsk, sn_w, seg, wa, wb, wc, wd, wo, ln_g, ln_b]
    return pl.pallas_call(
        _merge_kernel,
        grid=(bsz, tiles),
        in_specs=[t256(0), t256(0), t256(COL_AQ // BRANCH_W + 4), t256(0), t256(0), t256(0),
                  t256(COL_Z // BRANCH_W), t256(0), t512,
                  tgate(0), tgate(1), tgate(2), tgate(3), tx, gate]
                 + [full(a) for a in consts],
        out_specs=pl.BlockSpec((1, ROW_TILE, D_MODEL), lambda b, j: (b, j, 0)),
        out_shape=jax.ShapeDtypeStruct((bsz, tiles * ROW_TILE, D_MODEL), F32),
        compiler_params=_cparams(("parallel", "parallel")),
        name="merge_postnorm",
    )(hf, hb, p3, sf, sb, xact, p3, yc, yd, p3, p3, p3, p3, xa, mods, *consts)


FFN_CHUNK = 256


FFN_SUB = 2


def _ffn_kernel(x_ref, *refs):
    mod_refs, (wu_ref, wd_ref, ln_g, ln_b, o_ref) = refs[:3 * FFN_SUB], refs[3 * FFN_SUB:]
    tm = FFN_SUB * ROW_TILE
    x = x_ref[...]
    ln = _layer_norm(x)
    h = jnp.concatenate(
        [ln[u * ROW_TILE:(u + 1) * ROW_TILE] * (1.0 + mod_refs[3 * u + 1][0]) + mod_refs[3 * u][0]
         for u in range(FFN_SUB)], axis=0).astype(BF16)
    acc = jnp.zeros((tm, D_MODEL), F32)
    for c in range(FFN_HIDDEN // FFN_CHUNK):
        c0 = c * FFN_CHUNK
        g = _dot(h, wu_ref[:, c0:c0 + FFN_CHUNK])
        u = _dot(h, wu_ref[:, FFN_HIDDEN + c0:FFN_HIDDEN + c0 + FFN_CHUNK])
        a = (g * _sigmoid(g) * u).astype(BF16)
        acc = acc + _dot(a, wd_ref[c0:c0 + FFN_CHUNK, :])
    for u in range(FFN_SUB):
        rows = slice(u * ROW_TILE, (u + 1) * ROW_TILE)
        y = ALPHA * x[rows] + mod_refs[3 * u + 2][0] * acc[rows]
        o_ref[rows, :] = _layer_norm(y) * ln_g[...] + ln_b[...]


def _ffn_call(layer, mods, x1_flat, w_up, w_down, ln_g, ln_b, with_ctx):
    m = x1_flat.shape[0]
    tm = FFN_SUB * ROW_TILE
    seq_tiles, first = (N_TILES, 0) if with_ctx else (LAT_TILES, 1)
    mod_specs = [_flat_mod_spec(layer, 3 + which, FFN_SUB, u, seq_tiles, first)
                 for u in range(FFN_SUB) for which in range(3)]
    tx = pl.BlockSpec((tm, D_MODEL), lambda i: (i, 0))
    resident = lambda a: pl.BlockSpec(a.shape, lambda i: (0,) * a.ndim, pipeline_mode=pl.Buffered(1))
    full = lambda a: pl.BlockSpec(a.shape, lambda i: (0,) * a.ndim)
    return pl.pallas_call(
        _ffn_kernel,
        grid=(m // tm,),
        in_specs=[tx] + mod_specs + [resident(w_up), resident(w_down), full(ln_g), full(ln_b)],
        out_specs=tx,
        out_shape=jax.ShapeDtypeStruct(x1_flat.shape, F32),
        compiler_params=_cparams(("parallel",)),
        name="ffn_postnorm",
    )(x1_flat, *([mods] * (3 * FFN_SUB)), w_up, w_down, ln_g, ln_b)


def _pack_w_in(w):
    n_dt = 2 * N_HEADS
    o_dt = 5 * BRANCH_W + BRANCH_W + SSD_CONV_CH
    o_c = o_dt + n_dt
    o_dq = o_c + 3 * BRANCH_W
    o_dk = o_dq + BRANCH_W
    o_gate = o_dk + 2 * KV_HEADS * HEAD_DIM
    zeros = jnp.zeros((w.shape[0], COL_GATE - COL_DT - n_dt), w.dtype)
    cols = [w[:, :o_dt], w[:, o_c:o_dq], _q_pad_lanes(w[:, o_dq:o_dk]), w[:, o_dk:o_gate],
            w[:, o_dt:o_c], zeros, 0.5 * w[:, o_gate:]]
    packed = jnp.concatenate(cols, axis=1)
    assert packed.shape[1] == P_COLS
    return packed.astype(BF16)


def _pad_rows_d(wd):
    return _q_pad_lanes(wd.T).T


def kernel(x, c, ctx, c_ctx, ada_w, ada_b, w_in, hgrn_lb, hgrn_norm, ssd_conv_w, ssd_conv_b, ssd_dt_bias,
           ssd_a_log, ssd_d, ssd_norm, na_rpb, q_norm, k_norm, w_branch, w_out, ln1_g, ln1_b,
           ffn_w_up, ffn_w_down, ln2_g, ln2_b):
    bsz = x.shape[0]
    c8 = jnp.concatenate([c, c_ctx[None, :], jnp.zeros((8 - bsz - 1, D_MODEL), F32)], axis=0)
    mods = _ada_call(c8, ada_w, ada_b).reshape(DEPTH * 8 * 6, 1, D_MODEL)
    xa = jnp.concatenate([ctx, x], axis=1)
    row = lambda v: v.reshape(1, -1)
    for l in range(DEPTH):
        last = l == DEPTH - 1
        p = _inproj_call(xa.reshape(bsz * SEQ_ALL, D_MODEL), mods, _pack_w_in(w_in[l]), l)
        p3 = p.reshape(bsz, SEQ_ALL, P_COLS)
        hf, hb = _hgrn_call(p3, hgrn_lb[0], hgrn_lb[1], l)
        xact = _conv_call(p3, ssd_conv_w[l], ssd_conv_b[l])
        dt_rows = jnp.swapaxes(p3[:, :, COL_DT:COL_DT + 8], 1, 2).astype(F32)
        sf, sb = _ssd_call(xact, p3, dt_rows, ssd_dt_bias[l], ssd_a_log[l])
        yc = _na_call(p3, _na_bias_tables(na_rpb[l]))
        qz, kn, va = _qkprep_call(p3, q_norm[l], k_norm[l])
        yd = _gqa_call(qz, kn, va)
        wb = w_branch[l].astype(BF16)
        x1 = _merge_call(l, mods, xa, p3, hf, hb, sf, sb, xact, yc, yd,
                         row(hgrn_norm[l]), row(jnp.repeat(ssd_d[l], HEAD_DIM)), row(ssd_norm[l]),
                         wb[0], wb[1], wb[2], _pad_rows_d(wb[3]), (0.5 * w_out[l]).astype(BF16),
                         row(ln1_g[l]), row(ln1_b[l]), with_ctx=not last)
        xa = _ffn_call(l, mods, x1.reshape(-1, D_MODEL), ffn_w_up[l].astype(BF16), ffn_w_down[l].astype(BF16),
                       row(ln2_g[l]), row(ln2_b[l]), with_ctx=not last).reshape(bsz, -1, D_MODEL)
    return xa
```

```python
import functools
import math

import jax
import jax.numpy as jnp
import numpy as np
from jax import lax
from jax.experimental import pallas as pl
from jax.experimental.pallas import tpu as pltpu

F32 = jnp.float32
BF16 = jnp.bfloat16
HIGHEST = lax.Precision.HIGHEST

D_MODEL = 1024
DEPTH = 2
GRID_W = 64
N_CTX = 256
N_LAT = 4096
SEQ_ALL = N_CTX + N_LAT
HEAD_DIM = 64
BRANCH_W = 256
N_HEADS = 4
SSD_STATE = 128
SSD_GROUPS = 2
SSD_CONV_K = 5
SSD_CONV_CH = BRANCH_W + 2 * SSD_GROUPS * SSD_STATE
KV_HEADS = 2
NA_WIN_R = 8
NA_WIN_C = 16
ROPE_BASE = 10000.0
LN_EPS = 1e-6
FFN_HIDDEN = 2816
ALPHA = (2.0 * DEPTH) ** 0.25

ROW_TILE = 256
N_TILES = SEQ_ALL // ROW_TILE
LAT_TILES = N_LAT // ROW_TILE
HGRN_CHUNK = 64
HGRN_SUB = 16
SSD_CHUNK = 128
NEG_BIG = -1e30
VMEM_LIMIT = 48 * 1024 * 1024

P_COLS = 8192
COL_AQ = 0
COL_Z = 1280
COL_XBC = 1536
COL_CQ = 2304
COL_DQ = 3072
COL_DK = 3584
COL_DV = 3712
COL_DT = 3840
COL_GATE = 4096


def _cparams(sem):
    return pltpu.CompilerParams(dimension_semantics=sem, vmem_limit_bytes=VMEM_LIMIT)


def _sigmoid(x):
    return 0.5 * jnp.tanh(0.5 * x) + 0.5


def _div(x, n):
    return lax.shift_right_logical(x, int(math.log2(n)))


def _mod(x, n):
    return x & (n - 1)


def _softplus(x):
    return jnp.maximum(x, 0.0) + jnp.log(1.0 + jnp.exp(-jnp.abs(x)))


def _layer_norm(x):
    mu = jnp.mean(x, axis=-1, keepdims=True)
    xc = x - mu
    var = jnp.mean(xc * xc, axis=-1, keepdims=True)
    return xc * lax.rsqrt(var + LN_EPS)


def _dot(a, b):
    return jnp.dot(a, b, preferred_element_type=F32)


def _dot_nt(a, b):
    return lax.dot_general(a, b, (((1,), (1,)), ((), ())), preferred_element_type=F32)


def _dot_hi(a, b):
    return jnp.dot(a, b, precision=HIGHEST, preferred_element_type=F32)


def _segment_mean(x, seg):
    hi = x.astype(BF16)
    lo = (x - hi.astype(F32)).astype(BF16)
    return _dot(hi, seg) + _dot(lo, seg)


def _ada_kernel(c_ref, w_ref, b_ref, o_ref):
    cv = c_ref[...]
    a = cv * _sigmoid(cv)
    o_ref[0] = _dot_hi(a, w_ref[0]) + b_ref[0]


def _ada_call(c8, ada_w, ada_b):
    n_l = ada_w.shape[0]
    return pl.pallas_call(
        _ada_kernel,
        grid=(n_l, 6),
        in_specs=[
            pl.BlockSpec((8, D_MODEL), lambda l, j: (0, 0)),
            pl.BlockSpec((1, D_MODEL, D_MODEL), lambda l, j: (l, 0, j)),
            pl.BlockSpec((1, 1, D_MODEL), lambda l, j: (l, 0, j)),
        ],
        out_specs=pl.BlockSpec((1, 8, D_MODEL), lambda l, j: (l, 0, j)),
        out_shape=jax.ShapeDtypeStruct((n_l, 8, 6 * D_MODEL), F32),
        compiler_params=_cparams(("arbitrary", "arbitrary")),
        name="ada_mod",
    )(c8, ada_w, ada_b.reshape(n_l, 1, 6 * D_MODEL))


def _mod_index(layer, which):
    def index(b, j):
        return (layer * 48 + jnp.where(j == 0, 4, b) * 6 + which, 0, 0)
    return index


def _flat_mod_spec(layer, which, sub_tiles, u, seq_tiles, first_tile):
    def index(i, *_):
        t = i * sub_tiles + u
        b = t // seq_tiles
        j = t % seq_tiles + first_tile
        return (layer * 48 + jnp.where(j == 0, 4, b) * 6 + which, 0, 0)
    return pl.BlockSpec((1, 1, D_MODEL), index)


PROJ_SUB = 2
PROJ_TN = 1024


def _inproj_kernel(x_ref, *refs):
    mod_refs, (w_ref, o_ref) = refs[:2 * PROJ_SUB], refs[2 * PROJ_SUB:]
    parts = []
    for u in range(PROJ_SUB):
        rows = slice(u * ROW_TILE, (u + 1) * ROW_TILE)
        sh, sc = mod_refs[2 * u][0], mod_refs[2 * u + 1][0]
        parts.append((_layer_norm(x_ref[rows, :]) * (1.0 + sc) + sh).astype(BF16))
    h = jnp.concatenate(parts, axis=0)
    for n in range(P_COLS // PROJ_TN):
        cols = slice(n * PROJ_TN, (n + 1) * PROJ_TN)
        o_ref[:, cols] = _dot(h, w_ref[:, cols]).astype(BF16)


def _inproj_call(xa_flat, mods, wp, layer):
    m = xa_flat.shape[0]
    tm = PROJ_SUB * ROW_TILE
    mod_specs = []
    for u in range(PROJ_SUB):
        mod_specs += [_flat_mod_spec(layer, 0, PROJ_SUB, u, N_TILES, 0),
                      _flat_mod_spec(layer, 1, PROJ_SUB, u, N_TILES, 0)]
    return pl.pallas_call(
        _inproj_kernel,
        grid=(m // tm,),
        in_specs=[pl.BlockSpec((tm, D_MODEL), lambda i: (i, 0))] + mod_specs
                 + [pl.BlockSpec((D_MODEL, P_COLS), lambda i: (0, 0), pipeline_mode=pl.Buffered(1))],
        out_specs=pl.BlockSpec((tm, P_COLS), lambda i: (i, 0)),
        out_shape=jax.ShapeDtypeStruct((m, P_COLS), BF16),
        compiler_params=_cparams(("parallel",)),
        name="in_proj",
    )(xa_flat, *([mods] * (2 * PROJ_SUB)), wp)


CONV_PAD = 8


def _conv_kernel(x_ref, w_ref, b_ref, o_ref, xs_ref):
    zeros = jnp.zeros((CONV_PAD, 128), F32)
    xs_ref[0:CONV_PAD, :] = zeros
    xs_ref[CONV_PAD + SEQ_ALL:, :] = zeros
    xs_ref[CONV_PAD:CONV_PAD + SEQ_ALL, :] = x_ref[0].astype(F32)
    w = w_ref[...]
    row = lax.broadcasted_iota(jnp.int32, (ROW_TILE, 128), 0)
    half = SSD_CONV_K // 2
    for ci in range(N_TILES):
        t0 = ci * ROW_TILE
        acc = jnp.broadcast_to(b_ref[...], (ROW_TILE, 128))
        for k in range(SSD_CONV_K):
            j = k - half
            xk = xs_ref[CONV_PAD + t0 + j:CONV_PAD + t0 + j + ROW_TILE, :]
            if ci == 0 and j > 0:
                xk = jnp.where(row + j < ROW_TILE, xk, 0.0)
            if ci == 1 and j < 0:
                xk = jnp.where(row + j >= 0, xk, 0.0)
            acc = acc + xk * w[k:k + 1, :]
        o_ref[0, t0:t0 + ROW_TILE, :] = (acc * _sigmoid(acc)).astype(BF16)


def _conv_call(p3, conv_w, conv_b):
    bsz = p3.shape[0]
    n_ct = SSD_CONV_CH // 128
    return pl.pallas_call(
        _conv_kernel,
        grid=(bsz, n_ct),
        in_specs=[
            pl.BlockSpec((1, SEQ_ALL, 128), lambda b, c: (b, 0, COL_XBC // 128 + c)),
            pl.BlockSpec((SSD_CONV_K, 128), lambda b, c: (0, c)),
            pl.BlockSpec((1, 128), lambda b, c: (0, c)),
        ],
        out_specs=pl.BlockSpec((1, SEQ_ALL, 128), lambda b, c: (b, 0, c)),
        out_shape=jax.ShapeDtypeStruct((bsz, SEQ_ALL, SSD_CONV_CH), BF16),
        scratch_shapes=[pltpu.VMEM((SEQ_ALL + 2 * CONV_PAD, 128), F32)],
        compiler_params=_cparams(("parallel", "parallel")),
        name="ssd_conv",
    )(p3, conv_w, conv_b.reshape(1, SSD_CONV_CH))


def _fwd_tile(j):
    return j


def _bwd_tile(j):
    return jnp.where(j == 0, 0, N_TILES - j)


def _tri(n, upper):
    r = lax.broadcasted_iota(jnp.int32, (n, n), 0)
    c = lax.broadcasted_iota(jnp.int32, (n, n), 1)
    return (c >= r) if upper else (c <= r)


def _ssd_dir(rev, xa_ref, dtc_ref, dtr_ref, bias_r, alog_r, bias_c, alog_c, s_ref, o_ref):
    d = 1 if rev else 0
    dt_c = _softplus(dtc_ref[0].astype(F32) + bias_r)
    g_c = dt_c * (-jnp.exp(alog_r) * LOG2E)
    dt_r = _softplus(dtr_ref[0] + bias_c)
    g_r = dt_r * (-jnp.exp(alog_c) * LOG2E)
    n = SSD_CHUNK
    keep = _tri(n, rev)
    lane = lax.broadcasted_iota(jnp.int32, (n, 128), 1)
    first = lane < HEAD_DIM
    n_chunks = ROW_TILE // n
    order = range(n_chunks - 1, -1, -1) if rev else range(n_chunks)
    cum_c_all = _cumsum_rows(g_c, rev, n)
    cum_r_all = [_cumsum_cols(g_r[:, ci * n:(ci + 1) * n], rev) for ci in range(n_chunks)]
    for ci in order:
        c0 = ci * n
        cum_c = cum_c_all[c0:c0 + n, :]
        cum_r = cum_r_all[ci]
        end_row = 0 if rev else n - 1
        for g in range(SSD_GROUPS):
            bm = xa_ref[0, c0:c0 + n, BRANCH_W + g * SSD_STATE:BRANCH_W + (g + 1) * SSD_STATE]
            cm = xa_ref[0, c0:c0 + n, BRANCH_W + (SSD_GROUPS + g) * SSD_STATE:
                        BRANCH_W + (SSD_GROUPS + g + 1) * SSD_STATE]
            x2 = xa_ref[0, c0:c0 + n, g * 128:(g + 1) * 128].astype(F32)
            gmat = _dot_nt(cm, bm)
            col0 = 4 * d + 2 * g
            bc0 = cum_c[:, col0:col0 + 1]
            bc1 = cum_c[:, col0 + 1:col0 + 2]
            br0 = cum_r[col0:col0 + 1, :]
            br1 = cum_r[col0 + 1:col0 + 2, :]
            p0 = jnp.exp2(jnp.where(keep, bc0 - br0, NEG_BIG)) * gmat
            p1 = jnp.exp2(jnp.where(keep, bc1 - br1, NEG_BIG)) * gmat
            dt2 = jnp.where(first, dt_c[c0:c0 + n, col0:col0 + 1], dt_c[c0:c0 + n, col0 + 1:col0 + 2])
            v2 = x2 * dt2
            v2b = v2.astype(BF16)
            pv = jnp.where(first, _dot(p0.astype(BF16), v2b), _dot(p1.astype(BF16), v2b))
            bc2 = jnp.where(first, bc0, bc1)
            s_old = s_ref[g]
            qs = _dot(cm, s_old.astype(BF16))
            o_ref[0, c0:c0 + n, g * 128:(g + 1) * 128] = pv + jnp.exp2(bc2) * qs
            e0 = cum_c[end_row:end_row + 1, col0:col0 + 1]
            e1 = cum_c[end_row:end_row + 1, col0 + 1:col0 + 2]
            end2 = jnp.where(first[0:1, :], e0, e1)
            wv = (v2 * jnp.exp2(end2 - bc2)).astype(BF16)
            upd = lax.dot_general(bm, wv, (((0,), (0,)), ((), ())), preferred_element_type=F32)
            s_ref[g] = s_old * jnp.exp2(end2) + upd


def _ssd_kernel(xa_f, xa_b, dtc_f, dtc_b, dtr_f, dtr_b, bias_r, alog_r, bias_c, alog_c,
                o_f, o_b, s_ref):
    @pl.when(pl.program_id(1) == 0)
    def _():
        s_ref[...] = jnp.zeros(s_ref.shape, F32)

    br, ar, bc, ac = bias_r[...], alog_r[...], bias_c[...], alog_c[...]
    _ssd_dir(False, xa_f, dtc_f, dtr_f, br, ar, bc, ac, s_ref.at[0], o_f)
    _ssd_dir(True, xa_b, dtc_b, dtr_b, br, ar, bc, ac, s_ref.at[1], o_b)


def _ssd_call(xact, p3, dt_rows, dt_bias, a_log):
    bsz = xact.shape[0]
    bias8 = dt_bias.reshape(8).astype(F32)
    alog8 = a_log.reshape(8).astype(F32)
    pad_row = lambda v: jnp.zeros((1, 128), F32).at[0, :8].set(v)
    col = lambda v: jnp.broadcast_to(v[:, None], (8, ROW_TILE))

    def spec_x(tile):
        return pl.BlockSpec((1, ROW_TILE, SSD_CONV_CH), lambda b, j: (b, tile(j), 0))

    def spec_dtc(tile):
        return pl.BlockSpec((1, ROW_TILE, 128), lambda b, j: (b, tile(j), COL_DT // 128))

    def spec_dtr(tile):
        return pl.BlockSpec((1, 8, ROW_TILE), lambda b, j: (b, 0, tile(j)))

    def spec_o(tile):
        return pl.BlockSpec((1, ROW_TILE, BRANCH_W), lambda b, j: (b, tile(j), 0))

    small_r = pl.BlockSpec((1, 128), lambda b, j: (0, 0))
    small_c = pl.BlockSpec((8, ROW_TILE), lambda b, j: (0, 0))
    out = jax.ShapeDtypeStruct((bsz, SEQ_ALL, BRANCH_W), F32)
    return pl.pallas_call(
        _ssd_kernel,
        grid=(bsz, N_TILES),
        in_specs=[spec_x(_fwd_tile), spec_x(_bwd_tile), spec_dtc(_fwd_tile), spec_dtc(_bwd_tile),
                  spec_dtr(_fwd_tile), spec_dtr(_bwd_tile), small_r, small_r, small_c, small_c],
        out_specs=[spec_o(_fwd_tile), spec_o(_bwd_tile)],
        out_shape=[out, out],
        scratch_shapes=[pltpu.VMEM((2, SSD_GROUPS, SSD_STATE, 128), F32)],
        compiler_params=_cparams(("parallel", "arbitrary")),
        name="ssd_scan",
    )(xact, xact, p3, p3, dt_rows, dt_rows, pad_row(bias8), pad_row(alog8), col(bias8), col(alog8))


def _hgrn_lower_bound(lb_ref, layer):
    p = lb_ref[...]
    e = jnp.exp(p - jnp.max(p, axis=0, keepdims=True))
    sm = e / jnp.sum(e, axis=0, keepdims=True)
    acc = sm[0:1, :]
    for i in range(1, layer + 1):
        acc = acc + sm[i:i + 1, :]
    return acc - sm[0:1, :]


LOG2E = 1.4426950408889634


def _cumsum_rows(x, rev, block):
    n = x.shape[0]
    r_i = lax.broadcasted_iota(jnp.int32, (n, n), 0)
    c_i = lax.broadcasted_iota(jnp.int32, (n, n), 1)
    tri = ((c_i >= r_i) if rev else (c_i <= r_i)) & (_div(r_i, block) == _div(c_i, block))
    tri = tri.astype(BF16)
    hi = x.astype(BF16)
    rest = x - hi.astype(F32)
    mid = rest.astype(BF16)
    lo = (rest - mid.astype(F32)).astype(BF16)
    return _dot(tri, hi) + _dot(tri, mid) + _dot(tri, lo)


def _cumsum_cols(x, rev):
    n = x.shape[1]
    r_i = lax.broadcasted_iota(jnp.int32, (n, n), 0)
    c_i = lax.broadcasted_iota(jnp.int32, (n, n), 1)
    tri = ((r_i >= c_i) if rev else (r_i <= c_i)).astype(BF16)
    hi = x.astype(BF16)
    rest = x - hi.astype(F32)
    mid = rest.astype(BF16)
    lo = (rest - mid.astype(F32)).astype(BF16)
    return _dot(hi, tri) + _dot(mid, tri) + _dot(lo, tri)


def _hgrn_decay_logs(f, lb, rev):
    fg = lb + (1.0 - lb) * _sigmoid(f)
    cum2 = _cumsum_rows(jnp.log(fg) * LOG2E, rev, HGRN_CHUNK)
    return cum2, jnp.log(1.0 - fg) * LOG2E - cum2


def _hgrn_chunk(rev, q, v, cum2, lk2, st_ref, ones_seg, bd_mask):
    n, c = HGRN_CHUNK, HGRN_SUB
    row = lax.broadcasted_iota(jnp.int32, (n, BRANCH_W), 0)
    lane = lax.broadcasted_iota(jnp.int32, (n, BRANCH_W), 1)
    offset = (_mod(lane, HEAD_DIM) - row) if rev else (row - _mod(lane, HEAD_DIM))
    pos = _mod(row, c)

    es = []
    for dlt in range(c):
        if dlt == 0:
            arg = cum2 + lk2
        else:
            lks = pltpu.roll(lk2, (n - dlt) if rev else dlt, axis=0)
            ok = (pos + dlt < c) if rev else (pos >= dlt)
            arg = jnp.where(ok, cum2 + lks, NEG_BIG)
        es.append((q * jnp.exp2(arg)).astype(BF16))
    seg = _dot(jnp.concatenate(es, axis=0), ones_seg)
    sc = jnp.zeros((n, BRANCH_W), F32)
    for dlt in range(c):
        sc = jnp.where(offset == dlt, seg[dlt * n:(dlt + 1) * n, :], sc)

    cl_parts = []
    for i in range(n // c):
        blk_rows = cum2[i * c:(i + 1) * c, :]
        if rev:
            cl_parts.append(blk_rows - cum2[(i + 1) * c:(i + 1) * c + 1, :] if i < n // c - 1 else blk_rows)
        else:
            cl_parts.append(blk_rows - cum2[i * c - 1:i * c, :] if i > 0 else blk_rows)
    qe = (q * jnp.exp2(jnp.concatenate(cl_parts, axis=0))).astype(BF16)
    parts = []
    for i in range(n // c):
        if (rev and i == n // c - 1) or (not rev and i == 0):
            parts.append(jnp.zeros((c, BRANCH_W), F32))
            continue
        if rev:
            ref = cum2[(i + 1) * c:(i + 1) * c + 1, :]
            valid = row >= (i + 1) * c
        else:
            ref = cum2[i * c - 1:i * c, :]
            valid = row < i * c
        rhs = jnp.exp2(jnp.where(valid, lk2 + ref, NEG_BIG))
        rhs_bd = jnp.where(bd_mask, jnp.concatenate([rhs] * N_HEADS, axis=0), 0.0).astype(BF16)
        parts.append(_dot_nt(qe[i * c:(i + 1) * c, :], rhs_bd))
    sc = sc + jnp.concatenate(parts, axis=0)
    v_bd = jnp.where(bd_mask, jnp.concatenate([v] * N_HEADS, axis=0), 0.0).astype(BF16)
    o = _dot(sc.astype(BF16), v_bd)

    st = st_ref[...]
    o = o + _dot_nt((q * jnp.exp2(cum2)).astype(BF16), st.astype(BF16))
    end = cum2[0:1, :] if rev else cum2[n - 1:n, :]
    kp = jnp.exp2(lk2 + end)
    upd = _dot(v.T.astype(BF16), kp.astype(BF16))
    st_ref[...] = st * jnp.exp2(end) + jnp.where(bd_mask, upd, 0.0)
    return o


def _hgrn_kernel(layer, q_f, f_f, v_f, q_b, f_b, v_b, lbf_ref, lbb_ref, o_f, o_b,
                 st_f, st_b, cum_f, cum_b, lk_f, lk_b):
    @pl.when(pl.program_id(1) == 0)
    def _():
        st_f[...] = jnp.zeros(st_f.shape, F32)
        st_b[...] = jnp.zeros(st_b.shape, F32)

    for rev, f_ref, lb_ref, cum_ref, lk_ref in ((False, f_f, lbf_ref, cum_f, lk_f), (True, f_b, lbb_ref, cum_b, lk_b)):
        cum2, lk2 = _hgrn_decay_logs(f_ref[0].astype(F32), _hgrn_lower_bound(lb_ref, layer), rev)
        cum_ref[...] = cum2
        lk_ref[...] = lk2
    rb = _div(lax.broadcasted_iota(jnp.int32, (BRANCH_W, BRANCH_W), 0), HEAD_DIM)
    cb = _div(lax.broadcasted_iota(jnp.int32, (BRANCH_W, BRANCH_W), 1), HEAD_DIM)
    bd_mask = rb == cb
    ones_seg = bd_mask.astype(BF16)
    n_chunks = ROW_TILE // HGRN_CHUNK

    def body(i, carry):
        cf = pl.multiple_of(i * HGRN_CHUNK, HGRN_CHUNK)
        cbk = pl.multiple_of((n_chunks - 1 - i) * HGRN_CHUNK, HGRN_CHUNK)
        sl_f = pl.ds(cf, HGRN_CHUNK)
        sl_b = pl.ds(cbk, HGRN_CHUNK)
        o_f[0, sl_f, :] = _hgrn_chunk(False, q_f[0, sl_f, :].astype(F32), v_f[0, sl_f, :].astype(F32),
                                      cum_f[sl_f, :], lk_f[sl_f, :], st_f, ones_seg, bd_mask)
        o_b[0, sl_b, :] = _hgrn_chunk(True, q_b[0, sl_b, :].astype(F32), v_b[0, sl_b, :].astype(F32),
                                      cum_b[sl_b, :], lk_b[sl_b, :], st_b, ones_seg, bd_mask)
        return carry

    lax.fori_loop(0, n_chunks, body, 0, unroll=2)


def _hgrn_call(p3, lb_fwd, lb_bwd, layer):
    bsz = p3.shape[0]
    base = COL_AQ // BRANCH_W

    def spec(tile, col):
        return pl.BlockSpec((1, ROW_TILE, BRANCH_W), lambda b, j: (b, tile(j), col))

    lb_spec = pl.BlockSpec((DEPTH, BRANCH_W), lambda b, j: (0, 0))
    out = jax.ShapeDtypeStruct((bsz, SEQ_ALL, BRANCH_W), F32)
    return pl.pallas_call(
        functools.partial(_hgrn_kernel, layer),
        grid=(bsz, N_TILES),
        in_specs=[spec(_fwd_tile, base), spec(_fwd_tile, base + 1), spec(_fwd_tile, base + 3),
                  spec(_bwd_tile, base), spec(_bwd_tile, base + 2), spec(_bwd_tile, base + 3),
                  lb_spec, lb_spec],
        out_specs=[spec(_fwd_tile, 0), spec(_bwd_tile, 0)],
        out_shape=[out, out],
        scratch_shapes=[pltpu.VMEM((BRANCH_W, BRANCH_W), F32)] * 2 + [pltpu.VMEM((ROW_TILE, BRANCH_W), F32)] * 4,
        compiler_params=_cparams(("parallel", "arbitrary")),
        name="hgrn_scan",
    )(p3, p3, p3, p3, p3, p3, lb_fwd, lb_bwd)


def _rope(xn, cos, sin):
    w = xn.shape[1]
    lane = lax.broadcasted_iota(jnp.int32, xn.shape, 1)
    nxt = pltpu.roll(xn, w - HEAD_DIM // 4, axis=1)
    prv = pltpu.roll(xn, HEAD_DIM // 4, axis=1)
    partner = jnp.where(_mod(lane, HEAD_DIM // 2) < HEAD_DIM // 4, nxt, prv)
    return xn * cos + partner * sin


def _qkprep_kernel(q_ref, k_ref, v_ref, cq_ref, sq_ref, ck_ref, sk_ref, wq_ref, wk_ref, segq_ref, segk_ref,
                   qo_ref, ko_ref, vo_ref):
    q = q_ref[0].astype(F32)
    k = k_ref[0].astype(F32)
    qn = q * lax.rsqrt(_segment_mean(q * q, segq_ref[...]) + LN_EPS) * wq_ref[...]
    kn = k * lax.rsqrt(_segment_mean(k * k, segk_ref[...]) + LN_EPS) * wk_ref[...]
    qo_ref[0] = (_rope(qn, cq_ref[...], sq_ref[...]) * (HEAD_DIM ** -0.5 * LOG2E)).astype(BF16)
    ko_ref[0] = _rope(kn, ck_ref[...], sk_ref[...]).astype(BF16)
    v = v_ref[0].astype(F32)
    lane = lax.broadcasted_iota(jnp.int32, v.shape, 1)
    vo_ref[0] = jnp.concatenate([jnp.where(lane < HEAD_DIM, v, 1.0), jnp.where(lane < HEAD_DIM, 1.0, v)],
                                axis=1).astype(BF16)


def _rope_tables():
    t = np.arange(N_LAT)
    nf = HEAD_DIM // 4
    inv_freq = ROPE_BASE ** (-np.arange(nf, dtype=np.float32) / nf)
    ang_r = (t // GRID_W).astype(np.float32)[:, None] * inv_freq
    ang_c = (t % GRID_W).astype(np.float32)[:, None] * inv_freq
    ang = jnp.asarray(np.concatenate([ang_r, ang_r, ang_c, ang_c], axis=1))
    cos = jnp.cos(ang)
    sin = jnp.sin(ang)
    sign = np.tile(np.concatenate([-np.ones(nf, np.float32), np.ones(nf, np.float32)]), 2)
    sin = sin * sign
    cos = jnp.concatenate([jnp.ones((N_CTX, HEAD_DIM), F32), cos], axis=0)
    sin = jnp.concatenate([jnp.zeros((N_CTX, HEAD_DIM), F32), sin], axis=0)
    return cos, sin


def _q_pad_lanes(a):
    parts = []
    zero = jnp.zeros(a.shape[:-1] + (HEAD_DIM,), a.dtype)
    for h in range(N_HEADS):
        ah = a[..., h * HEAD_DIM:(h + 1) * HEAD_DIM]
        parts += [ah, zero] if h // 2 == 0 else [zero, ah]
    return jnp.concatenate(parts, axis=-1)


def _seg_matrix(width):
    idx = np.arange(width) // HEAD_DIM
    return jnp.asarray((idx[:, None] == idx[None, :]).astype(np.float32) / HEAD_DIM, dtype=BF16)


def _qkprep_call(p3, q_norm, k_norm):
    bsz = p3.shape[0]
    cos, sin = _rope_tables()
    cq = _q_pad_lanes(jnp.tile(cos, (1, N_HEADS)))
    sq = _q_pad_lanes(jnp.tile(sin, (1, N_HEADS)))
    ck = jnp.tile(cos, (1, KV_HEADS))
    sk = jnp.tile(sin, (1, KV_HEADS))
    wq = _q_pad_lanes(jnp.tile(q_norm, N_HEADS)[None, :])
    wk = jnp.tile(k_norm, KV_HEADS)[None, :]
    qw, kw = N_HEADS * 128, KV_HEADS * HEAD_DIM
    tile = lambda w, col: pl.BlockSpec((1, ROW_TILE, w), lambda j, b: (b, j, col))
    tab = lambda w: pl.BlockSpec((ROW_TILE, w), lambda j, b: (j, 0))
    full = lambda r, w: pl.BlockSpec((r, w), lambda j, b: (0, 0))
    return pl.pallas_call(
        _qkprep_kernel,
        grid=(N_TILES, bsz),
        in_specs=[tile(qw, COL_DQ // qw), tile(kw, COL_DK // kw), tile(kw, COL_DV // kw),
                  tab(qw), tab(qw), tab(kw), tab(kw),
                  full(1, qw), full(1, kw), full(qw, qw), full(kw, kw)],
        out_specs=[pl.BlockSpec((1, ROW_TILE, qw), lambda j, b: (b, jnp.where(j == 0, LAT_TILES, j - 1), 0)),
                   tile(kw, 0), tile(KV_HEADS * kw, 0)],
        out_shape=[jax.ShapeDtypeStruct((bsz, SEQ_ALL, qw), BF16),
                   jax.ShapeDtypeStruct((bsz, SEQ_ALL, kw), BF16),
                   jax.ShapeDtypeStruct((bsz, SEQ_ALL, KV_HEADS * kw), BF16)],
        compiler_params=_cparams(("parallel", "parallel")),
        name="gqa_qk_prep",
    )(p3, p3, p3, cq, sq, ck, sk, wq, wk, _seg_matrix(qw), _seg_matrix(kw))


GQA_KCHUNK = 2048


def _gqa_attend(q2, k_ref, v_ref, g, chunks):
    m = acc = None
    for (k0, kn) in chunks:
        kc = k_ref[0, k0:k0 + kn, :]
        vc = v_ref[0, k0:k0 + kn, g * 128:(g + 1) * 128]
        s = _dot_nt(q2, kc)
        mc = jnp.max(s, axis=-1, keepdims=True)
        if m is None:
            m = mc
            acc = _dot(jnp.exp2(s - m).astype(BF16), vc)
        else:
            mn = jnp.maximum(m, mc)
            acc = jnp.exp2(m - mn) * acc + _dot(jnp.exp2(s - mn).astype(BF16), vc)
            m = mn
    sum_lane = (1 - g) * HEAD_DIM
    return acc / acc[:, sum_lane:sum_lane + 1]


GQA_LAT_ROWS = 2 * ROW_TILE


def _gqa_kernel(rows, chunks, q_ref, k_ref, v_ref, o_ref):
    lane = lax.broadcasted_iota(jnp.int32, (2 * rows, 128), 1)
    for g in range(KV_HEADS):
        q2 = jnp.concatenate([q_ref[0, :, (2 * g) * 128:(2 * g + 1) * 128],
                              q_ref[0, :, (2 * g + 1) * 128:(2 * g + 2) * 128]], axis=0)
        o2 = _gqa_attend(q2, k_ref, v_ref, g, chunks)
        o2 = jnp.where(_div(lane, HEAD_DIM) == g, o2, 0.0).astype(BF16)
        o_ref[0, :, (2 * g) * 128:(2 * g + 1) * 128] = o2[:rows]
        o_ref[0, :, (2 * g + 1) * 128:(2 * g + 2) * 128] = o2[rows:]


def _gqa_call(qz, kn, va):
    bsz = qz.shape[0]
    qw, kw = N_HEADS * 128, KV_HEADS * HEAD_DIM
    ctx_chunks = [(0, N_CTX)]
    all_chunks = ctx_chunks + [(N_CTX + i * GQA_KCHUNK, GQA_KCHUNK) for i in range(N_LAT // GQA_KCHUNK)]
    lat = pl.pallas_call(
        functools.partial(_gqa_kernel, GQA_LAT_ROWS, all_chunks),
        grid=(bsz, N_LAT // GQA_LAT_ROWS),
        in_specs=[
            pl.BlockSpec((1, GQA_LAT_ROWS, qw), lambda b, j: (b, j, 0)),
            pl.BlockSpec((1, SEQ_ALL, kw), lambda b, j: (b, 0, 0)),
            pl.BlockSpec((1, SEQ_ALL, KV_HEADS * kw), lambda b, j: (b, 0, 0)),
        ],
        out_specs=pl.BlockSpec((1, GQA_LAT_ROWS, qw), lambda b, j: (b, j, 0)),
        out_shape=jax.ShapeDtypeStruct((bsz, N_LAT, qw), BF16),
        compiler_params=_cparams(("parallel", "parallel")),
        name="gqa_attention",
    )(qz, kn, va)
    ctx = pl.pallas_call(
        functools.partial(_gqa_kernel, N_CTX, ctx_chunks),
        grid=(bsz,),
        in_specs=[
            pl.BlockSpec((1, N_CTX, qw), lambda b: (b, N_LAT // N_CTX, 0)),
            pl.BlockSpec((1, N_CTX, kw), lambda b: (b, 0, 0)),
            pl.BlockSpec((1, N_CTX, KV_HEADS * kw), lambda b: (b, 0, 0)),
        ],
        out_specs=pl.BlockSpec((1, N_CTX, qw), lambda b: (b, 0, 0)),
        out_shape=jax.ShapeDtypeStruct((bsz, N_CTX, qw), BF16),
        compiler_params=_cparams(("parallel",)),
        name="gqa_attention_ctx",
    )(qz, kn, va)
    return lat, ctx


NA_ROWS = ROW_TILE // GRID_W
NA_KTILES = 3
NA_KW = NA_KTILES * ROW_TILE


def _na_bias_tables(rpb):
    rows = N_LAT // GRID_W
    cq = np.arange(GRID_W)
    kc = np.arange(GRID_W)
    start_c = np.clip(cq - NA_WIN_C // 2, 0, GRID_W - NA_WIN_C)
    ok_c = (kc[None, :] >= start_c[:, None]) & (kc[None, :] < start_c[:, None] + NA_WIN_C)
    dcol = np.clip(kc[None, :] - cq[:, None] + NA_WIN_C - 1, 0, 2 * NA_WIN_C - 2)
    pick_c = (dcol[:, :, None] == np.arange(2 * NA_WIN_C - 1)).astype(np.float32) * LOG2E
    pick_r, ok_r = [], []
    for jb in (0, 1, LAT_TILES - 1):
        base = min(max(jb - 1, 0), LAT_TILES - NA_KTILES)
        r = jb * NA_ROWS + np.arange(NA_ROWS)
        kr = base * NA_ROWS + np.arange(NA_KTILES * NA_ROWS)
        start_r = np.clip(r - NA_WIN_R // 2, 0, rows - NA_WIN_R)
        ok_r.append((kr[None, :] >= start_r[:, None]) & (kr[None, :] < start_r[:, None] + NA_WIN_R))
        drow = np.clip(kr[None, :] - r[:, None] + NA_WIN_R - 1, 0, 2 * NA_WIN_R - 2)
        pick_r.append((drow[:, :, None] == np.arange(2 * NA_WIN_R - 1)).astype(np.float32))
    pick_r, ok_r = np.stack(pick_r), np.stack(ok_r)
    rows_sel = jnp.einsum('vamr,hrd->vhamd', pick_r, rpb, precision=HIGHEST)
    bias = jnp.einsum('vhamd,ckd->vhacmk', rows_sel, pick_c, precision=HIGHEST)
    ok = ok_r[:, None, :, None, :, None] & ok_c[None, None, None, :, None, :]
    bias = bias + np.where(ok, 0.0, NEG_BIG).astype(np.float32)
    return bias.reshape(3, N_HEADS, ROW_TILE, NA_KW).astype(BF16)


def _na_kernel(q_ref, k0_ref, k1_ref, k2_ref, kc_ref, v0_ref, v1_ref, v2_ref, vc_ref, bias_ref, o_ref):
    lane = lax.broadcasted_iota(jnp.int32, (ROW_TILE, BRANCH_W), 1)
    blk = _div(lane, HEAD_DIM)
    q = q_ref[0].astype(F32) * (HEAD_DIM ** -0.5 * LOG2E)

    def attend(key_refs, val_refs, bias_of):
        acc = jnp.zeros((ROW_TILE, BRANCH_W), F32)
        for h in range(N_HEADS):
            qz = jnp.where(blk == h, q, 0.0).astype(BF16)
            s = [_dot_nt(qz, kr[0]) for kr in key_refs]
            s = [si if bias_of(h, i) is None else si + bias_of(h, i) for i, si in enumerate(s)]
            m = functools.reduce(jnp.maximum, [jnp.max(si, axis=-1, keepdims=True) for si in s])
            p = [jnp.exp2(si - m) for si in s]
            l = functools.reduce(jnp.add, [jnp.sum(pi, axis=-1, keepdims=True) for pi in p])
            o = functools.reduce(jnp.add, [_dot(pi.astype(BF16), vr[0]) for pi, vr in zip(p, val_refs)])
            acc = jnp.where(blk == h, o / l, acc)
        o_ref[0] = acc.astype(BF16)

    @pl.when(pl.program_id(1) == 0)
    def _():
        attend([kc_ref], [vc_ref], lambda h, i: None)

    @pl.when(pl.program_id(1) > 0)
    def _():
        def bias_of(h, i):
            if i == NA_KTILES:
                return None
            return bias_ref[0, h, :, i * ROW_TILE:(i + 1) * ROW_TILE].astype(F32)
        attend([k0_ref, k1_ref, k2_ref, kc_ref], [v0_ref, v1_ref, v2_ref, vc_ref], bias_of)


def _na_call(p3, bias_tabs):
    bsz = p3.shape[0]
    cq = COL_CQ // BRANCH_W

    def key_tile(j, m):
        return 1 + jnp.clip(j - 2, 0, LAT_TILES - NA_KTILES) + m

    def spec_k(col, m):
        return pl.BlockSpec((1, ROW_TILE, BRANCH_W), lambda b, j: (b, key_tile(j, m), col))

    def spec_c(col):
        return pl.BlockSpec((1, ROW_TILE, BRANCH_W), lambda b, j: (b, 0, col))

    def variant(j):
        return jnp.where(j <= 1, 0, jnp.where(j == LAT_TILES, 2, 1))

    return pl.pallas_call(
        _na_kernel,
        grid=(bsz, N_TILES),
        in_specs=[pl.BlockSpec((1, ROW_TILE, BRANCH_W), lambda b, j: (b, j, cq)),
                  spec_k(cq + 1, 0), spec_k(cq + 1, 1), spec_k(cq + 1, 2), spec_c(cq + 1),
                  spec_k(cq + 2, 0), spec_k(cq + 2, 1), spec_k(cq + 2, 2), spec_c(cq + 2),
                  pl.BlockSpec((1, N_HEADS, ROW_TILE, NA_KW), lambda b, j: (variant(j), 0, 0, 0))],
        out_specs=pl.BlockSpec((1, ROW_TILE, BRANCH_W), lambda b, j: (b, j, 0)),
        out_shape=jax.ShapeDtypeStruct((bsz, SEQ_ALL, BRANCH_W), BF16),
        compiler_params=_cparams(("parallel", "parallel")),
        name="na_attention",
    )(p3, p3, p3, p3, p3, p3, p3, p3, p3, bias_tabs)


def _merge_kernel(hf_ref, hb_ref, ag_ref, sf_ref, sb_ref, xs_ref, z_ref, yc_ref, yd_ref,
                  g0_ref, g1_ref, g2_ref, g3_ref, x_ref, gate_ref,
                  hn_w, dsk, sn_w, seg_ref, wa, wb, wc, wd, wo, ln_g, ln_b, o_ref):
    oa = hf_ref[0] + hb_ref[0]
    ag = ag_ref[0].astype(F32)
    oa = oa * lax.rsqrt(_segment_mean(oa * oa, seg_ref[...]) + LN_EPS) * hn_w[...]
    br_a = oa * (ag * _sigmoid(ag))
    z = z_ref[0].astype(F32)
    yb = (sf_ref[0] + sb_ref[0] + xs_ref[0].astype(F32) * dsk[...]) * (z * _sigmoid(z))
    br_b = yb * lax.rsqrt(jnp.mean(yb * yb, axis=-1, keepdims=True) + LN_EPS) * sn_w[...]
    merged = (1.0 + jnp.tanh(g0_ref[0].astype(F32))) * _dot(br_a.astype(BF16), wa[...])
    merged += (1.0 + jnp.tanh(g1_ref[0].astype(F32))) * _dot(br_b.astype(BF16), wb[...])
    merged += (1.0 + jnp.tanh(g2_ref[0].astype(F32))) * _dot(yc_ref[0], wc[...])
    merged += (1.0 + jnp.tanh(g3_ref[0].astype(F32))) * _dot(yd_ref[0], wd[...])
    y = _dot(merged.astype(BF16), wo[...])
    o_ref[0] = _layer_norm(ALPHA * x_ref[0] + gate_ref[0] * y) * ln_g[...] + ln_b[...]


def _merge_call(layer, mods, xa, p3, hf, hb, sf, sb, xact, yc, yd, hn_w, dsk, sn_w, wa, wb, wc, wd, wo,
                ln_g, ln_b, with_ctx):
    bsz = xa.shape[0]
    t0 = 0 if with_ctx else 1
    tiles = N_TILES - t0
    t256 = lambda col: pl.BlockSpec((1, ROW_TILE, BRANCH_W), lambda b, j: (b, j + t0, col))
    t512 = pl.BlockSpec((1, ROW_TILE, 512), lambda b, j: (b, j, 0))
    tgate = lambda n: pl.BlockSpec((1, ROW_TILE, D_MODEL), lambda b, j: (b, j + t0, COL_GATE // D_MODEL + n))
    tx = pl.BlockSpec((1, ROW_TILE, D_MODEL), lambda b, j: (b, j + t0, 0))
    gate = pl.BlockSpec((1, 1, D_MODEL), lambda b, j: _mod_index(layer, 2)(b, j + t0))
    full = lambda a: pl.BlockSpec(a.shape, lambda b, j: (0,) * a.ndim)
    seg = _seg_matrix(BRANCH_W)
    consts = [hn_w, dsk, sn_w, seg, wa, wb, wc, wd, wo, ln_g, ln_b]
    return pl.pallas_call(
        _merge_kernel,
        grid=(bsz, tiles),
        in_specs=[t256(0), t256(0), t256(COL_AQ // BRANCH_W + 4), t256(0), t256(0), t256(0),
                  t256(COL_Z // BRANCH_W), t256(0), t512,
                  tgate(0), tgate(1), tgate(2), tgate(3), tx, gate]
                 + [full(a) for a in consts],
        out_specs=pl.BlockSpec((1, ROW_TILE, D_MODEL), lambda b, j: (b, j, 0)),
        out_shape=jax.ShapeDtypeStruct((bsz, tiles * ROW_TILE, D_MODEL), F32),
        compiler_params=_cparams(("parallel", "parallel")),
        name="merge_postnorm",
    )(hf, hb, p3, sf, sb, xact, p3, yc, yd, p3, p3, p3, p3, xa, mods, *consts)


FFN_CHUNK = 256


FFN_SUB = 2


def _ffn_kernel(x_ref, *refs):
    mod_refs, (wu_ref, wd_ref, ln_g, ln_b, o_ref) = refs[:3 * FFN_SUB], refs[3 * FFN_SUB:]
    tm = FFN_SUB * ROW_TILE
    x = x_ref[...]
    ln = _layer_norm(x)
    h = jnp.concatenate(
        [ln[u * ROW_TILE:(u + 1) * ROW_TILE] * (1.0 + mod_refs[3 * u + 1][0]) + mod_refs[3 * u][0]
         for u in range(FFN_SUB)], axis=0).astype(BF16)
    acc = jnp.zeros((tm, D_MODEL), F32)
    for c in range(FFN_HIDDEN // FFN_CHUNK):
        c0 = c * FFN_CHUNK
        g = _dot(h, wu_ref[:, c0:c0 + FFN_CHUNK])
        u = _dot(h, wu_ref[:, FFN_HIDDEN + c0:FFN_HIDDEN + c0 + FFN_CHUNK])
        a = (g * _sigmoid(g) * u).astype(BF16)
        acc = acc + _dot(a, wd_ref[c0:c0 + FFN_CHUNK, :])
    for u in range(FFN_SUB):
        rows = slice(u * ROW_TILE, (u + 1) * ROW_TILE)
        y = ALPHA * x[rows] + mod_refs[3 * u + 2][0] * acc[rows]
        o_ref[rows, :] = _layer_norm(y) * ln_g[...] + ln_b[...]


def _ffn_call(layer, mods, x1_flat, w_up, w_down, ln_g, ln_b, with_ctx):
    m = x1_flat.shape[0]
    tm = FFN_SUB * ROW_TILE
    seq_tiles, first = (N_TILES, 0) if with_ctx else (LAT_TILES, 1)
    mod_specs = [_flat_mod_spec(layer, 3 + which, FFN_SUB, u, seq_tiles, first)
                 for u in range(FFN_SUB) for which in range(3)]
    tx = pl.BlockSpec((tm, D_MODEL), lambda i: (i, 0))
    resident = lambda a: pl.BlockSpec(a.shape, lambda i: (0,) * a.ndim, pipeline_mode=pl.Buffered(1))
    full = lambda a: pl.BlockSpec(a.shape, lambda i: (0,) * a.ndim)
    return pl.pallas_call(
        _ffn_kernel,
        grid=(m // tm,),
        in_specs=[tx] + mod_specs + [resident(w_up), resident(w_down), full(ln_g), full(ln_b)],
        out_specs=tx,
        out_shape=jax.ShapeDtypeStruct(x1_flat.shape, F32),
        compiler_params=_cparams(("parallel",)),
        name="ffn_postnorm",
    )(x1_flat, *([mods] * (3 * FFN_SUB)), w_up, w_down, ln_g, ln_b)


def _pack_w_in(w):
    n_dt = 2 * N_HEADS
    o_dt = 5 * BRANCH_W + BRANCH_W + SSD_CONV_CH
    o_c = o_dt + n_dt
    o_dq = o_c + 3 * BRANCH_W
    o_dk = o_dq + BRANCH_W
    o_gate = o_dk + 2 * KV_HEADS * HEAD_DIM
    zeros = jnp.zeros((w.shape[0], COL_GATE - COL_DT - n_dt), w.dtype)
    cols = [w[:, :o_dt], w[:, o_c:o_dq], _q_pad_lanes(w[:, o_dq:o_dk]), w[:, o_dk:o_gate],
            w[:, o_dt:o_c], zeros, 0.5 * w[:, o_gate:]]
    packed = jnp.concatenate(cols, axis=1)
    assert packed.shape[1] == P_COLS
    return packed.astype(BF16)


def _pad_rows_d(wd):
    return _q_pad_lanes(wd.T).T


def kernel(x, c, ctx, c_ctx, ada_w, ada_b, w_in, hgrn_lb, hgrn_norm, ssd_conv_w, ssd_conv_b, ssd_dt_bias,
           ssd_a_log, ssd_d, ssd_norm, na_rpb, q_norm, k_norm, w_branch, w_out, ln1_g, ln1_b,
           ffn_w_up, ffn_w_down, ln2_g, ln2_b):
    bsz = x.shape[0]
    c8 = jnp.concatenate([c, c_ctx[None, :], jnp.zeros((8 - bsz - 1, D_MODEL), F32)], axis=0)
    mods = _ada_call(c8, ada_w, ada_b).reshape(DEPTH * 8 * 6, 1, D_MODEL)
    xa = jnp.concatenate([ctx, x], axis=1)
    row = lambda v: v.reshape(1, -1)
    for l in range(DEPTH):
        last = l == DEPTH - 1
        p = _inproj_call(xa.reshape(bsz * SEQ_ALL, D_MODEL), mods, _pack_w_in(w_in[l]), l)
        p3 = p.reshape(bsz, SEQ_ALL, P_COLS)
        hf, hb = _hgrn_call(p3, hgrn_lb[0], hgrn_lb[1], l)
        xact = _conv_call(p3, ssd_conv_w[l], ssd_conv_b[l])
        dt_rows = jnp.swapaxes(p3[:, :, COL_DT:COL_DT + 8], 1, 2).astype(F32)
        sf, sb = _ssd_call(xact, p3, dt_rows, ssd_dt_bias[l], ssd_a_log[l])
        yc = _na_call(p3, _na_bias_tables(na_rpb[l]))
        qz, kn, va = _qkprep_call(p3, q_norm[l], k_norm[l])
        yd, yd_ctx = _gqa_call(qz, kn, va)
        if not last:
            yd = jnp.concatenate([yd_ctx, yd], axis=1)
        wb = w_branch[l].astype(BF16)
        x1 = _merge_call(l, mods, xa, p3, hf, hb, sf, sb, xact, yc, yd,
                         row(hgrn_norm[l]), row(jnp.repeat(ssd_d[l], HEAD_DIM)), row(ssd_norm[l]),
                         wb[0], wb[1], wb[2], _pad_rows_d(wb[3]), (0.5 * w_out[l]).astype(BF16),
                         row(ln1_g[l]), row(ln1_b[l]), with_ctx=not last)
        xa = _ffn_call(l, mods, x1.reshape(-1, D_MODEL), ffn_w_up[l].astype(BF16), ffn_w_down[l].astype(BF16),
                       row(ln2_g[l]), row(ln2_b[l]), with_ctx=not last).reshape(bsz, -1, D_MODEL)
    return xa
```

```python
import functools
import math

import jax
import jax.numpy as jnp
import numpy as np
from jax import lax
from jax.experimental import pallas as pl
from jax.experimental.pallas import tpu as pltpu

F32 = jnp.float32
BF16 = jnp.bfloat16
HIGHEST = lax.Precision.HIGHEST

D_MODEL = 1024
DEPTH = 2
GRID_W = 64
N_CTX = 256
N_LAT = 4096
SEQ_ALL = N_CTX + N_LAT
HEAD_DIM = 64
BRANCH_W = 256
N_HEADS = 4
SSD_STATE = 128
SSD_GROUPS = 2
SSD_CONV_K = 5
SSD_CONV_CH = BRANCH_W + 2 * SSD_GROUPS * SSD_STATE
KV_HEADS = 2
NA_WIN_R = 8
NA_WIN_C = 16
ROPE_BASE = 10000.0
LN_EPS = 1e-6
FFN_HIDDEN = 2816
ALPHA = (2.0 * DEPTH) ** 0.25

ROW_TILE = 256
N_TILES = SEQ_ALL // ROW_TILE
LAT_TILES = N_LAT // ROW_TILE
HGRN_CHUNK = 64
HGRN_SUB = 16
SSD_CHUNK = 128
NEG_BIG = -1e30
VMEM_LIMIT = 48 * 1024 * 1024

P_COLS = 8192
COL_AQ = 0
COL_Z = 1280
COL_XBC = 1536
COL_CQ = 2304
COL_DQ = 3072
COL_DK = 3584
COL_DV = 3712
COL_DT = 3840
COL_GATE = 4096


def _cparams(sem):
    return pltpu.CompilerParams(dimension_semantics=sem, vmem_limit_bytes=VMEM_LIMIT)


def _sigmoid(x):
    return 0.5 * jnp.tanh(0.5 * x) + 0.5


def _div(x, n):
    return lax.shift_right_logical(x, int(math.log2(n)))


def _mod(x, n):
    return x & (n - 1)


def _softplus(x):
    return jnp.maximum(x, 0.0) + jnp.log(1.0 + jnp.exp(-jnp.abs(x)))


def _layer_norm(x):
    mu = jnp.mean(x, axis=-1, keepdims=True)
    xc = x - mu
    var = jnp.mean(xc * xc, axis=-1, keepdims=True)
    return xc * lax.rsqrt(var + LN_EPS)


def _dot(a, b):
    return jnp.dot(a, b, preferred_element_type=F32)


def _dot_nt(a, b):
    return lax.dot_general(a, b, (((1,), (1,)), ((), ())), preferred_element_type=F32)


def _dot_hi(a, b):
    return jnp.dot(a, b, precision=HIGHEST, preferred_element_type=F32)


def _segment_mean(x, seg):
    hi = x.astype(BF16)
    lo = (x - hi.astype(F32)).astype(BF16)
    return _dot(hi, seg) + _dot(lo, seg)


def _ada_kernel(c_ref, w_ref, b_ref, o_ref):
    cv = c_ref[...]
    a = cv * _sigmoid(cv)
    o_ref[0] = _dot_hi(a, w_ref[0]) + b_ref[0]


def _ada_call(c8, ada_w, ada_b):
    n_l = ada_w.shape[0]
    return pl.pallas_call(
        _ada_kernel,
        grid=(n_l, 6),
        in_specs=[
            pl.BlockSpec((8, D_MODEL), lambda l, j: (0, 0)),
            pl.BlockSpec((1, D_MODEL, D_MODEL), lambda l, j: (l, 0, j)),
            pl.BlockSpec((1, 1, D_MODEL), lambda l, j: (l, 0, j)),
        ],
        out_specs=pl.BlockSpec((1, 8, D_MODEL), lambda l, j: (l, 0, j)),
        out_shape=jax.ShapeDtypeStruct((n_l, 8, 6 * D_MODEL), F32),
        compiler_params=_cparams(("arbitrary", "arbitrary")),
        name="ada_mod",
    )(c8, ada_w, ada_b.reshape(n_l, 1, 6 * D_MODEL))


def _mod_index(layer, which):
    def index(b, j):
        return (layer * 48 + jnp.where(j == 0, 4, b) * 6 + which, 0, 0)
    return index


def _flat_mod_spec(layer, which, sub_tiles, u, seq_tiles, first_tile):
    def index(i, *_):
        t = i * sub_tiles + u
        b = t // seq_tiles
        j = t % seq_tiles + first_tile
        return (layer * 48 + jnp.where(j == 0, 4, b) * 6 + which, 0, 0)
    return pl.BlockSpec((1, 1, D_MODEL), index)


PROJ_SUB = 2
PROJ_TN = 1024


def _split_row_specs(sub_tiles, u):
    def tile_of(i):
        t = i * sub_tiles + u
        return t // N_TILES, t % N_TILES
    lat = pl.BlockSpec((1, ROW_TILE, D_MODEL), lambda i: (tile_of(i)[0], jnp.maximum(tile_of(i)[1] - 1, 0), 0))
    ctx = pl.BlockSpec((1, ROW_TILE, D_MODEL), lambda i: (tile_of(i)[0], 0, 0))
    return [lat, ctx]


def _inproj_kernel(split, *refs):
    n_x = 2 * PROJ_SUB if split else 1
    x_refs, mod_refs, (w_ref, o_ref) = refs[:n_x], refs[n_x:n_x + 2 * PROJ_SUB], refs[n_x + 2 * PROJ_SUB:]
    parts = []
    for u in range(PROJ_SUB):
        if split:
            j = (pl.program_id(0) * PROJ_SUB + u) % N_TILES
            xt = jnp.where(j == 0, x_refs[2 * u + 1][0], x_refs[2 * u][0])
        else:
            xt = x_refs[0][u * ROW_TILE:(u + 1) * ROW_TILE, :]
        sh, sc = mod_refs[2 * u][0], mod_refs[2 * u + 1][0]
        parts.append((_layer_norm(xt) * (1.0 + sc) + sh).astype(BF16))
    h = jnp.concatenate(parts, axis=0)
    for n in range(P_COLS // PROJ_TN):
        cols = slice(n * PROJ_TN, (n + 1) * PROJ_TN)
        acc = _dot(h, w_ref[:, cols])
        if n * PROJ_TN >= COL_GATE:
            acc = 0.5 * acc
        o_ref[:, cols] = acc.astype(BF16)


def _inproj_call(x_src, mods, wp, layer):
    split = isinstance(x_src, tuple)
    bsz = (x_src[0] if split else x_src).shape[0]
    m = bsz * SEQ_ALL
    tm = PROJ_SUB * ROW_TILE
    mod_specs = []
    for u in range(PROJ_SUB):
        mod_specs += [_flat_mod_spec(layer, 0, PROJ_SUB, u, N_TILES, 0),
                      _flat_mod_spec(layer, 1, PROJ_SUB, u, N_TILES, 0)]
    if split:
        x_specs = [s for u in range(PROJ_SUB) for s in _split_row_specs(PROJ_SUB, u)]
        x_args = list(x_src) * PROJ_SUB
    else:
        x_specs = [pl.BlockSpec((tm, D_MODEL), lambda i: (i, 0))]
        x_args = [x_src.reshape(m, D_MODEL)]
    return pl.pallas_call(
        functools.partial(_inproj_kernel, split),
        grid=(m // tm,),
        in_specs=x_specs + mod_specs
                 + [pl.BlockSpec((D_MODEL, P_COLS), lambda i: (0, 0), pipeline_mode=pl.Buffered(1))],
        out_specs=pl.BlockSpec((tm, P_COLS), lambda i: (i, 0)),
        out_shape=jax.ShapeDtypeStruct((m, P_COLS), BF16),
        compiler_params=_cparams(("parallel",)),
        name="in_proj",
    )(*x_args, *([mods] * (2 * PROJ_SUB)), wp)


CONV_PAD = 8


def _conv_kernel(x_ref, w_ref, b_ref, o_ref, xs_ref):
    zeros = jnp.zeros((CONV_PAD, 128), F32)
    xs_ref[0:CONV_PAD, :] = zeros
    xs_ref[CONV_PAD + SEQ_ALL:, :] = zeros
    xs_ref[CONV_PAD:CONV_PAD + SEQ_ALL, :] = x_ref[0].astype(F32)
    w = w_ref[...]
    row = lax.broadcasted_iota(jnp.int32, (ROW_TILE, 128), 0)
    half = SSD_CONV_K // 2
    for ci in range(N_TILES):
        t0 = ci * ROW_TILE
        acc = jnp.broadcast_to(b_ref[...], (ROW_TILE, 128))
        for k in range(SSD_CONV_K):
            j = k - half
            xk = xs_ref[CONV_PAD + t0 + j:CONV_PAD + t0 + j + ROW_TILE, :]
            if ci == 0 and j > 0:
                xk = jnp.where(row + j < ROW_TILE, xk, 0.0)
            if ci == 1 and j < 0:
                xk = jnp.where(row + j >= 0, xk, 0.0)
            acc = acc + xk * w[k:k + 1, :]
        o_ref[0, t0:t0 + ROW_TILE, :] = (acc * _sigmoid(acc)).astype(BF16)


def _conv_call(p3, conv_w, conv_b):
    bsz = p3.shape[0]
    n_ct = SSD_CONV_CH // 128
    return pl.pallas_call(
        _conv_kernel,
        grid=(bsz, n_ct),
        in_specs=[
            pl.BlockSpec((1, SEQ_ALL, 128), lambda b, c: (b, 0, COL_XBC // 128 + c)),
            pl.BlockSpec((SSD_CONV_K, 128), lambda b, c: (0, c)),
            pl.BlockSpec((1, 128), lambda b, c: (0, c)),
        ],
        out_specs=pl.BlockSpec((1, SEQ_ALL, 128), lambda b, c: (b, 0, c)),
        out_shape=jax.ShapeDtypeStruct((bsz, SEQ_ALL, SSD_CONV_CH), BF16),
        scratch_shapes=[pltpu.VMEM((SEQ_ALL + 2 * CONV_PAD, 128), F32)],
        compiler_params=_cparams(("parallel", "parallel")),
        name="ssd_conv",
    )(p3, conv_w, conv_b.reshape(1, SSD_CONV_CH))


def _fwd_tile(j):
    return j


def _bwd_tile(j):
    return jnp.where(j == 0, 0, N_TILES - j)


def _tri(n, upper):
    r = lax.broadcasted_iota(jnp.int32, (n, n), 0)
    c = lax.broadcasted_iota(jnp.int32, (n, n), 1)
    return (c >= r) if upper else (c <= r)


def _ssd_dir(rev, xa_ref, dtc_ref, dtr_ref, bias_r, alog_r, bias_c, alog_c, s_ref, o_ref):
    d = 1 if rev else 0
    dt_c = _softplus(dtc_ref[0].astype(F32) + bias_r)
    g_c = dt_c * (-jnp.exp(alog_r) * LOG2E)
    dt_r = _softplus(dtr_ref[0] + bias_c)
    g_r = dt_r * (-jnp.exp(alog_c) * LOG2E)
    n = SSD_CHUNK
    keep = _tri(n, rev)
    lane = lax.broadcasted_iota(jnp.int32, (n, 128), 1)
    first = lane < HEAD_DIM
    n_chunks = ROW_TILE // n
    order = range(n_chunks - 1, -1, -1) if rev else range(n_chunks)
    cum_c_all = _cumsum_rows(g_c, rev, n)
    cum_r_all = [_cumsum_cols(g_r[:, ci * n:(ci + 1) * n], rev) for ci in range(n_chunks)]
    for ci in order:
        c0 = ci * n
        cum_c = cum_c_all[c0:c0 + n, :]
        cum_r = cum_r_all[ci]
        end_row = 0 if rev else n - 1
        for g in range(SSD_GROUPS):
            bm = xa_ref[0, c0:c0 + n, BRANCH_W + g * SSD_STATE:BRANCH_W + (g + 1) * SSD_STATE]
            cm = xa_ref[0, c0:c0 + n, BRANCH_W + (SSD_GROUPS + g) * SSD_STATE:
                        BRANCH_W + (SSD_GROUPS + g + 1) * SSD_STATE]
            x2 = xa_ref[0, c0:c0 + n, g * 128:(g + 1) * 128].astype(F32)
            gmat = _dot_nt(cm, bm)
            col0 = 4 * d + 2 * g
            bc0 = cum_c[:, col0:col0 + 1]
            bc1 = cum_c[:, col0 + 1:col0 + 2]
            br0 = cum_r[col0:col0 + 1, :]
            br1 = cum_r[col0 + 1:col0 + 2, :]
            p0 = jnp.exp2(jnp.where(keep, bc0 - br0, NEG_BIG)) * gmat
            p1 = jnp.exp2(jnp.where(keep, bc1 - br1, NEG_BIG)) * gmat
            dt2 = jnp.where(first, dt_c[c0:c0 + n, col0:col0 + 1], dt_c[c0:c0 + n, col0 + 1:col0 + 2])
            v2 = x2 * dt2
            v2b = v2.astype(BF16)
            pv = jnp.where(first, _dot(p0.astype(BF16), v2b), _dot(p1.astype(BF16), v2b))
            bc2 = jnp.where(first, bc0, bc1)
            s_old = s_ref[g]
            qs = _dot(cm, s_old.astype(BF16))
            o_ref[0, c0:c0 + n, g * 128:(g + 1) * 128] = pv + jnp.exp2(bc2) * qs
            e0 = cum_c[end_row:end_row + 1, col0:col0 + 1]
            e1 = cum_c[end_row:end_row + 1, col0 + 1:col0 + 2]
            end2 = jnp.where(first[0:1, :], e0, e1)
            wv = (v2 * jnp.exp2(end2 - bc2)).astype(BF16)
            upd = lax.dot_general(bm, wv, (((0,), (0,)), ((), ())), preferred_element_type=F32)
            s_ref[g] = s_old * jnp.exp2(end2) + upd


def _ssd_kernel(xa_f, xa_b, dtc_f, dtc_b, dtr_f, dtr_b, bias_r, alog_r, bias_c, alog_c,
                o_f, o_b, s_ref):
    @pl.when(pl.program_id(1) == 0)
    def _():
        s_ref[...] = jnp.zeros(s_ref.shape, F32)

    br, ar, bc, ac = bias_r[...], alog_r[...], bias_c[...], alog_c[...]
    _ssd_dir(False, xa_f, dtc_f, dtr_f, br, ar, bc, ac, s_ref.at[0], o_f)
    _ssd_dir(True, xa_b, dtc_b, dtr_b, br, ar, bc, ac, s_ref.at[1], o_b)


def _ssd_call(xact, p3, dt_rows, dt_bias, a_log):
    bsz = xact.shape[0]
    bias8 = dt_bias.reshape(8).astype(F32)
    alog8 = a_log.reshape(8).astype(F32)
    pad_row = lambda v: jnp.zeros((1, 128), F32).at[0, :8].set(v)
    col = lambda v: jnp.broadcast_to(v[:, None], (8, ROW_TILE))

    def spec_x(tile):
        return pl.BlockSpec((1, ROW_TILE, SSD_CONV_CH), lambda b, j: (b, tile(j), 0))

    def spec_dtc(tile):
        return pl.BlockSpec((1, ROW_TILE, 128), lambda b, j: (b, tile(j), COL_DT // 128))

    def spec_dtr(tile):
        return pl.BlockSpec((1, 8, ROW_TILE), lambda b, j: (b, 0, tile(j)))

    def spec_o(tile):
        return pl.BlockSpec((1, ROW_TILE, BRANCH_W), lambda b, j: (b, tile(j), 0))

    small_r = pl.BlockSpec((1, 128), lambda b, j: (0, 0))
    small_c = pl.BlockSpec((8, ROW_TILE), lambda b, j: (0, 0))
    out = jax.ShapeDtypeStruct((bsz, SEQ_ALL, BRANCH_W), F32)
    return pl.pallas_call(
        _ssd_kernel,
        grid=(bsz, N_TILES),
        in_specs=[spec_x(_fwd_tile), spec_x(_bwd_tile), spec_dtc(_fwd_tile), spec_dtc(_bwd_tile),
                  spec_dtr(_fwd_tile), spec_dtr(_bwd_tile), small_r, small_r, small_c, small_c],
        out_specs=[spec_o(_fwd_tile), spec_o(_bwd_tile)],
        out_shape=[out, out],
        scratch_shapes=[pltpu.VMEM((2, SSD_GROUPS, SSD_STATE, 128), F32)],
        compiler_params=_cparams(("parallel", "arbitrary")),
        name="ssd_scan",
    )(xact, xact, p3, p3, dt_rows, dt_rows, pad_row(bias8), pad_row(alog8), col(bias8), col(alog8))


def _hgrn_lower_bound(lb_ref, layer):
    p = lb_ref[...]
    e = jnp.exp(p - jnp.max(p, axis=0, keepdims=True))
    sm = e / jnp.sum(e, axis=0, keepdims=True)
    acc = sm[0:1, :]
    for i in range(1, layer + 1):
        acc = acc + sm[i:i + 1, :]
    return acc - sm[0:1, :]


LOG2E = 1.4426950408889634


def _cumsum_rows(x, rev, block):
    n = x.shape[0]
    r_i = lax.broadcasted_iota(jnp.int32, (n, n), 0)
    c_i = lax.broadcasted_iota(jnp.int32, (n, n), 1)
    tri = ((c_i >= r_i) if rev else (c_i <= r_i)) & (_div(r_i, block) == _div(c_i, block))
    tri = tri.astype(BF16)
    hi = x.astype(BF16)
    rest = x - hi.astype(F32)
    mid = rest.astype(BF16)
    lo = (rest - mid.astype(F32)).astype(BF16)
    return _dot(tri, hi) + _dot(tri, mid) + _dot(tri, lo)


def _cumsum_cols(x, rev):
    n = x.shape[1]
    r_i = lax.broadcasted_iota(jnp.int32, (n, n), 0)
    c_i = lax.broadcasted_iota(jnp.int32, (n, n), 1)
    tri = ((r_i >= c_i) if rev else (r_i <= c_i)).astype(BF16)
    hi = x.astype(BF16)
    rest = x - hi.astype(F32)
    mid = rest.astype(BF16)
    lo = (rest - mid.astype(F32)).astype(BF16)
    return _dot(hi, tri) + _dot(mid, tri) + _dot(lo, tri)


def _hgrn_decay_logs(f, lb, rev):
    fg = lb + (1.0 - lb) * _sigmoid(f)
    cum2 = _cumsum_rows(jnp.log(fg) * LOG2E, rev, HGRN_CHUNK)
    return cum2, jnp.log(1.0 - fg) * LOG2E - cum2


def _hgrn_chunk(rev, q, v, cum2, lk2, st_ref, ones_seg, bd_mask):
    n, c = HGRN_CHUNK, HGRN_SUB
    row = lax.broadcasted_iota(jnp.int32, (n, BRANCH_W), 0)
    lane = lax.broadcasted_iota(jnp.int32, (n, BRANCH_W), 1)
    offset = (_mod(lane, HEAD_DIM) - row) if rev else (row - _mod(lane, HEAD_DIM))
    pos = _mod(row, c)

    es = []
    for dlt in range(c):
        if dlt == 0:
            arg = cum2 + lk2
        else:
            lks = pltpu.roll(lk2, (n - dlt) if rev else dlt, axis=0)
            ok = (pos + dlt < c) if rev else (pos >= dlt)
            arg = jnp.where(ok, cum2 + lks, NEG_BIG)
        es.append((q * jnp.exp2(arg)).astype(BF16))
    seg = _dot(jnp.concatenate(es, axis=0), ones_seg)
    sc = jnp.zeros((n, BRANCH_W), F32)
    for dlt in range(c):
        sc = jnp.where(offset == dlt, seg[dlt * n:(dlt + 1) * n, :], sc)

    cl_parts = []
    for i in range(n // c):
        blk_rows = cum2[i * c:(i + 1) * c, :]
        if rev:
            cl_parts.append(blk_rows - cum2[(i + 1) * c:(i + 1) * c + 1, :] if i < n // c - 1 else blk_rows)
        else:
            cl_parts.append(blk_rows - cum2[i * c - 1:i * c, :] if i > 0 else blk_rows)
    qe = (q * jnp.exp2(jnp.concatenate(cl_parts, axis=0))).astype(BF16)
    parts = []
    for i in range(n // c):
        if (rev and i == n // c - 1) or (not rev and i == 0):
            parts.append(jnp.zeros((c, BRANCH_W), F32))
            continue
        if rev:
            ref = cum2[(i + 1) * c:(i + 1) * c + 1, :]
            valid = row >= (i + 1) * c
        else:
            ref = cum2[i * c - 1:i * c, :]
            valid = row < i * c
        rhs = jnp.exp2(jnp.where(valid, lk2 + ref, NEG_BIG))
        rhs_bd = jnp.where(bd_mask, jnp.concatenate([rhs] * N_HEADS, axis=0), 0.0).astype(BF16)
        parts.append(_dot_nt(qe[i * c:(i + 1) * c, :], rhs_bd))
    sc = sc + jnp.concatenate(parts, axis=0)
    v_bd = jnp.where(bd_mask, jnp.concatenate([v] * N_HEADS, axis=0), 0.0).astype(BF16)
    o = _dot(sc.astype(BF16), v_bd)

    st = st_ref[...]
    o = o + _dot_nt((q * jnp.exp2(cum2)).astype(BF16), st.astype(BF16))
    end = cum2[0:1, :] if rev else cum2[n - 1:n, :]
    kp = jnp.exp2(lk2 + end)
    upd = _dot(v.T.astype(BF16), kp.astype(BF16))
    st_ref[...] = st * jnp.exp2(end) + jnp.where(bd_mask, upd, 0.0)
    return o


def _hgrn_kernel(layer, q_f, f_f, v_f, q_b, f_b, v_b, lbf_ref, lbb_ref, o_f, o_b,
                 st_f, st_b, cum_f, cum_b, lk_f, lk_b):
    @pl.when(pl.program_id(1) == 0)
    def _():
        st_f[...] = jnp.zeros(st_f.shape, F32)
        st_b[...] = jnp.zeros(st_b.shape, F32)

    for rev, f_ref, lb_ref, cum_ref, lk_ref in ((False, f_f, lbf_ref, cum_f, lk_f), (True, f_b, lbb_ref, cum_b, lk_b)):
        cum2, lk2 = _hgrn_decay_logs(f_ref[0].astype(F32), _hgrn_lower_bound(lb_ref, layer), rev)
        cum_ref[...] = cum2
        lk_ref[...] = lk2
    rb = _div(lax.broadcasted_iota(jnp.int32, (BRANCH_W, BRANCH_W), 0), HEAD_DIM)
    cb = _div(lax.broadcasted_iota(jnp.int32, (BRANCH_W, BRANCH_W), 1), HEAD_DIM)
    bd_mask = rb == cb
    ones_seg = bd_mask.astype(BF16)
    n_chunks = ROW_TILE // HGRN_CHUNK

    def body(i, carry):
        cf = pl.multiple_of(i * HGRN_CHUNK, HGRN_CHUNK)
        cbk = pl.multiple_of((n_chunks - 1 - i) * HGRN_CHUNK, HGRN_CHUNK)
        sl_f = pl.ds(cf, HGRN_CHUNK)
        sl_b = pl.ds(cbk, HGRN_CHUNK)
        o_f[0, sl_f, :] = _hgrn_chunk(False, q_f[0, sl_f, :].astype(F32), v_f[0, sl_f, :].astype(F32),
                                      cum_f[sl_f, :], lk_f[sl_f, :], st_f, ones_seg, bd_mask)
        o_b[0, sl_b, :] = _hgrn_chunk(True, q_b[0, sl_b, :].astype(F32), v_b[0, sl_b, :].astype(F32),
                                      cum_b[sl_b, :], lk_b[sl_b, :], st_b, ones_seg, bd_mask)
        return carry

    lax.fori_loop(0, n_chunks, body, 0, unroll=2)


def _hgrn_call(p3, lb_fwd, lb_bwd, layer):
    bsz = p3.shape[0]
    base = COL_AQ // BRANCH_W

    def spec(tile, col):
        return pl.BlockSpec((1, ROW_TILE, BRANCH_W), lambda b, j: (b, tile(j), col))

    lb_spec = pl.BlockSpec((DEPTH, BRANCH_W), lambda b, j: (0, 0))
    out = jax.ShapeDtypeStruct((bsz, SEQ_ALL, BRANCH_W), F32)
    return pl.pallas_call(
        functools.partial(_hgrn_kernel, layer),
        grid=(bsz, N_TILES),
        in_specs=[spec(_fwd_tile, base), spec(_fwd_tile, base + 1), spec(_fwd_tile, base + 3),
                  spec(_bwd_tile, base), spec(_bwd_tile, base + 2), spec(_bwd_tile, base + 3),
                  lb_spec, lb_spec],
        out_specs=[spec(_fwd_tile, 0), spec(_bwd_tile, 0)],
        out_shape=[out, out],
        scratch_shapes=[pltpu.VMEM((BRANCH_W, BRANCH_W), F32)] * 2 + [pltpu.VMEM((ROW_TILE, BRANCH_W), F32)] * 4,
        compiler_params=_cparams(("parallel", "arbitrary")),
        name="hgrn_scan",
    )(p3, p3, p3, p3, p3, p3, lb_fwd, lb_bwd)


def _rope(xn, cos, sin):
    w = xn.shape[1]
    lane = lax.broadcasted_iota(jnp.int32, xn.shape, 1)
    nxt = pltpu.roll(xn, w - HEAD_DIM // 4, axis=1)
    prv = pltpu.roll(xn, HEAD_DIM // 4, axis=1)
    partner = jnp.where(_mod(lane, HEAD_DIM // 2) < HEAD_DIM // 4, nxt, prv)
    return xn * cos + partner * sin


def _qkprep_kernel(q_ref, k_ref, v_ref, cq_ref, sq_ref, ck_ref, sk_ref, wq_ref, wk_ref, segq_ref, segk_ref,
                   qo_ref, ko_ref, vo_ref):
    q = q_ref[0].astype(F32)
    k = k_ref[0].astype(F32)
    qn = q * lax.rsqrt(_segment_mean(q * q, segq_ref[...]) + LN_EPS) * wq_ref[...]
    kn = k * lax.rsqrt(_segment_mean(k * k, segk_ref[...]) + LN_EPS) * wk_ref[...]
    qo_ref[0] = (_rope(qn, cq_ref[...], sq_ref[...]) * (HEAD_DIM ** -0.5 * LOG2E)).astype(BF16)
    ko_ref[0] = _rope(kn, ck_ref[...], sk_ref[...]).astype(BF16)
    v = v_ref[0].astype(F32)
    lane = lax.broadcasted_iota(jnp.int32, v.shape, 1)
    vo_ref[0] = jnp.concatenate([jnp.where(lane < HEAD_DIM, v, 1.0), jnp.where(lane < HEAD_DIM, 1.0, v)],
                                axis=1).astype(BF16)


def _rope_tables():
    t = np.arange(N_LAT)
    nf = HEAD_DIM // 4
    inv_freq = (ROPE_BASE ** (-np.arange(nf, dtype=np.float32) / nf)).astype(np.float32)
    ang_r = ((t // GRID_W).astype(np.float32)[:, None] * inv_freq).astype(np.float64)
    ang_c = ((t % GRID_W).astype(np.float32)[:, None] * inv_freq).astype(np.float64)
    ang = np.concatenate([ang_r, ang_r, ang_c, ang_c], axis=1)
    sign = np.tile(np.concatenate([-np.ones(nf), np.ones(nf)]), 2)
    cos = np.concatenate([np.ones((N_CTX, HEAD_DIM)), np.cos(ang)], axis=0).astype(np.float32)
    sin = np.concatenate([np.zeros((N_CTX, HEAD_DIM)), np.sin(ang) * sign], axis=0).astype(np.float32)
    return cos, sin


def _q_pad_lanes_np(a):
    out = np.zeros(a.shape[:-1] + (N_HEADS * 128,), a.dtype)
    for h in range(N_HEADS):
        lo = h * 128 + (h // 2) * HEAD_DIM
        out[..., lo:lo + HEAD_DIM] = a[..., h * HEAD_DIM:(h + 1) * HEAD_DIM]
    return out


def _q_pad_lanes(a):
    parts = []
    zero = jnp.zeros(a.shape[:-1] + (HEAD_DIM,), a.dtype)
    for h in range(N_HEADS):
        ah = a[..., h * HEAD_DIM:(h + 1) * HEAD_DIM]
        parts += [ah, zero] if h // 2 == 0 else [zero, ah]
    return jnp.concatenate(parts, axis=-1)


def _seg_matrix(width):
    idx = np.arange(width) // HEAD_DIM
    return jnp.asarray((idx[:, None] == idx[None, :]).astype(np.float32) / HEAD_DIM, dtype=BF16)


def _qkprep_call(p3, q_norm, k_norm):
    bsz = p3.shape[0]
    cos, sin = _rope_tables()
    cq = _q_pad_lanes_np(np.tile(cos, (1, N_HEADS)))
    sq = _q_pad_lanes_np(np.tile(sin, (1, N_HEADS)))
    ck = np.tile(cos, (1, KV_HEADS))
    sk = np.tile(sin, (1, KV_HEADS))
    wq = _q_pad_lanes(jnp.tile(q_norm, N_HEADS)[None, :])
    wk = jnp.tile(k_norm, KV_HEADS)[None, :]
    qw, kw = N_HEADS * 128, KV_HEADS * HEAD_DIM
    tile = lambda w, col: pl.BlockSpec((1, ROW_TILE, w), lambda j, b: (b, j, col))
    tab = lambda w: pl.BlockSpec((ROW_TILE, w), lambda j, b: (j, 0))
    full = lambda r, w: pl.BlockSpec((r, w), lambda j, b: (0, 0))
    return pl.pallas_call(
        _qkprep_kernel,
        grid=(N_TILES, bsz),
        in_specs=[tile(qw, COL_DQ // qw), tile(kw, COL_DK // kw), tile(kw, COL_DV // kw),
                  tab(qw), tab(qw), tab(kw), tab(kw),
                  full(1, qw), full(1, kw), full(qw, qw), full(kw, kw)],
        out_specs=[pl.BlockSpec((1, ROW_TILE, qw), lambda j, b: (b, jnp.where(j == 0, LAT_TILES, j - 1), 0)),
                   tile(kw, 0), tile(KV_HEADS * kw, 0)],
        out_shape=[jax.ShapeDtypeStruct((bsz, SEQ_ALL, qw), BF16),
                   jax.ShapeDtypeStruct((bsz, SEQ_ALL, kw), BF16),
                   jax.ShapeDtypeStruct((bsz, SEQ_ALL, KV_HEADS * kw), BF16)],
        compiler_params=_cparams(("parallel", "parallel")),
        name="gqa_qk_prep",
    )(p3, p3, p3, cq, sq, ck, sk, wq, wk, _seg_matrix(qw), _seg_matrix(kw))


GQA_KCHUNK = 2048


def _gqa_attend(q2, k_ref, v_ref, g, chunks):
    m = acc = None
    for (k0, kn) in chunks:
        kc = k_ref[0, k0:k0 + kn, :]
        vc = v_ref[0, k0:k0 + kn, g * 128:(g + 1) * 128]
        s = _dot_nt(q2, kc)
        mc = jnp.max(s, axis=-1, keepdims=True)
        if m is None:
            m = mc
            acc = _dot(jnp.exp2(s - m).astype(BF16), vc)
        else:
            mn = jnp.maximum(m, mc)
            acc = jnp.exp2(m - mn) * acc + _dot(jnp.exp2(s - mn).astype(BF16), vc)
            m = mn
    sum_lane = (1 - g) * HEAD_DIM
    return acc / acc[:, sum_lane:sum_lane + 1]


GQA_LAT_ROWS = 2 * ROW_TILE


def _gqa_kernel(rows, chunks, q_ref, k_ref, v_ref, o_ref):
    lane = lax.broadcasted_iota(jnp.int32, (2 * rows, 128), 1)
    for g in range(KV_HEADS):
        q2 = jnp.concatenate([q_ref[0, :, (2 * g) * 128:(2 * g + 1) * 128],
                              q_ref[0, :, (2 * g + 1) * 128:(2 * g + 2) * 128]], axis=0)
        o2 = _gqa_attend(q2, k_ref, v_ref, g, chunks)
        o2 = jnp.where(_div(lane, HEAD_DIM) == g, o2, 0.0).astype(BF16)
        o_ref[0, :, (2 * g) * 128:(2 * g + 1) * 128] = o2[:rows]
        o_ref[0, :, (2 * g + 1) * 128:(2 * g + 2) * 128] = o2[rows:]


def _gqa_call(qz, kn, va):
    bsz = qz.shape[0]
    qw, kw = N_HEADS * 128, KV_HEADS * HEAD_DIM
    ctx_chunks = [(0, N_CTX)]
    all_chunks = ctx_chunks + [(N_CTX + i * GQA_KCHUNK, GQA_KCHUNK) for i in range(N_LAT // GQA_KCHUNK)]
    lat = pl.pallas_call(
        functools.partial(_gqa_kernel, GQA_LAT_ROWS, all_chunks),
        grid=(bsz, N_LAT // GQA_LAT_ROWS),
        in_specs=[
            pl.BlockSpec((1, GQA_LAT_ROWS, qw), lambda b, j: (b, j, 0)),
            pl.BlockSpec((1, SEQ_ALL, kw), lambda b, j: (b, 0, 0)),
            pl.BlockSpec((1, SEQ_ALL, KV_HEADS * kw), lambda b, j: (b, 0, 0)),
        ],
        out_specs=pl.BlockSpec((1, GQA_LAT_ROWS, qw), lambda b, j: (b, j, 0)),
        out_shape=jax.ShapeDtypeStruct((bsz, N_LAT, qw), BF16),
        compiler_params=_cparams(("parallel", "parallel")),
        name="gqa_attention",
    )(qz, kn, va)
    ctx = pl.pallas_call(
        functools.partial(_gqa_kernel, N_CTX, ctx_chunks),
        grid=(bsz,),
        in_specs=[
            pl.BlockSpec((1, N_CTX, qw), lambda b: (b, N_LAT // N_CTX, 0)),
            pl.BlockSpec((1, N_CTX, kw), lambda b: (b, 0, 0)),
            pl.BlockSpec((1, N_CTX, KV_HEADS * kw), lambda b: (b, 0, 0)),
        ],
        out_specs=pl.BlockSpec((1, N_CTX, qw), lambda b: (b, 0, 0)),
        out_shape=jax.ShapeDtypeStruct((bsz, N_CTX, qw), BF16),
        compiler_params=_cparams(("parallel",)),
        name="gqa_attention_ctx",
    )(qz, kn, va)
    return lat, ctx


NA_ROWS = ROW_TILE // GRID_W
NA_KTILES = 3
NA_KW = NA_KTILES * ROW_TILE


def _na_bias_tables(rpb):
    rows = N_LAT // GRID_W
    cq = np.arange(GRID_W)
    kc = np.arange(GRID_W)
    start_c = np.clip(cq - NA_WIN_C // 2, 0, GRID_W - NA_WIN_C)
    ok_c = (kc[None, :] >= start_c[:, None]) & (kc[None, :] < start_c[:, None] + NA_WIN_C)
    dcol = np.clip(kc[None, :] - cq[:, None] + NA_WIN_C - 1, 0, 2 * NA_WIN_C - 2)
    pick_c = (dcol[:, :, None] == np.arange(2 * NA_WIN_C - 1)).astype(np.float32) * LOG2E
    pick_r, ok_r = [], []
    for jb in (0, 1, LAT_TILES - 1):
        base = min(max(jb - 1, 0), LAT_TILES - NA_KTILES)
        r = jb * NA_ROWS + np.arange(NA_ROWS)
        kr = base * NA_ROWS + np.arange(NA_KTILES * NA_ROWS)
        start_r = np.clip(r - NA_WIN_R // 2, 0, rows - NA_WIN_R)
        ok_r.append((kr[None, :] >= start_r[:, None]) & (kr[None, :] < start_r[:, None] + NA_WIN_R))
        drow = np.clip(kr[None, :] - r[:, None] + NA_WIN_R - 1, 0, 2 * NA_WIN_R - 2)
        pick_r.append((drow[:, :, None] == np.arange(2 * NA_WIN_R - 1)).astype(np.float32))
    pick_r, ok_r = np.stack(pick_r), np.stack(ok_r)
    rows_sel = jnp.einsum('vamr,hrd->vhamd', pick_r, rpb, precision=HIGHEST)
    bias = jnp.einsum('vhamd,ckd->vhacmk', rows_sel, pick_c, precision=HIGHEST)
    ok = ok_r[:, None, :, None, :, None] & ok_c[None, None, None, :, None, :]
    bias = bias + np.where(ok, 0.0, NEG_BIG).astype(np.float32)
    return bias.reshape(3, N_HEADS, ROW_TILE, NA_KW).astype(BF16)


def _na_kernel(q_ref, k0_ref, k1_ref, k2_ref, kc_ref, v0_ref, v1_ref, v2_ref, vc_ref, bias_ref, o_ref):
    lane = lax.broadcasted_iota(jnp.int32, (ROW_TILE, BRANCH_W), 1)
    blk = _div(lane, HEAD_DIM)
    q = q_ref[0].astype(F32) * (HEAD_DIM ** -0.5 * LOG2E)

    def attend(key_refs, val_refs, bias_of):
        acc = jnp.zeros((ROW_TILE, BRANCH_W), F32)
        for h in range(N_HEADS):
            qz = jnp.where(blk == h, q, 0.0).astype(BF16)
            s = [_dot_nt(qz, kr[0]) for kr in key_refs]
            s = [si if bias_of(h, i) is None else si + bias_of(h, i) for i, si in enumerate(s)]
            m = functools.reduce(jnp.maximum, [jnp.max(si, axis=-1, keepdims=True) for si in s])
            p = [jnp.exp2(si - m) for si in s]
            l = functools.reduce(jnp.add, [jnp.sum(pi, axis=-1, keepdims=True) for pi in p])
            o = functools.reduce(jnp.add, [_dot(pi.astype(BF16), vr[0]) for pi, vr in zip(p, val_refs)])
            acc = jnp.where(blk == h, o / l, acc)
        o_ref[0] = acc.astype(BF16)

    @pl.when(pl.program_id(1) == 0)
    def _():
        attend([kc_ref], [vc_ref], lambda h, i: None)

    @pl.when(pl.program_id(1) > 0)
    def _():
        def bias_of(h, i):
            if i == NA_KTILES:
                return None
            return bias_ref[0, h, :, i * ROW_TILE:(i + 1) * ROW_TILE].astype(F32)
        attend([k0_ref, k1_ref, k2_ref, kc_ref], [v0_ref, v1_ref, v2_ref, vc_ref], bias_of)


def _na_call(p3, bias_tabs):
    bsz = p3.shape[0]
    cq = COL_CQ // BRANCH_W

    def key_tile(j, m):
        return 1 + jnp.clip(j - 2, 0, LAT_TILES - NA_KTILES) + m

    def spec_k(col, m):
        return pl.BlockSpec((1, ROW_TILE, BRANCH_W), lambda b, j: (b, key_tile(j, m), col))

    def spec_c(col):
        return pl.BlockSpec((1, ROW_TILE, BRANCH_W), lambda b, j: (b, 0, col))

    def variant(j):
        return jnp.where(j <= 1, 0, jnp.where(j == LAT_TILES, 2, 1))

    return pl.pallas_call(
        _na_kernel,
        grid=(bsz, N_TILES),
        in_specs=[pl.BlockSpec((1, ROW_TILE, BRANCH_W), lambda b, j: (b, j, cq)),
                  spec_k(cq + 1, 0), spec_k(cq + 1, 1), spec_k(cq + 1, 2), spec_c(cq + 1),
                  spec_k(cq + 2, 0), spec_k(cq + 2, 1), spec_k(cq + 2, 2), spec_c(cq + 2),
                  pl.BlockSpec((1, N_HEADS, ROW_TILE, NA_KW), lambda b, j: (variant(j), 0, 0, 0))],
        out_specs=pl.BlockSpec((1, ROW_TILE, BRANCH_W), lambda b, j: (b, j, 0)),
        out_shape=jax.ShapeDtypeStruct((bsz, SEQ_ALL, BRANCH_W), BF16),
        compiler_params=_cparams(("parallel", "parallel")),
        name="na_attention",
    )(p3, p3, p3, p3, p3, p3, p3, p3, p3, bias_tabs)


def _merge_kernel(split, hf_ref, hb_ref, ag_ref, sf_ref, sb_ref, xs_ref, z_ref, yc_ref, yd_ref,
                  g0_ref, g1_ref, g2_ref, g3_ref, *rest):
    if split:
        x_ref, xc_ref, *rest = rest
        x_res = jnp.where(pl.program_id(1) == 0, xc_ref[0], x_ref[0])
    else:
        x_ref, *rest = rest
        x_res = x_ref[0]
    gate_ref, hn_w, dsk, sn_w, seg_ref, wa, wb, wc, wd, wo, ln_g, ln_b, o_ref = rest
    oa = hf_ref[0] + hb_ref[0]
    ag = ag_ref[0].astype(F32)
    oa = oa * lax.rsqrt(_segment_mean(oa * oa, seg_ref[...]) + LN_EPS) * hn_w[...]
    br_a = oa * (ag * _sigmoid(ag))
    z = z_ref[0].astype(F32)
    yb = (sf_ref[0] + sb_ref[0] + xs_ref[0].astype(F32) * dsk[...]) * (z * _sigmoid(z))
    br_b = yb * lax.rsqrt(jnp.mean(yb * yb, axis=-1, keepdims=True) + LN_EPS) * sn_w[...]
    merged = (1.0 + jnp.tanh(g0_ref[0].astype(F32))) * _dot(br_a.astype(BF16), wa[...])
    merged += (1.0 + jnp.tanh(g1_ref[0].astype(F32))) * _dot(br_b.astype(BF16), wb[...])
    merged += (1.0 + jnp.tanh(g2_ref[0].astype(F32))) * _dot(yc_ref[0], wc[...])
    merged += (1.0 + jnp.tanh(g3_ref[0].astype(F32))) * _dot(yd_ref[0], wd[...])
    y = _dot(merged.astype(BF16), wo[...])
    o_ref[0] = _layer_norm(ALPHA * x_res + gate_ref[0] * y) * ln_g[...] + ln_b[...]


def _merge_call(layer, mods, x_src, p3, hf, hb, sf, sb, xact, yc, yd, hn_w, dsk, sn_w, wa, wb, wc, wd, wo,
                ln_g, ln_b, with_ctx):
    split = isinstance(x_src, tuple)
    assert with_ctx or not split
    bsz = p3.shape[0]
    t0 = 0 if with_ctx else 1
    tiles = N_TILES - t0
    t256 = lambda col: pl.BlockSpec((1, ROW_TILE, BRANCH_W), lambda b, j: (b, j + t0, col))
    t512 = pl.BlockSpec((1, ROW_TILE, 512), lambda b, j: (b, j, 0))
    tgate = lambda n: pl.BlockSpec((1, ROW_TILE, D_MODEL), lambda b, j: (b, j + t0, COL_GATE // D_MODEL + n))
    tx = pl.BlockSpec((1, ROW_TILE, D_MODEL), lambda b, j: (b, j + t0, 0))
    gate = pl.BlockSpec((1, 1, D_MODEL), lambda b, j: _mod_index(layer, 2)(b, j + t0))
    full = lambda a: pl.BlockSpec(a.shape, lambda b, j: (0,) * a.ndim)
    seg = _seg_matrix(BRANCH_W)
    consts = [hn_w, dsk, sn_w, seg, wa, wb, wc, wd, wo, ln_g, ln_b]
    if split:
        x_specs = [pl.BlockSpec((1, ROW_TILE, D_MODEL), lambda b, j: (b, jnp.maximum(j - 1, 0), 0)),
                   pl.BlockSpec((1, ROW_TILE, D_MODEL), lambda b, j: (b, 0, 0))]
        x_args = list(x_src)
    else:
        x_specs, x_args = [tx], [x_src]
    return pl.pallas_call(
        functools.partial(_merge_kernel, split),
        grid=(bsz, tiles),
        in_specs=[t256(0), t256(0), t256(COL_AQ // BRANCH_W + 4), t256(0), t256(0), t256(0),
                  t256(COL_Z // BRANCH_W), t256(0), t512,
                  tgate(0), tgate(1), tgate(2), tgate(3)] + x_specs + [gate]
                 + [full(a) for a in consts],
        out_specs=pl.BlockSpec((1, ROW_TILE, D_MODEL), lambda b, j: (b, j, 0)),
        out_shape=jax.ShapeDtypeStruct((bsz, tiles * ROW_TILE, D_MODEL), F32),
        compiler_params=_cparams(("parallel", "parallel")),
        name="merge_postnorm",
    )(hf, hb, p3, sf, sb, xact, p3, yc, yd, p3, p3, p3, p3, *x_args, mods, *consts)


FFN_CHUNK = 256


FFN_SUB = 2


def _ffn_kernel(x_ref, *refs):
    mod_refs, (wu_ref, wd_ref, ln_g, ln_b, o_ref) = refs[:3 * FFN_SUB], refs[3 * FFN_SUB:]
    tm = FFN_SUB * ROW_TILE
    x = x_ref[...]
    ln = _layer_norm(x)
    h = jnp.concatenate(
        [ln[u * ROW_TILE:(u + 1) * ROW_TILE] * (1.0 + mod_refs[3 * u + 1][0]) + mod_refs[3 * u][0]
         for u in range(FFN_SUB)], axis=0).astype(BF16)
    acc = jnp.zeros((tm, D_MODEL), F32)
    for c in range(FFN_HIDDEN // FFN_CHUNK):
        c0 = c * FFN_CHUNK
        g = _dot(h, wu_ref[:, c0:c0 + FFN_CHUNK])
        u = _dot(h, wu_ref[:, FFN_HIDDEN + c0:FFN_HIDDEN + c0 + FFN_CHUNK])
        a = (g * _sigmoid(g) * u).astype(BF16)
        acc = acc + _dot(a, wd_ref[c0:c0 + FFN_CHUNK, :])
    for u in range(FFN_SUB):
        rows = slice(u * ROW_TILE, (u + 1) * ROW_TILE)
        y = ALPHA * x[rows] + mod_refs[3 * u + 2][0] * acc[rows]
        o_ref[rows, :] = _layer_norm(y) * ln_g[...] + ln_b[...]


def _ffn_call(layer, mods, x1_flat, w_up, w_down, ln_g, ln_b, with_ctx):
    m = x1_flat.shape[0]
    tm = FFN_SUB * ROW_TILE
    seq_tiles, first = (N_TILES, 0) if with_ctx else (LAT_TILES, 1)
    mod_specs = [_flat_mod_spec(layer, 3 + which, FFN_SUB, u, seq_tiles, first)
                 for u in range(FFN_SUB) for which in range(3)]
    tx = pl.BlockSpec((tm, D_MODEL), lambda i: (i, 0))
    resident = lambda a: pl.BlockSpec(a.shape, lambda i: (0,) * a.ndim, pipeline_mode=pl.Buffered(1))
    full = lambda a: pl.BlockSpec(a.shape, lambda i: (0,) * a.ndim)
    return pl.pallas_call(
        _ffn_kernel,
        grid=(m // tm,),
        in_specs=[tx] + mod_specs + [resident(w_up), resident(w_down), full(ln_g), full(ln_b)],
        out_specs=tx,
        out_shape=jax.ShapeDtypeStruct(x1_flat.shape, F32),
        compiler_params=_cparams(("parallel",)),
        name="ffn_postnorm",
    )(x1_flat, *([mods] * (3 * FFN_SUB)), w_up, w_down, ln_g, ln_b)


def _pack_w_in(w):
    n_dt = 2 * N_HEADS
    o_dt = 5 * BRANCH_W + BRANCH_W + SSD_CONV_CH
    o_c = o_dt + n_dt
    o_dq = o_c + 3 * BRANCH_W
    o_dk = o_dq + BRANCH_W
    o_gate = o_dk + 2 * KV_HEADS * HEAD_DIM
    zeros = jnp.zeros((w.shape[0], COL_GATE - COL_DT - n_dt), w.dtype)
    cols = [w[:, :o_dt], w[:, o_c:o_dq], _q_pad_lanes(w[:, o_dq:o_dk]), w[:, o_dk:o_gate],
            w[:, o_dt:o_c], zeros, w[:, o_gate:]]
    packed = jnp.concatenate(cols, axis=1)
    assert packed.shape[1] == P_COLS
    return packed.astype(BF16)


def _pad_rows_d(wd):
    return _q_pad_lanes(wd.T).T


def kernel(x, c, ctx, c_ctx, ada_w, ada_b, w_in, hgrn_lb, hgrn_norm, ssd_conv_w, ssd_conv_b, ssd_dt_bias,
           ssd_a_log, ssd_d, ssd_norm, na_rpb, q_norm, k_norm, w_branch, w_out, ln1_g, ln1_b,
           ffn_w_up, ffn_w_down, ln2_g, ln2_b):
    bsz = x.shape[0]
    c8 = jnp.concatenate([c, c_ctx[None, :], jnp.zeros((8 - bsz - 1, D_MODEL), F32)], axis=0)
    mods = _ada_call(c8, ada_w, ada_b).reshape(DEPTH * 8 * 6, 1, D_MODEL)
    xa = (x, ctx)
    row = lambda v: v.reshape(1, -1)
    for l in range(DEPTH):
        last = l == DEPTH - 1
        p = _inproj_call(xa, mods, _pack_w_in(w_in[l]), l)
        p3 = p.reshape(bsz, SEQ_ALL, P_COLS)
        hf, hb = _hgrn_call(p3, hgrn_lb[0], hgrn_lb[1], l)
        xact = _conv_call(p3, ssd_conv_w[l], ssd_conv_b[l])
        dt_rows = jnp.swapaxes(p3[:, :, COL_DT:COL_DT + 8], 1, 2).astype(F32)
        sf, sb = _ssd_call(xact, p3, dt_rows, ssd_dt_bias[l], ssd_a_log[l])
        yc = _na_call(p3, _na_bias_tables(na_rpb[l]))
        qz, kn, va = _qkprep_call(p3, q_norm[l], k_norm[l])
        yd, yd_ctx = _gqa_call(qz, kn, va)
        if not last:
            yd = jnp.concatenate([yd_ctx, yd], axis=1)
        wb = w_branch[l].astype(BF16)
        x1 = _merge_call(l, mods, xa, p3, hf, hb, sf, sb, xact, yc, yd,
                         row(hgrn_norm[l]), row(jnp.repeat(ssd_d[l], HEAD_DIM)), row(ssd_norm[l]),
                         wb[0], wb[1], wb[2], _pad_rows_d(wb[3]), (0.5 * w_out[l]).astype(BF16),
                         row(ln1_g[l]), row(ln1_b[l]), with_ctx=not last)
        xa = _ffn_call(l, mods, x1.reshape(-1, D_MODEL), ffn_w_up[l].astype(BF16), ffn_w_down[l].astype(BF16),
                       row(ln2_g[l]), row(ln2_b[l]), with_ctx=not last).reshape(bsz, -1, D_MODEL)
    return xa
```

```python
import functools
import math

import jax
import jax.numpy as jnp
import numpy as np
from jax import lax
from jax.experimental import pallas as pl
from jax.experimental.pallas import tpu as pltpu

F32 = jnp.float32
BF16 = jnp.bfloat16
HIGHEST = lax.Precision.HIGHEST

D_MODEL = 1024
DEPTH = 2
GRID_W = 64
N_CTX = 256
N_LAT = 4096
SEQ_ALL = N_CTX + N_LAT
HEAD_DIM = 64
BRANCH_W = 256
N_HEADS = 4
SSD_STATE = 128
SSD_GROUPS = 2
SSD_CONV_K = 5
SSD_CONV_CH = BRANCH_W + 2 * SSD_GROUPS * SSD_STATE
KV_HEADS = 2
NA_WIN_R = 8
NA_WIN_C = 16
ROPE_BASE = 10000.0
LN_EPS = 1e-6
FFN_HIDDEN = 2816
ALPHA = (2.0 * DEPTH) ** 0.25

ROW_TILE = 256
N_TILES = SEQ_ALL // ROW_TILE
LAT_TILES = N_LAT // ROW_TILE
HGRN_CHUNK = 64
HGRN_SUB = 16
SSD_CHUNK = 128
NEG_BIG = -1e30
VMEM_LIMIT = 48 * 1024 * 1024

P_COLS = 8192
COL_AQ = 0
COL_Z = 1280
COL_XBC = 1536
COL_CQ = 2304
COL_DQ = 3072
COL_DK = 3584
COL_DV = 3712
COL_DT = 3840
COL_GATE = 4096


def _cparams(sem):
    return pltpu.CompilerParams(dimension_semantics=sem, vmem_limit_bytes=VMEM_LIMIT)


def _sigmoid(x):
    return 0.5 * jnp.tanh(0.5 * x) + 0.5


def _div(x, n):
    return lax.shift_right_logical(x, int(math.log2(n)))


def _mod(x, n):
    return x & (n - 1)


def _softplus(x):
    return jnp.maximum(x, 0.0) + jnp.log(1.0 + jnp.exp(-jnp.abs(x)))


def _layer_norm(x):
    mu = jnp.mean(x, axis=-1, keepdims=True)
    xc = x - mu
    var = jnp.mean(xc * xc, axis=-1, keepdims=True)
    return xc * lax.rsqrt(var + LN_EPS)


def _dot(a, b):
    return jnp.dot(a, b, preferred_element_type=F32)


def _dot_nt(a, b):
    return lax.dot_general(a, b, (((1,), (1,)), ((), ())), preferred_element_type=F32)


def _dot_hi(a, b):
    return jnp.dot(a, b, precision=HIGHEST, preferred_element_type=F32)


def _run_staggered(stages, lag=1):
    results = [None] * len(stages)
    done = [False] * len(stages)
    rnd = 0
    while not all(done):
        for g, gen in enumerate(stages):
            if rnd >= g * lag and not done[g]:
                try:
                    val = next(gen)
                    if val is not None:
                        results[g] = val
                except StopIteration:
                    done[g] = True
        rnd += 1
    return results


def _segment_mean(x, seg):
    hi = x.astype(BF16)
    lo = (x - hi.astype(F32)).astype(BF16)
    return _dot(hi, seg) + _dot(lo, seg)


def _ada_kernel(c_ref, w_ref, b_ref, o_ref):
    cv = c_ref[...]
    a = cv * _sigmoid(cv)
    o_ref[0] = _dot_hi(a, w_ref[0]) + b_ref[0]


def _ada_call(c8, ada_w, ada_b):
    n_l = ada_w.shape[0]
    return pl.pallas_call(
        _ada_kernel,
        grid=(n_l, 6),
        in_specs=[
            pl.BlockSpec((8, D_MODEL), lambda l, j: (0, 0)),
            pl.BlockSpec((1, D_MODEL, D_MODEL), lambda l, j: (l, 0, j)),
            pl.BlockSpec((1, 1, D_MODEL), lambda l, j: (l, 0, j)),
        ],
        out_specs=pl.BlockSpec((1, 8, D_MODEL), lambda l, j: (l, 0, j)),
        out_shape=jax.ShapeDtypeStruct((n_l, 8, 6 * D_MODEL), F32),
        compiler_params=_cparams(("arbitrary", "arbitrary")),
        name="ada_mod",
    )(c8, ada_w, ada_b.reshape(n_l, 1, 6 * D_MODEL))


def _mod_index(layer, which):
    def index(b, j):
        return (layer * 48 + jnp.where(j == 0, 4, b) * 6 + which, 0, 0)
    return index


def _flat_mod_spec(layer, which, sub_tiles, u, seq_tiles, first_tile):
    def index(i, *_):
        t = i * sub_tiles + u
        b = t // seq_tiles
        j = t % seq_tiles + first_tile
        return (layer * 48 + jnp.where(j == 0, 4, b) * 6 + which, 0, 0)
    return pl.BlockSpec((1, 1, D_MODEL), index)


PROJ_SUB = 2
PROJ_TN = 1024


def _split_row_specs(sub_tiles, u):
    def tile_of(i):
        t = i * sub_tiles + u
        return t // N_TILES, t % N_TILES
    lat = pl.BlockSpec((1, ROW_TILE, D_MODEL), lambda i: (tile_of(i)[0], jnp.maximum(tile_of(i)[1] - 1, 0), 0))
    ctx = pl.BlockSpec((1, ROW_TILE, D_MODEL), lambda i: (tile_of(i)[0], 0, 0))
    return [lat, ctx]


def _inproj_kernel(split, *refs):
    n_x = 2 * PROJ_SUB if split else 1
    x_refs, mod_refs, (w_ref, o_ref) = refs[:n_x], refs[n_x:n_x + 2 * PROJ_SUB], refs[n_x + 2 * PROJ_SUB:]
    parts = []
    for u in range(PROJ_SUB):
        if split:
            j = (pl.program_id(0) * PROJ_SUB + u) % N_TILES
            xt = jnp.where(j == 0, x_refs[2 * u + 1][0], x_refs[2 * u][0])
        else:
            xt = x_refs[0][u * ROW_TILE:(u + 1) * ROW_TILE, :]
        sh, sc = mod_refs[2 * u][0], mod_refs[2 * u + 1][0]
        parts.append((_layer_norm(xt) * (1.0 + sc) + sh).astype(BF16))
    h = jnp.concatenate(parts, axis=0)
    for n in range(P_COLS // PROJ_TN):
        cols = slice(n * PROJ_TN, (n + 1) * PROJ_TN)
        acc = _dot(h, w_ref[:, cols])
        if n * PROJ_TN >= COL_GATE:
            acc = 0.5 * acc
        o_ref[:, cols] = acc.astype(BF16)


def _inproj_call(x_src, mods, wp, layer):
    split = isinstance(x_src, tuple)
    bsz = (x_src[0] if split else x_src).shape[0]
    m = bsz * SEQ_ALL
    tm = PROJ_SUB * ROW_TILE
    mod_specs = []
    for u in range(PROJ_SUB):
        mod_specs += [_flat_mod_spec(layer, 0, PROJ_SUB, u, N_TILES, 0),
                      _flat_mod_spec(layer, 1, PROJ_SUB, u, N_TILES, 0)]
    if split:
        x_specs = [s for u in range(PROJ_SUB) for s in _split_row_specs(PROJ_SUB, u)]
        x_args = list(x_src) * PROJ_SUB
    else:
        x_specs = [pl.BlockSpec((tm, D_MODEL), lambda i: (i, 0))]
        x_args = [x_src.reshape(m, D_MODEL)]
    return pl.pallas_call(
        functools.partial(_inproj_kernel, split),
        grid=(m // tm,),
        in_specs=x_specs + mod_specs
                 + [pl.BlockSpec((D_MODEL, P_COLS), lambda i: (0, 0), pipeline_mode=pl.Buffered(1))],
        out_specs=pl.BlockSpec((tm, P_COLS), lambda i: (i, 0)),
        out_shape=jax.ShapeDtypeStruct((m, P_COLS), BF16),
        compiler_params=_cparams(("parallel",)),
        name="in_proj",
    )(*x_args, *([mods] * (2 * PROJ_SUB)), wp)


CONV_PAD = 8


def _conv_kernel(x_ref, w_ref, b_ref, o_ref, xs_ref):
    zeros = jnp.zeros((CONV_PAD, 128), F32)
    xs_ref[0:CONV_PAD, :] = zeros
    xs_ref[CONV_PAD + SEQ_ALL:, :] = zeros
    xs_ref[CONV_PAD:CONV_PAD + SEQ_ALL, :] = x_ref[0].astype(F32)
    w = w_ref[...]
    row = lax.broadcasted_iota(jnp.int32, (ROW_TILE, 128), 0)
    half = SSD_CONV_K // 2
    for ci in range(N_TILES):
        t0 = ci * ROW_TILE
        acc = jnp.broadcast_to(b_ref[...], (ROW_TILE, 128))
        for k in range(SSD_CONV_K):
            j = k - half
            xk = xs_ref[CONV_PAD + t0 + j:CONV_PAD + t0 + j + ROW_TILE, :]
            if ci == 0 and j > 0:
                xk = jnp.where(row + j < ROW_TILE, xk, 0.0)
            if ci == 1 and j < 0:
                xk = jnp.where(row + j >= 0, xk, 0.0)
            acc = acc + xk * w[k:k + 1, :]
        o_ref[0, t0:t0 + ROW_TILE, :] = (acc * _sigmoid(acc)).astype(BF16)


def _conv_call(p3, conv_w, conv_b):
    bsz = p3.shape[0]
    n_ct = SSD_CONV_CH // 128
    return pl.pallas_call(
        _conv_kernel,
        grid=(bsz, n_ct),
        in_specs=[
            pl.BlockSpec((1, SEQ_ALL, 128), lambda b, c: (b, 0, COL_XBC // 128 + c)),
            pl.BlockSpec((SSD_CONV_K, 128), lambda b, c: (0, c)),
            pl.BlockSpec((1, 128), lambda b, c: (0, c)),
        ],
        out_specs=pl.BlockSpec((1, SEQ_ALL, 128), lambda b, c: (b, 0, c)),
        out_shape=jax.ShapeDtypeStruct((bsz, SEQ_ALL, SSD_CONV_CH), BF16),
        scratch_shapes=[pltpu.VMEM((SEQ_ALL + 2 * CONV_PAD, 128), F32)],
        compiler_params=_cparams(("parallel", "parallel")),
        name="ssd_conv",
    )(p3, conv_w, conv_b.reshape(1, SSD_CONV_CH))


def _fwd_tile(j):
    return j


def _bwd_tile(j):
    return jnp.where(j == 0, 0, N_TILES - j)


def _tri(n, upper):
    r = lax.broadcasted_iota(jnp.int32, (n, n), 0)
    c = lax.broadcasted_iota(jnp.int32, (n, n), 1)
    return (c >= r) if upper else (c <= r)


def _ssd_decays(rev, dtc_ref, dtr_ref, bias_r, alog_r, bias_c, alog_c):
    n = SSD_CHUNK
    dt_c = _softplus(dtc_ref[0].astype(F32) + bias_r)
    g_c = dt_c * (-jnp.exp(alog_r) * LOG2E)
    dt_r = _softplus(dtr_ref[0] + bias_c)
    g_r = dt_r * (-jnp.exp(alog_c) * LOG2E)
    cum_c = _cumsum_rows(g_c, rev, n)
    cum_r = [_cumsum_cols(g_r[:, ci * n:(ci + 1) * n], rev) for ci in range(ROW_TILE // n)]
    return dt_c, cum_c, cum_r


def _ssd_stream(rev, g, decays, xa_ref, s_ref, o_ref):
    d = 1 if rev else 0
    n = SSD_CHUNK
    dt_c, cum_c_all, cum_r_all = decays
    keep = _tri(n, rev)
    lane = lax.broadcasted_iota(jnp.int32, (n, 128), 1)
    first = lane < HEAD_DIM
    n_chunks = ROW_TILE // n
    col0 = 4 * d + 2 * g
    end_row = 0 if rev else n - 1
    for ci in (range(n_chunks - 1, -1, -1) if rev else range(n_chunks)):
        c0 = ci * n
        cum_c = cum_c_all[c0:c0 + n, :]
        cum_r = cum_r_all[ci]
        bm = xa_ref[0, c0:c0 + n, BRANCH_W + g * SSD_STATE:BRANCH_W + (g + 1) * SSD_STATE]
        cm = xa_ref[0, c0:c0 + n, BRANCH_W + (SSD_GROUPS + g) * SSD_STATE:
                    BRANCH_W + (SSD_GROUPS + g + 1) * SSD_STATE]
        x2 = xa_ref[0, c0:c0 + n, g * 128:(g + 1) * 128].astype(F32)
        gmat = _dot_nt(cm, bm)
        s_old = s_ref[...]
        qs = _dot(cm, s_old.astype(BF16))
        yield
        bc0 = cum_c[:, col0:col0 + 1]
        bc1 = cum_c[:, col0 + 1:col0 + 2]
        br0 = cum_r[col0:col0 + 1, :]
        br1 = cum_r[col0 + 1:col0 + 2, :]
        p0 = jnp.exp2(jnp.where(keep, bc0 - br0, NEG_BIG)) * gmat
        p1 = jnp.exp2(jnp.where(keep, bc1 - br1, NEG_BIG)) * gmat
        dt2 = jnp.where(first, dt_c[c0:c0 + n, col0:col0 + 1], dt_c[c0:c0 + n, col0 + 1:col0 + 2])
        v2 = x2 * dt2
        v2b = v2.astype(BF16)
        yield
        pv = jnp.where(first, _dot(p0.astype(BF16), v2b), _dot(p1.astype(BF16), v2b))
        bc2 = jnp.where(first, bc0, bc1)
        e0 = cum_c[end_row:end_row + 1, col0:col0 + 1]
        e1 = cum_c[end_row:end_row + 1, col0 + 1:col0 + 2]
        end2 = jnp.where(first[0:1, :], e0, e1)
        wv = (v2 * jnp.exp2(end2 - bc2)).astype(BF16)
        upd = lax.dot_general(bm, wv, (((0,), (0,)), ((), ())), preferred_element_type=F32)
        yield
        o_ref[0, c0:c0 + n, g * 128:(g + 1) * 128] = pv + jnp.exp2(bc2) * qs
        s_ref[...] = s_old * jnp.exp2(end2) + upd
        yield


def _ssd_kernel(xa_f, xa_b, dtc_f, dtc_b, dtr_f, dtr_b, bias_r, alog_r, bias_c, alog_c,
                o_f, o_b, s_ref):
    @pl.when(pl.program_id(1) == 0)
    def _():
        s_ref[...] = jnp.zeros(s_ref.shape, F32)

    br, ar, bc, ac = bias_r[...], alog_r[...], bias_c[...], alog_c[...]
    dec_f = _ssd_decays(False, dtc_f, dtr_f, br, ar, bc, ac)
    dec_b = _ssd_decays(True, dtc_b, dtr_b, br, ar, bc, ac)
    _run_staggered([_ssd_stream(False, 0, dec_f, xa_f, s_ref.at[0, 0], o_f),
                    _ssd_stream(True, 0, dec_b, xa_b, s_ref.at[1, 0], o_b),
                    _ssd_stream(False, 1, dec_f, xa_f, s_ref.at[0, 1], o_f),
                    _ssd_stream(True, 1, dec_b, xa_b, s_ref.at[1, 1], o_b)])


def _ssd_call(xact, p3, dt_rows, dt_bias, a_log):
    bsz = xact.shape[0]
    bias8 = dt_bias.reshape(8).astype(F32)
    alog8 = a_log.reshape(8).astype(F32)
    pad_row = lambda v: jnp.zeros((1, 128), F32).at[0, :8].set(v)
    col = lambda v: jnp.broadcast_to(v[:, None], (8, ROW_TILE))

    def spec_x(tile):
        return pl.BlockSpec((1, ROW_TILE, SSD_CONV_CH), lambda b, j: (b, tile(j), 0))

    def spec_dtc(tile):
        return pl.BlockSpec((1, ROW_TILE, 128), lambda b, j: (b, tile(j), COL_DT // 128))

    def spec_dtr(tile):
        return pl.BlockSpec((1, 8, ROW_TILE), lambda b, j: (b, 0, tile(j)))

    def spec_o(tile):
        return pl.BlockSpec((1, ROW_TILE, BRANCH_W), lambda b, j: (b, tile(j), 0))

    small_r = pl.BlockSpec((1, 128), lambda b, j: (0, 0))
    small_c = pl.BlockSpec((8, ROW_TILE), lambda b, j: (0, 0))
    out = jax.ShapeDtypeStruct((bsz, SEQ_ALL, BRANCH_W), F32)
    return pl.pallas_call(
        _ssd_kernel,
        grid=(bsz, N_TILES),
        in_specs=[spec_x(_fwd_tile), spec_x(_bwd_tile), spec_dtc(_fwd_tile), spec_dtc(_bwd_tile),
                  spec_dtr(_fwd_tile), spec_dtr(_bwd_tile), small_r, small_r, small_c, small_c],
        out_specs=[spec_o(_fwd_tile), spec_o(_bwd_tile)],
        out_shape=[out, out],
        scratch_shapes=[pltpu.VMEM((2, SSD_GROUPS, SSD_STATE, 128), F32)],
        compiler_params=_cparams(("parallel", "arbitrary")),
        name="ssd_scan",
    )(xact, xact, p3, p3, dt_rows, dt_rows, pad_row(bias8), pad_row(alog8), col(bias8), col(alog8))


def _hgrn_lower_bound(lb_ref, layer):
    p = lb_ref[...]
    e = jnp.exp(p - jnp.max(p, axis=0, keepdims=True))
    sm = e / jnp.sum(e, axis=0, keepdims=True)
    acc = sm[0:1, :]
    for i in range(1, layer + 1):
        acc = acc + sm[i:i + 1, :]
    return acc - sm[0:1, :]


LOG2E = 1.4426950408889634


def _cumsum_rows(x, rev, block):
    n = x.shape[0]
    r_i = lax.broadcasted_iota(jnp.int32, (n, n), 0)
    c_i = lax.broadcasted_iota(jnp.int32, (n, n), 1)
    tri = ((c_i >= r_i) if rev else (c_i <= r_i)) & (_div(r_i, block) == _div(c_i, block))
    tri = tri.astype(BF16)
    hi = x.astype(BF16)
    rest = x - hi.astype(F32)
    mid = rest.astype(BF16)
    lo = (rest - mid.astype(F32)).astype(BF16)
    return _dot(tri, hi) + _dot(tri, mid) + _dot(tri, lo)


def _cumsum_cols(x, rev):
    n = x.shape[1]
    r_i = lax.broadcasted_iota(jnp.int32, (n, n), 0)
    c_i = lax.broadcasted_iota(jnp.int32, (n, n), 1)
    tri = ((r_i >= c_i) if rev else (r_i <= c_i)).astype(BF16)
    hi = x.astype(BF16)
    rest = x - hi.astype(F32)
    mid = rest.astype(BF16)
    lo = (rest - mid.astype(F32)).astype(BF16)
    return _dot(hi, tri) + _dot(mid, tri) + _dot(lo, tri)


def _hgrn_decay_logs(f, lb, rev):
    fg = lb + (1.0 - lb) * _sigmoid(f)
    cum2 = _cumsum_rows(jnp.log(fg) * LOG2E, rev, HGRN_CHUNK)
    return cum2, jnp.log(1.0 - fg) * LOG2E - cum2


def _hgrn_chunk(rev, q, v, cum2, lk2, st_ref, out_ref, ones_seg, bd_mask):
    n, c = HGRN_CHUNK, HGRN_SUB
    row = lax.broadcasted_iota(jnp.int32, (n, BRANCH_W), 0)
    lane = lax.broadcasted_iota(jnp.int32, (n, BRANCH_W), 1)
    offset = (_mod(lane, HEAD_DIM) - row) if rev else (row - _mod(lane, HEAD_DIM))
    pos = _mod(row, c)

    es = []
    for dlt in range(c):
        if dlt == 0:
            arg = cum2 + lk2
        else:
            lks = pltpu.roll(lk2, (n - dlt) if rev else dlt, axis=0)
            ok = (pos + dlt < c) if rev else (pos >= dlt)
            arg = jnp.where(ok, cum2 + lks, NEG_BIG)
        es.append((q * jnp.exp2(arg)).astype(BF16))
        if dlt % 4 == 3:
            yield
    seg = _dot(jnp.concatenate(es, axis=0), ones_seg)

    cl_parts = []
    for i in range(n // c):
        blk_rows = cum2[i * c:(i + 1) * c, :]
        if rev:
            cl_parts.append(blk_rows - cum2[(i + 1) * c:(i + 1) * c + 1, :] if i < n // c - 1 else blk_rows)
        else:
            cl_parts.append(blk_rows - cum2[i * c - 1:i * c, :] if i > 0 else blk_rows)
    qe = (q * jnp.exp2(jnp.concatenate(cl_parts, axis=0))).astype(BF16)
    parts = []
    for i in range(n // c):
        if (rev and i == n // c - 1) or (not rev and i == 0):
            parts.append(jnp.zeros((c, BRANCH_W), F32))
            continue
        if rev:
            ref = cum2[(i + 1) * c:(i + 1) * c + 1, :]
            valid = row >= (i + 1) * c
        else:
            ref = cum2[i * c - 1:i * c, :]
            valid = row < i * c
        rhs = jnp.exp2(jnp.where(valid, lk2 + ref, NEG_BIG))
        rhs_bd = jnp.where(bd_mask, jnp.concatenate([rhs] * N_HEADS, axis=0), 0.0).astype(BF16)
        parts.append(_dot_nt(qe[i * c:(i + 1) * c, :], rhs_bd))
        yield
    v_bd = jnp.where(bd_mask, jnp.concatenate([v] * N_HEADS, axis=0), 0.0).astype(BF16)
    st = st_ref[...]
    o_prev = _dot_nt((q * jnp.exp2(cum2)).astype(BF16), st.astype(BF16))
    end = cum2[0:1, :] if rev else cum2[n - 1:n, :]
    kp = jnp.exp2(lk2 + end)
    upd = _dot(v.T.astype(BF16), kp.astype(BF16))
    st_ref[...] = st * jnp.exp2(end) + jnp.where(bd_mask, upd, 0.0)
    yield
    sc = jnp.zeros((n, BRANCH_W), F32)
    for dlt in range(c):
        sc = jnp.where(offset == dlt, seg[dlt * n:(dlt + 1) * n, :], sc)
        if dlt % 8 == 7:
            yield
    sc = sc + jnp.concatenate(parts, axis=0)
    out_ref[...] = _dot(sc.astype(BF16), v_bd) + o_prev
    yield


def _hgrn_kernel(layer, q_f, f_f, v_f, q_b, f_b, v_b, lbf_ref, lbb_ref, o_f, o_b,
                 st_f, st_b, cum_f, cum_b, lk_f, lk_b):
    @pl.when(pl.program_id(1) == 0)
    def _():
        st_f[...] = jnp.zeros(st_f.shape, F32)
        st_b[...] = jnp.zeros(st_b.shape, F32)

    for rev, f_ref, lb_ref, cum_ref, lk_ref in ((False, f_f, lbf_ref, cum_f, lk_f), (True, f_b, lbb_ref, cum_b, lk_b)):
        cum2, lk2 = _hgrn_decay_logs(f_ref[0].astype(F32), _hgrn_lower_bound(lb_ref, layer), rev)
        cum_ref[...] = cum2
        lk_ref[...] = lk2
    rb = _div(lax.broadcasted_iota(jnp.int32, (BRANCH_W, BRANCH_W), 0), HEAD_DIM)
    cb = _div(lax.broadcasted_iota(jnp.int32, (BRANCH_W, BRANCH_W), 1), HEAD_DIM)
    bd_mask = rb == cb
    ones_seg = bd_mask.astype(BF16)
    n_chunks = ROW_TILE // HGRN_CHUNK

    def stream(rev, q_ref, v_ref, cum_ref, lk_ref, st_ref, o_ref):
        for i in (range(n_chunks - 1, -1, -1) if rev else range(n_chunks)):
            sl = slice(i * HGRN_CHUNK, (i + 1) * HGRN_CHUNK)
            yield from _hgrn_chunk(rev, q_ref[0, sl, :].astype(F32), v_ref[0, sl, :].astype(F32),
                                   cum_ref[sl, :], lk_ref[sl, :], st_ref, o_ref.at[0, sl, :], ones_seg, bd_mask)

    _run_staggered([stream(False, q_f, v_f, cum_f, lk_f, st_f, o_f),
                    stream(True, q_b, v_b, cum_b, lk_b, st_b, o_b)], lag=4)


def _hgrn_call(p3, lb_fwd, lb_bwd, layer):
    bsz = p3.shape[0]
    base = COL_AQ // BRANCH_W

    def spec(tile, col):
        return pl.BlockSpec((1, ROW_TILE, BRANCH_W), lambda b, j: (b, tile(j), col))

    lb_spec = pl.BlockSpec((DEPTH, BRANCH_W), lambda b, j: (0, 0))
    out = jax.ShapeDtypeStruct((bsz, SEQ_ALL, BRANCH_W), F32)
    return pl.pallas_call(
        functools.partial(_hgrn_kernel, layer),
        grid=(bsz, N_TILES),
        in_specs=[spec(_fwd_tile, base), spec(_fwd_tile, base + 1), spec(_fwd_tile, base + 3),
                  spec(_bwd_tile, base), spec(_bwd_tile, base + 2), spec(_bwd_tile, base + 3),
                  lb_spec, lb_spec],
        out_specs=[spec(_fwd_tile, 0), spec(_bwd_tile, 0)],
        out_shape=[out, out],
        scratch_shapes=[pltpu.VMEM((BRANCH_W, BRANCH_W), F32)] * 2 + [pltpu.VMEM((ROW_TILE, BRANCH_W), F32)] * 4,
        compiler_params=_cparams(("parallel", "arbitrary")),
        name="hgrn_scan",
    )(p3, p3, p3, p3, p3, p3, lb_fwd, lb_bwd)


def _rope(xn, cos, sin):
    w = xn.shape[1]
    lane = lax.broadcasted_iota(jnp.int32, xn.shape, 1)
    nxt = pltpu.roll(xn, w - HEAD_DIM // 4, axis=1)
    prv = pltpu.roll(xn, HEAD_DIM // 4, axis=1)
    partner = jnp.where(_mod(lane, HEAD_DIM // 2) < HEAD_DIM // 4, nxt, prv)
    return xn * cos + partner * sin


def _qkprep_kernel(q_ref, k_ref, v_ref, cq_ref, sq_ref, ck_ref, sk_ref, wq_ref, wk_ref, segq_ref, segk_ref,
                   qo_ref, ko_ref, vo_ref):
    q = q_ref[0].astype(F32)
    k = k_ref[0].astype(F32)
    qn = q * lax.rsqrt(_segment_mean(q * q, segq_ref[...]) + LN_EPS) * wq_ref[...]
    kn = k * lax.rsqrt(_segment_mean(k * k, segk_ref[...]) + LN_EPS) * wk_ref[...]
    qo_ref[0] = (_rope(qn, cq_ref[...], sq_ref[...]) * (HEAD_DIM ** -0.5 * LOG2E)).astype(BF16)
    ko_ref[0] = _rope(kn, ck_ref[...], sk_ref[...]).astype(BF16)
    v = v_ref[0].astype(F32)
    lane = lax.broadcasted_iota(jnp.int32, v.shape, 1)
    vo_ref[0] = jnp.concatenate([jnp.where(lane < HEAD_DIM, v, 1.0), jnp.where(lane < HEAD_DIM, 1.0, v)],
                                axis=1).astype(BF16)


def _rope_tables():
    t = np.arange(N_LAT)
    nf = HEAD_DIM // 4
    inv_freq = (ROPE_BASE ** (-np.arange(nf, dtype=np.float32) / nf)).astype(np.float32)
    ang_r = ((t // GRID_W).astype(np.float32)[:, None] * inv_freq).astype(np.float64)
    ang_c = ((t % GRID_W).astype(np.float32)[:, None] * inv_freq).astype(np.float64)
    ang = np.concatenate([ang_r, ang_r, ang_c, ang_c], axis=1)
    sign = np.tile(np.concatenate([-np.ones(nf), np.ones(nf)]), 2)
    cos = np.concatenate([np.ones((N_CTX, HEAD_DIM)), np.cos(ang)], axis=0).astype(np.float32)
    sin = np.concatenate([np.zeros((N_CTX, HEAD_DIM)), np.sin(ang) * sign], axis=0).astype(np.float32)
    return cos, sin


def _q_pad_lanes_np(a):
    out = np.zeros(a.shape[:-1] + (N_HEADS * 128,), a.dtype)
    for h in range(N_HEADS):
        lo = h * 128 + (h // 2) * HEAD_DIM
        out[..., lo:lo + HEAD_DIM] = a[..., h * HEAD_DIM:(h + 1) * HEAD_DIM]
    return out


def _q_pad_lanes(a):
    parts = []
    zero = jnp.zeros(a.shape[:-1] + (HEAD_DIM,), a.dtype)
    for h in range(N_HEADS):
        ah = a[..., h * HEAD_DIM:(h + 1) * HEAD_DIM]
        parts += [ah, zero] if h // 2 == 0 else [zero, ah]
    return jnp.concatenate(parts, axis=-1)


def _seg_matrix(width):
    idx = np.arange(width) // HEAD_DIM
    return jnp.asarray((idx[:, None] == idx[None, :]).astype(np.float32) / HEAD_DIM, dtype=BF16)


def _qkprep_call(p3, q_norm, k_norm):
    bsz = p3.shape[0]
    cos, sin = _rope_tables()
    cq = _q_pad_lanes_np(np.tile(cos, (1, N_HEADS)))
    sq = _q_pad_lanes_np(np.tile(sin, (1, N_HEADS)))
    ck = np.tile(cos, (1, KV_HEADS))
    sk = np.tile(sin, (1, KV_HEADS))
    wq = _q_pad_lanes(jnp.tile(q_norm, N_HEADS)[None, :])
    wk = jnp.tile(k_norm, KV_HEADS)[None, :]
    qw, kw = N_HEADS * 128, KV_HEADS * HEAD_DIM
    tile = lambda w, col: pl.BlockSpec((1, ROW_TILE, w), lambda j, b: (b, j, col))
    tab = lambda w: pl.BlockSpec((ROW_TILE, w), lambda j, b: (j, 0))
    full = lambda r, w: pl.BlockSpec((r, w), lambda j, b: (0, 0))
    return pl.pallas_call(
        _qkprep_kernel,
        grid=(N_TILES, bsz),
        in_specs=[tile(qw, COL_DQ // qw), tile(kw, COL_DK // kw), tile(kw, COL_DV // kw),
                  tab(qw), tab(qw), tab(kw), tab(kw),
                  full(1, qw), full(1, kw), full(qw, qw), full(kw, kw)],
        out_specs=[pl.BlockSpec((1, ROW_TILE, qw), lambda j, b: (b, jnp.where(j == 0, LAT_TILES, j - 1), 0)),
                   tile(kw, 0), tile(KV_HEADS * kw, 0)],
        out_shape=[jax.ShapeDtypeStruct((bsz, SEQ_ALL, qw), BF16),
                   jax.ShapeDtypeStruct((bsz, SEQ_ALL, kw), BF16),
                   jax.ShapeDtypeStruct((bsz, SEQ_ALL, KV_HEADS * kw), BF16)],
        compiler_params=_cparams(("parallel", "parallel")),
        name="gqa_qk_prep",
    )(p3, p3, p3, cq, sq, ck, sk, wq, wk, _seg_matrix(qw), _seg_matrix(kw))


GQA_KCHUNK = 2048


def _gqa_attend(q2, k_ref, v_ref, g, chunks):
    m = acc = None
    for (k0, kn) in chunks:
        kc = k_ref[0, k0:k0 + kn, :]
        vc = v_ref[0, k0:k0 + kn, g * 128:(g + 1) * 128]
        s = _dot_nt(q2, kc)
        mc = jnp.max(s, axis=-1, keepdims=True)
        if m is None:
            m = mc
            acc = _dot(jnp.exp2(s - m).astype(BF16), vc)
        else:
            mn = jnp.maximum(m, mc)
            acc = jnp.exp2(m - mn) * acc + _dot(jnp.exp2(s - mn).astype(BF16), vc)
            m = mn
    sum_lane = (1 - g) * HEAD_DIM
    return acc / acc[:, sum_lane:sum_lane + 1]


GQA_LAT_ROWS = 2 * ROW_TILE


def _gqa_kernel(rows, chunks, q_ref, k_ref, v_ref, o_ref):
    lane = lax.broadcasted_iota(jnp.int32, (2 * rows, 128), 1)
    for g in range(KV_HEADS):
        q2 = jnp.concatenate([q_ref[0, :, (2 * g) * 128:(2 * g + 1) * 128],
                              q_ref[0, :, (2 * g + 1) * 128:(2 * g + 2) * 128]], axis=0)
        o2 = _gqa_attend(q2, k_ref, v_ref, g, chunks)
        o2 = jnp.where(_div(lane, HEAD_DIM) == g, o2, 0.0).astype(BF16)
        o_ref[0, :, (2 * g) * 128:(2 * g + 1) * 128] = o2[:rows]
        o_ref[0, :, (2 * g + 1) * 128:(2 * g + 2) * 128] = o2[rows:]


def _gqa_call(qz, kn, va):
    bsz = qz.shape[0]
    qw, kw = N_HEADS * 128, KV_HEADS * HEAD_DIM
    ctx_chunks = [(0, N_CTX)]
    all_chunks = ctx_chunks + [(N_CTX + i * GQA_KCHUNK, GQA_KCHUNK) for i in range(N_LAT // GQA_KCHUNK)]
    lat = pl.pallas_call(
        functools.partial(_gqa_kernel, GQA_LAT_ROWS, all_chunks),
        grid=(bsz, N_LAT // GQA_LAT_ROWS),
        in_specs=[
            pl.BlockSpec((1, GQA_LAT_ROWS, qw), lambda b, j: (b, j, 0)),
            pl.BlockSpec((1, SEQ_ALL, kw), lambda b, j: (b, 0, 0)),
            pl.BlockSpec((1, SEQ_ALL, KV_HEADS * kw), lambda b, j: (b, 0, 0)),
        ],
        out_specs=pl.BlockSpec((1, GQA_LAT_ROWS, qw), lambda b, j: (b, j, 0)),
        out_shape=jax.ShapeDtypeStruct((bsz, N_LAT, qw), BF16),
        compiler_params=_cparams(("parallel", "parallel")),
        name="gqa_attention",
    )(qz, kn, va)
    ctx = pl.pallas_call(
        functools.partial(_gqa_kernel, N_CTX, ctx_chunks),
        grid=(bsz,),
        in_specs=[
            pl.BlockSpec((1, N_CTX, qw), lambda b: (b, N_LAT // N_CTX, 0)),
            pl.BlockSpec((1, N_CTX, kw), lambda b: (b, 0, 0)),
            pl.BlockSpec((1, N_CTX, KV_HEADS * kw), lambda b: (b, 0, 0)),
        ],
        out_specs=pl.BlockSpec((1, N_CTX, qw), lambda b: (b, 0, 0)),
        out_shape=jax.ShapeDtypeStruct((bsz, N_CTX, qw), BF16),
        compiler_params=_cparams(("parallel",)),
        name="gqa_attention_ctx",
    )(qz, kn, va)
    return lat, ctx


NA_ROWS = ROW_TILE // GRID_W
NA_KTILES = 3
NA_KW = NA_KTILES * ROW_TILE


def _na_bias_tables(rpb):
    rows = N_LAT // GRID_W
    cq = np.arange(GRID_W)
    kc = np.arange(GRID_W)
    start_c = np.clip(cq - NA_WIN_C // 2, 0, GRID_W - NA_WIN_C)
    ok_c = (kc[None, :] >= start_c[:, None]) & (kc[None, :] < start_c[:, None] + NA_WIN_C)
    dcol = np.clip(kc[None, :] - cq[:, None] + NA_WIN_C - 1, 0, 2 * NA_WIN_C - 2)
    pick_c = (dcol[:, :, None] == np.arange(2 * NA_WIN_C - 1)).astype(np.float32) * LOG2E
    pick_r, ok_r = [], []
    for jb in (0, 1, LAT_TILES - 1):
        base = min(max(jb - 1, 0), LAT_TILES - NA_KTILES)
        r = jb * NA_ROWS + np.arange(NA_ROWS)
        kr = base * NA_ROWS + np.arange(NA_KTILES * NA_ROWS)
        start_r = np.clip(r - NA_WIN_R // 2, 0, rows - NA_WIN_R)
        ok_r.append((kr[None, :] >= start_r[:, None]) & (kr[None, :] < start_r[:, None] + NA_WIN_R))
        drow = np.clip(kr[None, :] - r[:, None] + NA_WIN_R - 1, 0, 2 * NA_WIN_R - 2)
        pick_r.append((drow[:, :, None] == np.arange(2 * NA_WIN_R - 1)).astype(np.float32))
    pick_r, ok_r = np.stack(pick_r), np.stack(ok_r)
    rows_sel = jnp.einsum('vamr,hrd->vhamd', pick_r, rpb, precision=HIGHEST)
    bias = jnp.einsum('vhamd,ckd->vhacmk', rows_sel, pick_c, precision=HIGHEST)
    ok = ok_r[:, None, :, None, :, None] & ok_c[None, None, None, :, None, :]
    bias = bias + np.where(ok, 0.0, NEG_BIG).astype(np.float32)
    return bias.reshape(3, N_HEADS, ROW_TILE, NA_KW).astype(BF16)


def _na_kernel(q_ref, k0_ref, k1_ref, k2_ref, kc_ref, v0_ref, v1_ref, v2_ref, vc_ref, bias_ref, o_ref):
    lane = lax.broadcasted_iota(jnp.int32, (ROW_TILE, BRANCH_W), 1)
    blk = _div(lane, HEAD_DIM)
    q = q_ref[0].astype(F32) * (HEAD_DIM ** -0.5 * LOG2E)

    def attend(key_refs, val_refs, bias_of):
        def head(h):
            qz = jnp.where(blk == h, q, 0.0).astype(BF16)
            s = [_dot_nt(qz, kr[0]) for kr in key_refs]
            yield
            s = [si if bias_of(h, i) is None else si + bias_of(h, i) for i, si in enumerate(s)]
            yield
            m = functools.reduce(jnp.maximum, [jnp.max(si, axis=-1, keepdims=True) for si in s])
            yield
            p = [jnp.exp2(si - m) for si in s]
            yield
            l = functools.reduce(jnp.add, [jnp.sum(pi, axis=-1, keepdims=True) for pi in p])
            yield
            o = functools.reduce(jnp.add, [_dot(pi.astype(BF16), vr[0]) for pi, vr in zip(p, val_refs)])
            yield o / l

        outs = _run_staggered([head(h) for h in range(N_HEADS)])
        acc = jnp.zeros((ROW_TILE, BRANCH_W), F32)
        for h in range(N_HEADS):
            acc = jnp.where(blk == h, outs[h], acc)
        o_ref[0] = acc.astype(BF16)

    @pl.when(pl.program_id(1) == 0)
    def _():
        attend([kc_ref], [vc_ref], lambda h, i: None)

    @pl.when(pl.program_id(1) > 0)
    def _():
        def bias_of(h, i):
            if i == NA_KTILES:
                return None
            return bias_ref[0, h, :, i * ROW_TILE:(i + 1) * ROW_TILE].astype(F32)
        attend([k0_ref, k1_ref, k2_ref, kc_ref], [v0_ref, v1_ref, v2_ref, vc_ref], bias_of)


def _na_call(p3, bias_tabs):
    bsz = p3.shape[0]
    cq = COL_CQ // BRANCH_W

    def key_tile(j, m):
        return 1 + jnp.clip(j - 2, 0, LAT_TILES - NA_KTILES) + m

    def spec_k(col, m):
        return pl.BlockSpec((1, ROW_TILE, BRANCH_W), lambda b, j: (b, key_tile(j, m), col))

    def spec_c(col):
        return pl.BlockSpec((1, ROW_TILE, BRANCH_W), lambda b, j: (b, 0, col))

    def variant(j):
        return jnp.where(j <= 1, 0, jnp.where(j == LAT_TILES, 2, 1))

    return pl.pallas_call(
        _na_kernel,
        grid=(bsz, N_TILES),
        in_specs=[pl.BlockSpec((1, ROW_TILE, BRANCH_W), lambda b, j: (b, j, cq)),
                  spec_k(cq + 1, 0), spec_k(cq + 1, 1), spec_k(cq + 1, 2), spec_c(cq + 1),
                  spec_k(cq + 2, 0), spec_k(cq + 2, 1), spec_k(cq + 2, 2), spec_c(cq + 2),
                  pl.BlockSpec((1, N_HEADS, ROW_TILE, NA_KW), lambda b, j: (variant(j), 0, 0, 0))],
        out_specs=pl.BlockSpec((1, ROW_TILE, BRANCH_W), lambda b, j: (b, j, 0)),
        out_shape=jax.ShapeDtypeStruct((bsz, SEQ_ALL, BRANCH_W), BF16),
        compiler_params=_cparams(("parallel", "parallel")),
        name="na_attention",
    )(p3, p3, p3, p3, p3, p3, p3, p3, p3, bias_tabs)


def _merge_kernel(split, hf_ref, hb_ref, ag_ref, sf_ref, sb_ref, xs_ref, z_ref, yc_ref, yd_ref,
                  g0_ref, g1_ref, g2_ref, g3_ref, *rest):
    if split:
        x_ref, xc_ref, *rest = rest
    else:
        x_ref, *rest = rest
    gate_ref, hn_w, dsk, sn_w, seg_ref, wa, wb, wc, wd, wo, ln_g, ln_b, o_ref = rest
    if split:
        x_res = jnp.where(pl.program_id(1) == 0, xc_ref[0], x_ref[0])
    else:
        x_res = x_ref[0]
    oa = hf_ref[0] + hb_ref[0]
    ag = ag_ref[0].astype(F32)
    oa = oa * lax.rsqrt(_segment_mean(oa * oa, seg_ref[...]) + LN_EPS) * hn_w[...]
    br_a = oa * (ag * _sigmoid(ag))
    z = z_ref[0].astype(F32)
    yb = (sf_ref[0] + sb_ref[0] + xs_ref[0].astype(F32) * dsk[...]) * (z * _sigmoid(z))
    br_b = yb * lax.rsqrt(jnp.mean(yb * yb, axis=-1, keepdims=True) + LN_EPS) * sn_w[...]
    merged = (1.0 + jnp.tanh(g0_ref[0].astype(F32))) * _dot(br_a.astype(BF16), wa[...])
    merged += (1.0 + jnp.tanh(g1_ref[0].astype(F32))) * _dot(br_b.astype(BF16), wb[...])
    merged += (1.0 + jnp.tanh(g2_ref[0].astype(F32))) * _dot(yc_ref[0], wc[...])
    merged += (1.0 + jnp.tanh(g3_ref[0].astype(F32))) * _dot(yd_ref[0], wd[...])
    y = _dot(merged.astype(BF16), wo[...])
    o_ref[0] = _layer_norm(ALPHA * x_res + gate_ref[0] * y) * ln_g[...] + ln_b[...]


def _merge_call(layer, mods, x_src, p3, hf, hb, sf, sb, xact, yc, yd, hn_w, dsk, sn_w, wa, wb, wc, wd, wo,
                ln_g, ln_b, with_ctx):
    split = isinstance(x_src, tuple)
    assert with_ctx or not split
    bsz = p3.shape[0]
    t0 = 0 if with_ctx else 1
    tiles = N_TILES - t0
    t256 = lambda col: pl.BlockSpec((1, ROW_TILE, BRANCH_W), lambda b, j: (b, j + t0, col))
    t512 = pl.BlockSpec((1, ROW_TILE, 512), lambda b, j: (b, j, 0))
    tgate = lambda n: pl.BlockSpec((1, ROW_TILE, D_MODEL), lambda b, j: (b, j + t0, COL_GATE // D_MODEL + n))
    tx = pl.BlockSpec((1, ROW_TILE, D_MODEL), lambda b, j: (b, j + t0, 0))
    gate = pl.BlockSpec((1, 1, D_MODEL), lambda b, j: _mod_index(layer, 2)(b, j + t0))
    full = lambda a: pl.BlockSpec(a.shape, lambda b, j: (0,) * a.ndim)
    seg = _seg_matrix(BRANCH_W)
    consts = [hn_w, dsk, sn_w, seg, wa, wb, wc, wd, wo, ln_g, ln_b]
    if split:
        x_specs = [pl.BlockSpec((1, ROW_TILE, D_MODEL), lambda b, j: (b, jnp.maximum(j - 1, 0), 0)),
                   pl.BlockSpec((1, ROW_TILE, D_MODEL), lambda b, j: (b, 0, 0))]
        x_args = list(x_src)
    else:
        x_specs, x_args = [tx], [x_src]
    return pl.pallas_call(
        functools.partial(_merge_kernel, split),
        grid=(bsz, tiles),
        in_specs=[t256(0), t256(0), t256(COL_AQ // BRANCH_W + 4), t256(0), t256(0), t256(0),
                  t256(COL_Z // BRANCH_W), t256(0), t512,
                  tgate(0), tgate(1), tgate(2), tgate(3)] + x_specs + [gate]
                 + [full(a) for a in consts],
        out_specs=pl.BlockSpec((1, ROW_TILE, D_MODEL), lambda b, j: (b, j, 0)),
        out_shape=jax.ShapeDtypeStruct((bsz, tiles * ROW_TILE, D_MODEL), F32),
        compiler_params=_cparams(("parallel", "parallel")),
        name="merge_postnorm",
    )(hf, hb, p3, sf, sb, xact, p3, yc, yd, p3, p3, p3, p3, *x_args, mods, *consts)


FFN_CHUNK = 256


FFN_SUB = 2


def _ffn_kernel(x_ref, *refs):
    mod_refs, (wu_ref, wd_ref, ln_g, ln_b, o_ref) = refs[:3 * FFN_SUB], refs[3 * FFN_SUB:]
    tm = FFN_SUB * ROW_TILE
    x = x_ref[...]
    ln = _layer_norm(x)
    h = jnp.concatenate(
        [ln[u * ROW_TILE:(u + 1) * ROW_TILE] * (1.0 + mod_refs[3 * u + 1][0]) + mod_refs[3 * u][0]
         for u in range(FFN_SUB)], axis=0).astype(BF16)
    acc = jnp.zeros((tm, D_MODEL), F32)
    for c in range(FFN_HIDDEN // FFN_CHUNK):
        c0 = c * FFN_CHUNK
        g = _dot(h, wu_ref[:, c0:c0 + FFN_CHUNK])
        u = _dot(h, wu_ref[:, FFN_HIDDEN + c0:FFN_HIDDEN + c0 + FFN_CHUNK])
        a = (g * _sigmoid(g) * u).astype(BF16)
        acc = acc + _dot(a, wd_ref[c0:c0 + FFN_CHUNK, :])
    for u in range(FFN_SUB):
        rows = slice(u * ROW_TILE, (u + 1) * ROW_TILE)
        y = ALPHA * x[rows] + mod_refs[3 * u + 2][0] * acc[rows]
        o_ref[rows, :] = _layer_norm(y) * ln_g[...] + ln_b[...]


def _ffn_call(layer, mods, x1_flat, w_up, w_down, ln_g, ln_b, with_ctx):
    m = x1_flat.shape[0]
    tm = FFN_SUB * ROW_TILE
    seq_tiles, first = (N_TILES, 0) if with_ctx else (LAT_TILES, 1)
    mod_specs = [_flat_mod_spec(layer, 3 + which, FFN_SUB, u, seq_tiles, first)
                 for u in range(FFN_SUB) for which in range(3)]
    tx = pl.BlockSpec((tm, D_MODEL), lambda i: (i, 0))
    resident = lambda a: pl.BlockSpec(a.shape, lambda i: (0,) * a.ndim, pipeline_mode=pl.Buffered(1))
    full = lambda a: pl.BlockSpec(a.shape, lambda i: (0,) * a.ndim)
    return pl.pallas_call(
        _ffn_kernel,
        grid=(m // tm,),
        in_specs=[tx] + mod_specs + [resident(w_up), resident(w_down), full(ln_g), full(ln_b)],
        out_specs=tx,
        out_shape=jax.ShapeDtypeStruct(x1_flat.shape, F32),
        compiler_params=_cparams(("parallel",)),
        name="ffn_postnorm",
    )(x1_flat, *([mods] * (3 * FFN_SUB)), w_up, w_down, ln_g, ln_b)


def _pack_w_in(w):
    n_dt = 2 * N_HEADS
    o_dt = 5 * BRANCH_W + BRANCH_W + SSD_CONV_CH
    o_c = o_dt + n_dt
    o_dq = o_c + 3 * BRANCH_W
    o_dk = o_dq + BRANCH_W
    o_gate = o_dk + 2 * KV_HEADS * HEAD_DIM
    zeros = jnp.zeros((w.shape[0], COL_GATE - COL_DT - n_dt), w.dtype)
    cols = [w[:, :o_dt], w[:, o_c:o_dq], _q_pad_lanes(w[:, o_dq:o_dk]), w[:, o_dk:o_gate],
            w[:, o_dt:o_c], zeros, w[:, o_gate:]]
    packed = jnp.concatenate(cols, axis=1)
    assert packed.shape[1] == P_COLS
    return packed.astype(BF16)


def _pad_rows_d(wd):
    return _q_pad_lanes(wd.T).T


def kernel(x, c, ctx, c_ctx, ada_w, ada_b, w_in, hgrn_lb, hgrn_norm, ssd_conv_w, ssd_conv_b, ssd_dt_bias,
           ssd_a_log, ssd_d, ssd_norm, na_rpb, q_norm, k_norm, w_branch, w_out, ln1_g, ln1_b,
           ffn_w_up, ffn_w_down, ln2_g, ln2_b):
    bsz = x.shape[0]
    c8 = jnp.concatenate([c, c_ctx[None, :], jnp.zeros((8 - bsz - 1, D_MODEL), F32)], axis=0)
    mods = _ada_call(c8, ada_w, ada_b).reshape(DEPTH * 8 * 6, 1, D_MODEL)
    xa = (x, ctx)
    row = lambda v: v.reshape(1, -1)
    for l in range(DEPTH):
        last = l == DEPTH - 1
        p = _inproj_call(xa, mods, _pack_w_in(w_in[l]), l)
        p3 = p.reshape(bsz, SEQ_ALL, P_COLS)
        hf, hb = _hgrn_call(p3, hgrn_lb[0], hgrn_lb[1], l)
        xact = _conv_call(p3, ssd_conv_w[l], ssd_conv_b[l])
        dt_rows = jnp.swapaxes(p3[:, :, COL_DT:COL_DT + 8], 1, 2).astype(F32)
        sf, sb = _ssd_call(xact, p3, dt_rows, ssd_dt_bias[l], ssd_a_log[l])
        yc = _na_call(p3, _na_bias_tables(na_rpb[l]))
        qz, kn, va = _qkprep_call(p3, q_norm[l], k_norm[l])
        yd, yd_ctx = _gqa_call(qz, kn, va)
        if not last:
            yd = jnp.concatenate([yd_ctx, yd], axis=1)
        wb = w_branch[l].astype(BF16)
        x1 = _merge_call(l, mods, xa, p3, hf, hb, sf, sb, xact, yc, yd,
                         row(hgrn_norm[l]), row(jnp.repeat(ssd_d[l], HEAD_DIM)), row(ssd_norm[l]),
                         wb[0], wb[1], wb[2], _pad_rows_d(wb[3]), (0.5 * w_out[l]).astype(BF16),
                         row(ln1_g[l]), row(ln1_b[l]), with_ctx=not last)
        xa = _ffn_call(l, mods, x1.reshape(-1, D_MODEL), ffn_w_up[l].astype(BF16), ffn_w_down[l].astype(BF16),
                       row(ln2_g[l]), row(ln2_b[l]), with_ctx=not last).reshape(bsz, -1, D_MODEL)
    return xa
```

```python
import functools
import math

import jax
import jax.numpy as jnp
import numpy as np
from jax import lax
from jax.experimental import pallas as pl
from jax.experimental.pallas import tpu as pltpu

F32 = jnp.float32
BF16 = jnp.bfloat16
HIGHEST = lax.Precision.HIGHEST

D_MODEL = 1024
DEPTH = 2
GRID_W = 64
N_CTX = 256
N_LAT = 4096
SEQ_ALL = N_CTX + N_LAT
HEAD_DIM = 64
BRANCH_W = 256
N_HEADS = 4
SSD_STATE = 128
SSD_GROUPS = 2
SSD_CONV_K = 5
SSD_CONV_CH = BRANCH_W + 2 * SSD_GROUPS * SSD_STATE
KV_HEADS = 2
NA_WIN_R = 8
NA_WIN_C = 16
ROPE_BASE = 10000.0
LN_EPS = 1e-6
FFN_HIDDEN = 2816
ALPHA = (2.0 * DEPTH) ** 0.25

ROW_TILE = 256
N_TILES = SEQ_ALL // ROW_TILE
LAT_TILES = N_LAT // ROW_TILE
HGRN_CHUNK = 64
HGRN_SUB = 16
SSD_CHUNK = 128
NEG_BIG = -1e30
VMEM_LIMIT = 48 * 1024 * 1024

P_COLS = 8192
COL_AQ = 0
COL_Z = 1280
COL_XBC = 1536
COL_CQ = 2304
COL_DQ = 3072
COL_DK = 3584
COL_DV = 3712
COL_DT = 3840
COL_GATE = 4096


def _cparams(sem):
    return pltpu.CompilerParams(dimension_semantics=sem, vmem_limit_bytes=VMEM_LIMIT)


def _sigmoid(x):
    return 0.5 * jnp.tanh(0.5 * x) + 0.5


def _div(x, n):
    return lax.shift_right_logical(x, int(math.log2(n)))


def _mod(x, n):
    return x & (n - 1)


def _softplus(x):
    return jnp.maximum(x, 0.0) + jnp.log(1.0 + jnp.exp(-jnp.abs(x)))


def _layer_norm(x):
    mu = jnp.mean(x, axis=-1, keepdims=True)
    xc = x - mu
    var = jnp.mean(xc * xc, axis=-1, keepdims=True)
    return xc * lax.rsqrt(var + LN_EPS)


def _dot(a, b):
    return jnp.dot(a, b, preferred_element_type=F32)


def _dot_nt(a, b):
    return lax.dot_general(a, b, (((1,), (1,)), ((), ())), preferred_element_type=F32)


def _dot_hi(a, b):
    return jnp.dot(a, b, precision=HIGHEST, preferred_element_type=F32)


def _run_staggered(stages, lag=1):
    results = [None] * len(stages)
    done = [False] * len(stages)
    rnd = 0
    while not all(done):
        for g, gen in enumerate(stages):
            if rnd >= g * lag and not done[g]:
                try:
                    val = next(gen)
                    if val is not None:
                        results[g] = val
                except StopIteration:
                    done[g] = True
        rnd += 1
    return results


def _segment_mean(x, seg):
    hi = x.astype(BF16)
    lo = (x - hi.astype(F32)).astype(BF16)
    return _dot(hi, seg) + _dot(lo, seg)


def _ada_kernel(c_ref, w_ref, b_ref, o_ref):
    cv = c_ref[...]
    a = cv * _sigmoid(cv)
    o_ref[0] = _dot_hi(a, w_ref[0]) + b_ref[0]


def _ada_call(c8, ada_w, ada_b):
    n_l = ada_w.shape[0]
    return pl.pallas_call(
        _ada_kernel,
        grid=(n_l, 6),
        in_specs=[
            pl.BlockSpec((8, D_MODEL), lambda l, j: (0, 0)),
            pl.BlockSpec((1, D_MODEL, D_MODEL), lambda l, j: (l, 0, j)),
            pl.BlockSpec((1, 1, D_MODEL), lambda l, j: (l, 0, j)),
        ],
        out_specs=pl.BlockSpec((1, 8, D_MODEL), lambda l, j: (l, 0, j)),
        out_shape=jax.ShapeDtypeStruct((n_l, 8, 6 * D_MODEL), F32),
        compiler_params=_cparams(("arbitrary", "arbitrary")),
        name="ada_mod",
    )(c8, ada_w, ada_b.reshape(n_l, 1, 6 * D_MODEL))


def _mod_index(layer, which):
    def index(b, j):
        return (layer * 48 + jnp.where(j == 0, 4, b) * 6 + which, 0, 0)
    return index


def _flat_mod_spec(layer, which, sub_tiles, u, seq_tiles, first_tile):
    def index(i, *_):
        t = i * sub_tiles + u
        b = t // seq_tiles
        j = t % seq_tiles + first_tile
        return (layer * 48 + jnp.where(j == 0, 4, b) * 6 + which, 0, 0)
    return pl.BlockSpec((1, 1, D_MODEL), index)


PROJ_SUB = 2
PROJ_TN = 1024


def _split_row_specs(sub_tiles, u):
    def tile_of(i):
        t = i * sub_tiles + u
        return t // N_TILES, t % N_TILES
    lat = pl.BlockSpec((1, ROW_TILE, D_MODEL), lambda i: (tile_of(i)[0], jnp.maximum(tile_of(i)[1] - 1, 0), 0))
    ctx = pl.BlockSpec((1, ROW_TILE, D_MODEL), lambda i: (tile_of(i)[0], 0, 0))
    return [lat, ctx]


def _inproj_kernel(split, *refs):
    n_x = 2 * PROJ_SUB if split else 1
    x_refs, mod_refs, (w_ref, o_ref) = refs[:n_x], refs[n_x:n_x + 2 * PROJ_SUB], refs[n_x + 2 * PROJ_SUB:]
    parts = []
    for u in range(PROJ_SUB):
        if split:
            j = (pl.program_id(0) * PROJ_SUB + u) % N_TILES
            xt = jnp.where(j == 0, x_refs[2 * u + 1][0], x_refs[2 * u][0])
        else:
            xt = x_refs[0][u * ROW_TILE:(u + 1) * ROW_TILE, :]
        sh, sc = mod_refs[2 * u][0], mod_refs[2 * u + 1][0]
        parts.append((_layer_norm(xt) * (1.0 + sc) + sh).astype(BF16))
    h = jnp.concatenate(parts, axis=0)
    for n in range(P_COLS // PROJ_TN):
        cols = slice(n * PROJ_TN, (n + 1) * PROJ_TN)
        acc = _dot(h, w_ref[:, cols])
        if n * PROJ_TN >= COL_GATE:
            acc = 0.5 * acc
        o_ref[:, cols] = acc.astype(BF16)


def _inproj_call(x_src, mods, wp, layer):
    split = isinstance(x_src, tuple)
    bsz = (x_src[0] if split else x_src).shape[0]
    m = bsz * SEQ_ALL
    tm = PROJ_SUB * ROW_TILE
    mod_specs = []
    for u in range(PROJ_SUB):
        mod_specs += [_flat_mod_spec(layer, 0, PROJ_SUB, u, N_TILES, 0),
                      _flat_mod_spec(layer, 1, PROJ_SUB, u, N_TILES, 0)]
    if split:
        x_specs = [s for u in range(PROJ_SUB) for s in _split_row_specs(PROJ_SUB, u)]
        x_args = list(x_src) * PROJ_SUB
    else:
        x_specs = [pl.BlockSpec((tm, D_MODEL), lambda i: (i, 0))]
        x_args = [x_src.reshape(m, D_MODEL)]
    return pl.pallas_call(
        functools.partial(_inproj_kernel, split),
        grid=(m // tm,),
        in_specs=x_specs + mod_specs
                 + [pl.BlockSpec((D_MODEL, P_COLS), lambda i: (0, 0), pipeline_mode=pl.Buffered(1))],
        out_specs=pl.BlockSpec((tm, P_COLS), lambda i: (i, 0)),
        out_shape=jax.ShapeDtypeStruct((m, P_COLS), BF16),
        compiler_params=_cparams(("parallel",)),
        name="in_proj",
    )(*x_args, *([mods] * (2 * PROJ_SUB)), wp)


CONV_PAD = 8


def _conv_kernel(x_ref, w_ref, b_ref, o_ref, xs_ref):
    zeros = jnp.zeros((CONV_PAD, 128), F32)
    xs_ref[0:CONV_PAD, :] = zeros
    xs_ref[CONV_PAD + SEQ_ALL:, :] = zeros
    xs_ref[CONV_PAD:CONV_PAD + SEQ_ALL, :] = x_ref[0].astype(F32)
    w = w_ref[...]
    row = lax.broadcasted_iota(jnp.int32, (ROW_TILE, 128), 0)
    half = SSD_CONV_K // 2
    for ci in range(N_TILES):
        t0 = ci * ROW_TILE
        acc = jnp.broadcast_to(b_ref[...], (ROW_TILE, 128))
        for k in range(SSD_CONV_K):
            j = k - half
            xk = xs_ref[CONV_PAD + t0 + j:CONV_PAD + t0 + j + ROW_TILE, :]
            if ci == 0 and j > 0:
                xk = jnp.where(row + j < ROW_TILE, xk, 0.0)
            if ci == 1 and j < 0:
                xk = jnp.where(row + j >= 0, xk, 0.0)
            acc = acc + xk * w[k:k + 1, :]
        o_ref[0, t0:t0 + ROW_TILE, :] = (acc * _sigmoid(acc)).astype(BF16)


def _conv_call(p3, conv_w, conv_b):
    bsz = p3.shape[0]
    n_ct = SSD_CONV_CH // 128
    return pl.pallas_call(
        _conv_kernel,
        grid=(bsz, n_ct),
        in_specs=[
            pl.BlockSpec((1, SEQ_ALL, 128), lambda b, c: (b, 0, COL_XBC // 128 + c)),
            pl.BlockSpec((SSD_CONV_K, 128), lambda b, c: (0, c)),
            pl.BlockSpec((1, 128), lambda b, c: (0, c)),
        ],
        out_specs=pl.BlockSpec((1, SEQ_ALL, 128), lambda b, c: (b, 0, c)),
        out_shape=jax.ShapeDtypeStruct((bsz, SEQ_ALL, SSD_CONV_CH), BF16),
        scratch_shapes=[pltpu.VMEM((SEQ_ALL + 2 * CONV_PAD, 128), F32)],
        compiler_params=_cparams(("parallel", "parallel")),
        name="ssd_conv",
    )(p3, conv_w, conv_b.reshape(1, SSD_CONV_CH))


def _fwd_tile(j):
    return j


def _bwd_tile(j):
    return jnp.where(j == 0, 0, N_TILES - j)


def _tri(n, upper):
    r = lax.broadcasted_iota(jnp.int32, (n, n), 0)
    c = lax.broadcasted_iota(jnp.int32, (n, n), 1)
    return (c >= r) if upper else (c <= r)


def _ssd_decays(rev, dtc_ref, dtr_ref, bias_r, alog_r, bias_c, alog_c):
    n = SSD_CHUNK
    dt_c = _softplus(dtc_ref[0].astype(F32) + bias_r)
    g_c = dt_c * (-jnp.exp(alog_r) * LOG2E)
    dt_r = _softplus(dtr_ref[0] + bias_c)
    g_r = dt_r * (-jnp.exp(alog_c) * LOG2E)
    cum_c = _cumsum_rows(g_c, rev, n)
    cum_r = [_cumsum_cols(g_r[:, ci * n:(ci + 1) * n], rev) for ci in range(ROW_TILE // n)]
    return dt_c, cum_c, cum_r


def _ssd_stream(rev, g, decays, xa_ref, s_ref, o_ref):
    d = 1 if rev else 0
    n = SSD_CHUNK
    dt_c, cum_c_all, cum_r_all = decays
    keep = _tri(n, rev)
    lane = lax.broadcasted_iota(jnp.int32, (n, 128), 1)
    first = lane < HEAD_DIM
    n_chunks = ROW_TILE // n
    col0 = 4 * d + 2 * g
    end_row = 0 if rev else n - 1
    for ci in (range(n_chunks - 1, -1, -1) if rev else range(n_chunks)):
        c0 = ci * n
        cum_c = cum_c_all[c0:c0 + n, :]
        cum_r = cum_r_all[ci]
        bm = xa_ref[0, c0:c0 + n, BRANCH_W + g * SSD_STATE:BRANCH_W + (g + 1) * SSD_STATE]
        cm = xa_ref[0, c0:c0 + n, BRANCH_W + (SSD_GROUPS + g) * SSD_STATE:
                    BRANCH_W + (SSD_GROUPS + g + 1) * SSD_STATE]
        x2 = xa_ref[0, c0:c0 + n, g * 128:(g + 1) * 128].astype(F32)
        gmat = _dot_nt(cm, bm)
        s_old = s_ref[...]
        qs = _dot(cm, s_old.astype(BF16))
        yield
        bc0 = cum_c[:, col0:col0 + 1]
        bc1 = cum_c[:, col0 + 1:col0 + 2]
        br0 = cum_r[col0:col0 + 1, :]
        br1 = cum_r[col0 + 1:col0 + 2, :]
        p0 = jnp.exp2(jnp.where(keep, bc0 - br0, NEG_BIG)) * gmat
        p1 = jnp.exp2(jnp.where(keep, bc1 - br1, NEG_BIG)) * gmat
        dt2 = jnp.where(first, dt_c[c0:c0 + n, col0:col0 + 1], dt_c[c0:c0 + n, col0 + 1:col0 + 2])
        v2 = x2 * dt2
        v2b = v2.astype(BF16)
        yield
        pv = jnp.where(first, _dot(p0.astype(BF16), v2b), _dot(p1.astype(BF16), v2b))
        bc2 = jnp.where(first, bc0, bc1)
        e0 = cum_c[end_row:end_row + 1, col0:col0 + 1]
        e1 = cum_c[end_row:end_row + 1, col0 + 1:col0 + 2]
        end2 = jnp.where(first[0:1, :], e0, e1)
        wv = (v2 * jnp.exp2(end2 - bc2)).astype(BF16)
        upd = lax.dot_general(bm, wv, (((0,), (0,)), ((), ())), preferred_element_type=F32)
        yield
        o_ref[0, c0:c0 + n, g * 128:(g + 1) * 128] = pv + jnp.exp2(bc2) * qs
        s_ref[...] = s_old * jnp.exp2(end2) + upd
        yield


def _ssd_kernel(xa_f, xa_b, dtc_f, dtc_b, dtr_f, dtr_b, bias_r, alog_r, bias_c, alog_c,
                o_f, o_b, s_ref):
    @pl.when(pl.program_id(1) == 0)
    def _():
        s_ref[...] = jnp.zeros(s_ref.shape, F32)

    br, ar, bc, ac = bias_r[...], alog_r[...], bias_c[...], alog_c[...]
    dec_f = _ssd_decays(False, dtc_f, dtr_f, br, ar, bc, ac)
    dec_b = _ssd_decays(True, dtc_b, dtr_b, br, ar, bc, ac)
    _run_staggered([_ssd_stream(False, 0, dec_f, xa_f, s_ref.at[0, 0], o_f),
                    _ssd_stream(True, 0, dec_b, xa_b, s_ref.at[1, 0], o_b),
                    _ssd_stream(False, 1, dec_f, xa_f, s_ref.at[0, 1], o_f),
                    _ssd_stream(True, 1, dec_b, xa_b, s_ref.at[1, 1], o_b)])


def _ssd_call(xact, p3, dt_rows, dt_bias, a_log):
    bsz = xact.shape[0]
    bias8 = dt_bias.reshape(8).astype(F32)
    alog8 = a_log.reshape(8).astype(F32)
    pad_row = lambda v: jnp.zeros((1, 128), F32).at[0, :8].set(v)
    col = lambda v: jnp.broadcast_to(v[:, None], (8, ROW_TILE))

    def spec_x(tile):
        return pl.BlockSpec((1, ROW_TILE, SSD_CONV_CH), lambda b, j: (b, tile(j), 0))

    def spec_dtc(tile):
        return pl.BlockSpec((1, ROW_TILE, 128), lambda b, j: (b, tile(j), COL_DT // 128))

    def spec_dtr(tile):
        return pl.BlockSpec((1, 8, ROW_TILE), lambda b, j: (b, 0, tile(j)))

    def spec_o(tile):
        return pl.BlockSpec((1, ROW_TILE, BRANCH_W), lambda b, j: (b, tile(j), 0))

    small_r = pl.BlockSpec((1, 128), lambda b, j: (0, 0))
    small_c = pl.BlockSpec((8, ROW_TILE), lambda b, j: (0, 0))
    out = jax.ShapeDtypeStruct((bsz, SEQ_ALL, BRANCH_W), F32)
    return pl.pallas_call(
        _ssd_kernel,
        grid=(bsz, N_TILES),
        in_specs=[spec_x(_fwd_tile), spec_x(_bwd_tile), spec_dtc(_fwd_tile), spec_dtc(_bwd_tile),
                  spec_dtr(_fwd_tile), spec_dtr(_bwd_tile), small_r, small_r, small_c, small_c],
        out_specs=[spec_o(_fwd_tile), spec_o(_bwd_tile)],
        out_shape=[out, out],
        scratch_shapes=[pltpu.VMEM((2, SSD_GROUPS, SSD_STATE, 128), F32)],
        compiler_params=_cparams(("parallel", "arbitrary")),
        name="ssd_scan",
    )(xact, xact, p3, p3, dt_rows, dt_rows, pad_row(bias8), pad_row(alog8), col(bias8), col(alog8))


def _hgrn_lower_bound(lb_ref, layer):
    p = lb_ref[...]
    e = jnp.exp(p - jnp.max(p, axis=0, keepdims=True))
    sm = e / jnp.sum(e, axis=0, keepdims=True)
    acc = sm[0:1, :]
    for i in range(1, layer + 1):
        acc = acc + sm[i:i + 1, :]
    return acc - sm[0:1, :]


LOG2E = 1.4426950408889634


def _cumsum_rows(x, rev, block):
    n = x.shape[0]
    r_i = lax.broadcasted_iota(jnp.int32, (n, n), 0)
    c_i = lax.broadcasted_iota(jnp.int32, (n, n), 1)
    tri = ((c_i >= r_i) if rev else (c_i <= r_i)) & (_div(r_i, block) == _div(c_i, block))
    tri = tri.astype(BF16)
    hi = x.astype(BF16)
    rest = x - hi.astype(F32)
    mid = rest.astype(BF16)
    lo = (rest - mid.astype(F32)).astype(BF16)
    return _dot(tri, hi) + _dot(tri, mid) + _dot(tri, lo)


def _cumsum_cols(x, rev):
    n = x.shape[1]
    r_i = lax.broadcasted_iota(jnp.int32, (n, n), 0)
    c_i = lax.broadcasted_iota(jnp.int32, (n, n), 1)
    tri = ((r_i >= c_i) if rev else (r_i <= c_i)).astype(BF16)
    hi = x.astype(BF16)
    rest = x - hi.astype(F32)
    mid = rest.astype(BF16)
    lo = (rest - mid.astype(F32)).astype(BF16)
    return _dot(hi, tri) + _dot(mid, tri) + _dot(lo, tri)


def _hgrn_decay_logs(f, lb, rev):
    fg = lb + (1.0 - lb) * _sigmoid(f)
    cum2 = _cumsum_rows(jnp.log(fg) * LOG2E, rev, HGRN_CHUNK)
    return cum2, jnp.log(1.0 - fg) * LOG2E - cum2


def _hgrn_chunk(rev, q, v, cum2, lk2, st_ref, out_ref, ones_seg, bd_mask):
    n, c = HGRN_CHUNK, HGRN_SUB
    row = lax.broadcasted_iota(jnp.int32, (n, BRANCH_W), 0)
    lane = lax.broadcasted_iota(jnp.int32, (n, BRANCH_W), 1)
    offset = (_mod(lane, HEAD_DIM) - row) if rev else (row - _mod(lane, HEAD_DIM))
    pos = _mod(row, c)

    es = []
    for dlt in range(c):
        if dlt == 0:
            arg = cum2 + lk2
        else:
            lks = pltpu.roll(lk2, (n - dlt) if rev else dlt, axis=0)
            ok = (pos + dlt < c) if rev else (pos >= dlt)
            arg = jnp.where(ok, cum2 + lks, NEG_BIG)
        es.append((q * jnp.exp2(arg)).astype(BF16))
        if dlt % 4 == 3:
            yield
    seg = _dot(jnp.concatenate(es, axis=0), ones_seg)

    cl_parts = []
    for i in range(n // c):
        blk_rows = cum2[i * c:(i + 1) * c, :]
        if rev:
            cl_parts.append(blk_rows - cum2[(i + 1) * c:(i + 1) * c + 1, :] if i < n // c - 1 else blk_rows)
        else:
            cl_parts.append(blk_rows - cum2[i * c - 1:i * c, :] if i > 0 else blk_rows)
    qe = (q * jnp.exp2(jnp.concatenate(cl_parts, axis=0))).astype(BF16)
    parts = []
    for i in range(n // c):
        if (rev and i == n // c - 1) or (not rev and i == 0):
            parts.append(jnp.zeros((c, BRANCH_W), F32))
            continue
        if rev:
            ref = cum2[(i + 1) * c:(i + 1) * c + 1, :]
            valid = row >= (i + 1) * c
        else:
            ref = cum2[i * c - 1:i * c, :]
            valid = row < i * c
        rhs = jnp.exp2(jnp.where(valid, lk2 + ref, NEG_BIG))
        rhs_bd = jnp.where(bd_mask, jnp.concatenate([rhs] * N_HEADS, axis=0), 0.0).astype(BF16)
        parts.append(_dot_nt(qe[i * c:(i + 1) * c, :], rhs_bd))
        yield
    v_bd = jnp.where(bd_mask, jnp.concatenate([v] * N_HEADS, axis=0), 0.0).astype(BF16)
    st = st_ref[...]
    o_prev = _dot_nt((q * jnp.exp2(cum2)).astype(BF16), st.astype(BF16))
    end = cum2[0:1, :] if rev else cum2[n - 1:n, :]
    kp = jnp.exp2(lk2 + end)
    upd = _dot(v.T.astype(BF16), kp.astype(BF16))
    st_ref[...] = st * jnp.exp2(end) + jnp.where(bd_mask, upd, 0.0)
    yield
    sc = jnp.zeros((n, BRANCH_W), F32)
    for dlt in range(c):
        sc = jnp.where(offset == dlt, seg[dlt * n:(dlt + 1) * n, :], sc)
        if dlt % 8 == 7:
            yield
    sc = sc + jnp.concatenate(parts, axis=0)
    out_ref[...] = _dot(sc.astype(BF16), v_bd) + o_prev
    yield


def _hgrn_kernel(layer, q_f, f_f, v_f, q_b, f_b, v_b, lbf_ref, lbb_ref, o_f, o_b,
                 st_f, st_b, cum_f, cum_b, lk_f, lk_b):
    @pl.when(pl.program_id(1) == 0)
    def _():
        st_f[...] = jnp.zeros(st_f.shape, F32)
        st_b[...] = jnp.zeros(st_b.shape, F32)

    rb = _div(lax.broadcasted_iota(jnp.int32, (BRANCH_W, BRANCH_W), 0), HEAD_DIM)
    cb = _div(lax.broadcasted_iota(jnp.int32, (BRANCH_W, BRANCH_W), 1), HEAD_DIM)
    bd_mask = rb == cb
    ones_seg = bd_mask.astype(BF16)
    n_chunks = ROW_TILE // HGRN_CHUNK

    def stream(rev, q_ref, f_ref, v_ref, lb_ref, cum_ref, lk_ref, st_ref, o_ref):
        cum2, lk2 = _hgrn_decay_logs(f_ref[0].astype(F32), _hgrn_lower_bound(lb_ref, layer), rev)
        cum_ref[...] = cum2
        lk_ref[...] = lk2
        yield
        for i in (range(n_chunks - 1, -1, -1) if rev else range(n_chunks)):
            sl = slice(i * HGRN_CHUNK, (i + 1) * HGRN_CHUNK)
            yield from _hgrn_chunk(rev, q_ref[0, sl, :].astype(F32), v_ref[0, sl, :].astype(F32),
                                   cum_ref[sl, :], lk_ref[sl, :], st_ref, o_ref.at[0, sl, :], ones_seg, bd_mask)

    _run_staggered([stream(False, q_f, f_f, v_f, lbf_ref, cum_f, lk_f, st_f, o_f),
                    stream(True, q_b, f_b, v_b, lbb_ref, cum_b, lk_b, st_b, o_b)], lag=4)


def _hgrn_call(p3, lb_fwd, lb_bwd, layer):
    bsz = p3.shape[0]
    base = COL_AQ // BRANCH_W

    def spec(tile, col):
        return pl.BlockSpec((1, ROW_TILE, BRANCH_W), lambda b, j: (b, tile(j), col))

    lb_spec = pl.BlockSpec((DEPTH, BRANCH_W), lambda b, j: (0, 0))
    out = jax.ShapeDtypeStruct((bsz, SEQ_ALL, BRANCH_W), F32)
    return pl.pallas_call(
        functools.partial(_hgrn_kernel, layer),
        grid=(bsz, N_TILES),
        in_specs=[spec(_fwd_tile, base), spec(_fwd_tile, base + 1), spec(_fwd_tile, base + 3),
                  spec(_bwd_tile, base), spec(_bwd_tile, base + 2), spec(_bwd_tile, base + 3),
                  lb_spec, lb_spec],
        out_specs=[spec(_fwd_tile, 0), spec(_bwd_tile, 0)],
        out_shape=[out, out],
        scratch_shapes=[pltpu.VMEM((BRANCH_W, BRANCH_W), F32)] * 2 + [pltpu.VMEM((ROW_TILE, BRANCH_W), F32)] * 4,
        compiler_params=_cparams(("parallel", "arbitrary")),
        name="hgrn_scan",
    )(p3, p3, p3, p3, p3, p3, lb_fwd, lb_bwd)


def _rope(xn, cos, sin):
    w = xn.shape[1]
    lane = lax.broadcasted_iota(jnp.int32, xn.shape, 1)
    nxt = pltpu.roll(xn, w - HEAD_DIM // 4, axis=1)
    prv = pltpu.roll(xn, HEAD_DIM // 4, axis=1)
    partner = jnp.where(_mod(lane, HEAD_DIM // 2) < HEAD_DIM // 4, nxt, prv)
    return xn * cos + partner * sin


def _qkprep_kernel(q_ref, k_ref, v_ref, cq_ref, sq_ref, ck_ref, sk_ref, wq_ref, wk_ref, segk_ref,
                   qo_ref, ko_ref, vo_ref):
    def norm_rope(x_ref, w_ref, cos_ref, sin_ref, out_ref, lanes, scale):
        x = x_ref[0, :, lanes].astype(F32)
        sq = x * x
        yield
        ms = _segment_mean(sq, segk_ref[...])
        yield
        xn = x * lax.rsqrt(ms + LN_EPS) * w_ref[:, lanes]
        out_ref[0, :, lanes] = (_rope(xn, cos_ref[:, lanes], sin_ref[:, lanes]) * scale).astype(BF16)
        yield

    slots = [slice(s * 128, (s + 1) * 128) for s in range(N_HEADS)]
    _run_staggered([norm_rope(q_ref, wq_ref, cq_ref, sq_ref, qo_ref, s, HEAD_DIM ** -0.5 * LOG2E) for s in slots]
                   + [norm_rope(k_ref, wk_ref, ck_ref, sk_ref, ko_ref, slice(0, 128), 1.0)])
    v = v_ref[0].astype(F32)
    lane = lax.broadcasted_iota(jnp.int32, v.shape, 1)
    vo_ref[0] = jnp.concatenate([jnp.where(lane < HEAD_DIM, v, 1.0), jnp.where(lane < HEAD_DIM, 1.0, v)],
                                axis=1).astype(BF16)


def _rope_tables():
    t = np.arange(N_LAT)
    nf = HEAD_DIM // 4
    inv_freq = (ROPE_BASE ** (-np.arange(nf, dtype=np.float32) / nf)).astype(np.float32)
    ang_r = ((t // GRID_W).astype(np.float32)[:, None] * inv_freq).astype(np.float64)
    ang_c = ((t % GRID_W).astype(np.float32)[:, None] * inv_freq).astype(np.float64)
    ang = np.concatenate([ang_r, ang_r, ang_c, ang_c], axis=1)
    sign = np.tile(np.concatenate([-np.ones(nf), np.ones(nf)]), 2)
    cos = np.concatenate([np.ones((N_CTX, HEAD_DIM)), np.cos(ang)], axis=0).astype(np.float32)
    sin = np.concatenate([np.zeros((N_CTX, HEAD_DIM)), np.sin(ang) * sign], axis=0).astype(np.float32)
    return cos, sin


def _q_pad_lanes_np(a):
    out = np.zeros(a.shape[:-1] + (N_HEADS * 128,), a.dtype)
    for h in range(N_HEADS):
        lo = h * 128 + (h // 2) * HEAD_DIM
        out[..., lo:lo + HEAD_DIM] = a[..., h * HEAD_DIM:(h + 1) * HEAD_DIM]
    return out


def _q_pad_lanes(a):
    parts = []
    zero = jnp.zeros(a.shape[:-1] + (HEAD_DIM,), a.dtype)
    for h in range(N_HEADS):
        ah = a[..., h * HEAD_DIM:(h + 1) * HEAD_DIM]
        parts += [ah, zero] if h // 2 == 0 else [zero, ah]
    return jnp.concatenate(parts, axis=-1)


def _seg_matrix(width):
    idx = np.arange(width) // HEAD_DIM
    return jnp.asarray((idx[:, None] == idx[None, :]).astype(np.float32) / HEAD_DIM, dtype=BF16)


def _qkprep_call(p3, q_norm, k_norm):
    bsz = p3.shape[0]
    cos, sin = _rope_tables()
    cq = _q_pad_lanes_np(np.tile(cos, (1, N_HEADS)))
    sq = _q_pad_lanes_np(np.tile(sin, (1, N_HEADS)))
    ck = np.tile(cos, (1, KV_HEADS))
    sk = np.tile(sin, (1, KV_HEADS))
    wq = _q_pad_lanes(jnp.tile(q_norm, N_HEADS)[None, :])
    wk = jnp.tile(k_norm, KV_HEADS)[None, :]
    qw, kw = N_HEADS * 128, KV_HEADS * HEAD_DIM
    tile = lambda w, col: pl.BlockSpec((1, ROW_TILE, w), lambda j, b: (b, j, col))
    tab = lambda w: pl.BlockSpec((ROW_TILE, w), lambda j, b: (j, 0))
    full = lambda r, w: pl.BlockSpec((r, w), lambda j, b: (0, 0))
    return pl.pallas_call(
        _qkprep_kernel,
        grid=(N_TILES, bsz),
        in_specs=[tile(qw, COL_DQ // qw), tile(kw, COL_DK // kw), tile(kw, COL_DV // kw),
                  tab(qw), tab(qw), tab(kw), tab(kw),
                  full(1, qw), full(1, kw), full(kw, kw)],
        out_specs=[pl.BlockSpec((1, ROW_TILE, qw), lambda j, b: (b, jnp.where(j == 0, LAT_TILES, j - 1), 0)),
                   tile(kw, 0), tile(KV_HEADS * kw, 0)],
        out_shape=[jax.ShapeDtypeStruct((bsz, SEQ_ALL, qw), BF16),
                   jax.ShapeDtypeStruct((bsz, SEQ_ALL, kw), BF16),
                   jax.ShapeDtypeStruct((bsz, SEQ_ALL, KV_HEADS * kw), BF16)],
        compiler_params=_cparams(("parallel", "parallel")),
        name="gqa_qk_prep",
    )(p3, p3, p3, cq, sq, ck, sk, wq, wk, _seg_matrix(kw))


GQA_KCHUNK = 2048


def _gqa_attend(q2, k_ref, v_ref, g, chunks):
    m = acc = None
    for (k0, kn) in chunks:
        kc = k_ref[0, k0:k0 + kn, :]
        vc = v_ref[0, k0:k0 + kn, g * 128:(g + 1) * 128]
        s = _dot_nt(q2, kc)
        mc = jnp.max(s, axis=-1, keepdims=True)
        if m is None:
            m = mc
            acc = _dot(jnp.exp2(s - m).astype(BF16), vc)
        else:
            mn = jnp.maximum(m, mc)
            acc = jnp.exp2(m - mn) * acc + _dot(jnp.exp2(s - mn).astype(BF16), vc)
            m = mn
    sum_lane = (1 - g) * HEAD_DIM
    return acc / acc[:, sum_lane:sum_lane + 1]


GQA_LAT_ROWS = 2 * ROW_TILE


def _gqa_kernel(rows, chunks, q_ref, k_ref, v_ref, o_ref):
    lane = lax.broadcasted_iota(jnp.int32, (2 * rows, 128), 1)
    for g in range(KV_HEADS):
        q2 = jnp.concatenate([q_ref[0, :, (2 * g) * 128:(2 * g + 1) * 128],
                              q_ref[0, :, (2 * g + 1) * 128:(2 * g + 2) * 128]], axis=0)
        o2 = _gqa_attend(q2, k_ref, v_ref, g, chunks)
        o2 = jnp.where(_div(lane, HEAD_DIM) == g, o2, 0.0).astype(BF16)
        o_ref[0, :, (2 * g) * 128:(2 * g + 1) * 128] = o2[:rows]
        o_ref[0, :, (2 * g + 1) * 128:(2 * g + 2) * 128] = o2[rows:]


def _gqa_call(qz, kn, va):
    bsz = qz.shape[0]
    qw, kw = N_HEADS * 128, KV_HEADS * HEAD_DIM
    ctx_chunks = [(0, N_CTX)]
    all_chunks = ctx_chunks + [(N_CTX + i * GQA_KCHUNK, GQA_KCHUNK) for i in range(N_LAT // GQA_KCHUNK)]
    lat = pl.pallas_call(
        functools.partial(_gqa_kernel, GQA_LAT_ROWS, all_chunks),
        grid=(bsz, N_LAT // GQA_LAT_ROWS),
        in_specs=[
            pl.BlockSpec((1, GQA_LAT_ROWS, qw), lambda b, j: (b, j, 0)),
            pl.BlockSpec((1, SEQ_ALL, kw), lambda b, j: (b, 0, 0)),
            pl.BlockSpec((1, SEQ_ALL, KV_HEADS * kw), lambda b, j: (b, 0, 0)),
        ],
        out_specs=pl.BlockSpec((1, GQA_LAT_ROWS, qw), lambda b, j: (b, j, 0)),
        out_shape=jax.ShapeDtypeStruct((bsz, N_LAT, qw), BF16),
        compiler_params=_cparams(("parallel", "parallel")),
        name="gqa_attention",
    )(qz, kn, va)
    ctx = pl.pallas_call(
        functools.partial(_gqa_kernel, N_CTX, ctx_chunks),
        grid=(bsz,),
        in_specs=[
            pl.BlockSpec((1, N_CTX, qw), lambda b: (b, N_LAT // N_CTX, 0)),
            pl.BlockSpec((1, N_CTX, kw), lambda b: (b, 0, 0)),
            pl.BlockSpec((1, N_CTX, KV_HEADS * kw), lambda b: (b, 0, 0)),
        ],
        out_specs=pl.BlockSpec((1, N_CTX, qw), lambda b: (b, 0, 0)),
        out_shape=jax.ShapeDtypeStruct((bsz, N_CTX, qw), BF16),
        compiler_params=_cparams(("parallel",)),
        name="gqa_attention_ctx",
    )(qz, kn, va)
    return lat, ctx


NA_ROWS = ROW_TILE // GRID_W
NA_KTILES = 3
NA_KW = NA_KTILES * ROW_TILE


def _na_bias_tables(rpb):
    rows = N_LAT // GRID_W
    cq = np.arange(GRID_W)
    kc = np.arange(GRID_W)
    start_c = np.clip(cq - NA_WIN_C // 2, 0, GRID_W - NA_WIN_C)
    ok_c = (kc[None, :] >= start_c[:, None]) & (kc[None, :] < start_c[:, None] + NA_WIN_C)
    dcol = np.clip(kc[None, :] - cq[:, None] + NA_WIN_C - 1, 0, 2 * NA_WIN_C - 2)
    pick_c = jnp.asarray(dcol[:, :, None] == np.arange(2 * NA_WIN_C - 1), dtype=BF16)
    pick_r, ok_r = [], []
    for jb in (0, 1, LAT_TILES - 1):
        base = min(max(jb - 1, 0), LAT_TILES - NA_KTILES)
        r = jb * NA_ROWS + np.arange(NA_ROWS)
        kr = base * NA_ROWS + np.arange(NA_KTILES * NA_ROWS)
        start_r = np.clip(r - NA_WIN_R // 2, 0, rows - NA_WIN_R)
        ok_r.append((kr[None, :] >= start_r[:, None]) & (kr[None, :] < start_r[:, None] + NA_WIN_R))
        drow = np.clip(kr[None, :] - r[:, None] + NA_WIN_R - 1, 0, 2 * NA_WIN_R - 2)
        pick_r.append((drow[:, :, None] == np.arange(2 * NA_WIN_R - 1)).astype(np.float32))
    pick_r, ok_r = np.stack(pick_r), np.stack(ok_r)
    rows_sel = (jnp.einsum('vamr,hrd->vhamd', pick_r, rpb, precision=HIGHEST) * LOG2E).astype(BF16)
    bias = jnp.einsum('vhamd,ckd->vhacmk', rows_sel, pick_c, preferred_element_type=BF16)
    ok = ok_r[:, None, :, None, :, None] & ok_c[None, None, None, :, None, :]
    bias = bias + jnp.asarray(np.where(ok, 0.0, NEG_BIG), dtype=BF16)
    return bias.reshape(3, N_HEADS, ROW_TILE, NA_KW)


def _na_kernel(q_ref, k0_ref, k1_ref, k2_ref, kc_ref, v0_ref, v1_ref, v2_ref, vc_ref, bias_ref, o_ref):
    lane = lax.broadcasted_iota(jnp.int32, (ROW_TILE, BRANCH_W), 1)
    blk = _div(lane, HEAD_DIM)
    q = q_ref[0].astype(F32) * (HEAD_DIM ** -0.5 * LOG2E)

    def attend(key_refs, val_refs, bias_of):
        def head(h):
            qz = jnp.where(blk == h, q, 0.0).astype(BF16)
            s = [_dot_nt(qz, kr[0]) for kr in key_refs]
            yield
            s = [si if bias_of(h, i) is None else si + bias_of(h, i) for i, si in enumerate(s)]
            yield
            m = functools.reduce(jnp.maximum, [jnp.max(si, axis=-1, keepdims=True) for si in s])
            yield
            p = [jnp.exp2(si - m) for si in s]
            yield
            l = functools.reduce(jnp.add, [jnp.sum(pi, axis=-1, keepdims=True) for pi in p])
            yield
            o = functools.reduce(jnp.add, [_dot(pi.astype(BF16), vr[0]) for pi, vr in zip(p, val_refs)])
            yield o / l

        outs = _run_staggered([head(h) for h in range(N_HEADS)])
        acc = jnp.zeros((ROW_TILE, BRANCH_W), F32)
        for h in range(N_HEADS):
            acc = jnp.where(blk == h, outs[h], acc)
        o_ref[0] = acc.astype(BF16)

    @pl.when(pl.program_id(1) == 0)
    def _():
        attend([kc_ref], [vc_ref], lambda h, i: None)

    @pl.when(pl.program_id(1) > 0)
    def _():
        def bias_of(h, i):
            if i == NA_KTILES:
                return None
            return bias_ref[0, h, :, i * ROW_TILE:(i + 1) * ROW_TILE].astype(F32)
        attend([k0_ref, k1_ref, k2_ref, kc_ref], [v0_ref, v1_ref, v2_ref, vc_ref], bias_of)


def _na_call(p3, bias_tabs):
    bsz = p3.shape[0]
    cq = COL_CQ // BRANCH_W

    def key_tile(j, m):
        return 1 + jnp.clip(j - 2, 0, LAT_TILES - NA_KTILES) + m

    def spec_k(col, m):
        return pl.BlockSpec((1, ROW_TILE, BRANCH_W), lambda b, j: (b, key_tile(j, m), col))

    def spec_c(col):
        return pl.BlockSpec((1, ROW_TILE, BRANCH_W), lambda b, j: (b, 0, col))

    def variant(j):
        return jnp.where(j <= 1, 0, jnp.where(j == LAT_TILES, 2, 1))

    return pl.pallas_call(
        _na_kernel,
        grid=(bsz, N_TILES),
        in_specs=[pl.BlockSpec((1, ROW_TILE, BRANCH_W), lambda b, j: (b, j, cq)),
                  spec_k(cq + 1, 0), spec_k(cq + 1, 1), spec_k(cq + 1, 2), spec_c(cq + 1),
                  spec_k(cq + 2, 0), spec_k(cq + 2, 1), spec_k(cq + 2, 2), spec_c(cq + 2),
                  pl.BlockSpec((1, N_HEADS, ROW_TILE, NA_KW), lambda b, j: (variant(j), 0, 0, 0))],
        out_specs=pl.BlockSpec((1, ROW_TILE, BRANCH_W), lambda b, j: (b, j, 0)),
        out_shape=jax.ShapeDtypeStruct((bsz, SEQ_ALL, BRANCH_W), BF16),
        compiler_params=_cparams(("parallel", "parallel")),
        name="na_attention",
    )(p3, p3, p3, p3, p3, p3, p3, p3, p3, bias_tabs)


def _merge_kernel(split, hf_ref, hb_ref, ag_ref, sf_ref, sb_ref, xs_ref, z_ref, yc_ref, yd_ref,
                  g0_ref, g1_ref, g2_ref, g3_ref, *rest):
    if split:
        x_ref, xc_ref, *rest = rest
    else:
        x_ref, *rest = rest
    gate_ref, hn_w, dsk, sn_w, seg_ref, wa, wb, wc, wd, wo, ln_g, ln_b, o_ref = rest
    if split:
        x_res = jnp.where(pl.program_id(1) == 0, xc_ref[0], x_ref[0])
    else:
        x_res = x_ref[0]
    oa = hf_ref[0] + hb_ref[0]
    ag = ag_ref[0].astype(F32)
    oa = oa * lax.rsqrt(_segment_mean(oa * oa, seg_ref[...]) + LN_EPS) * hn_w[...]
    br_a = oa * (ag * _sigmoid(ag))
    z = z_ref[0].astype(F32)
    yb = (sf_ref[0] + sb_ref[0] + xs_ref[0].astype(F32) * dsk[...]) * (z * _sigmoid(z))
    br_b = yb * lax.rsqrt(jnp.mean(yb * yb, axis=-1, keepdims=True) + LN_EPS) * sn_w[...]
    merged = (1.0 + jnp.tanh(g0_ref[0].astype(F32))) * _dot(br_a.astype(BF16), wa[...])
    merged += (1.0 + jnp.tanh(g1_ref[0].astype(F32))) * _dot(br_b.astype(BF16), wb[...])
    merged += (1.0 + jnp.tanh(g2_ref[0].astype(F32))) * _dot(yc_ref[0], wc[...])
    merged += (1.0 + jnp.tanh(g3_ref[0].astype(F32))) * _dot(yd_ref[0], wd[...])
    y = _dot(merged.astype(BF16), wo[...])
    o_ref[0] = _layer_norm(ALPHA * x_res + gate_ref[0] * y) * ln_g[...] + ln_b[...]


def _merge_call(layer, mods, x_src, p3, hf, hb, sf, sb, xact, yc, yd, hn_w, dsk, sn_w, wa, wb, wc, wd, wo,
                ln_g, ln_b, with_ctx):
    split = isinstance(x_src, tuple)
    assert with_ctx or not split
    bsz = p3.shape[0]
    t0 = 0 if with_ctx else 1
    tiles = N_TILES - t0
    t256 = lambda col: pl.BlockSpec((1, ROW_TILE, BRANCH_W), lambda b, j: (b, j + t0, col))
    t512 = pl.BlockSpec((1, ROW_TILE, 512), lambda b, j: (b, j, 0))
    tgate = lambda n: pl.BlockSpec((1, ROW_TILE, D_MODEL), lambda b, j: (b, j + t0, COL_GATE // D_MODEL + n))
    tx = pl.BlockSpec((1, ROW_TILE, D_MODEL), lambda b, j: (b, j + t0, 0))
    gate = pl.BlockSpec((1, 1, D_MODEL), lambda b, j: _mod_index(layer, 2)(b, j + t0))
    full = lambda a: pl.BlockSpec(a.shape, lambda b, j: (0,) * a.ndim)
    seg = _seg_matrix(BRANCH_W)
    consts = [hn_w, dsk, sn_w, seg, wa, wb, wc, wd, wo, ln_g, ln_b]
    if split:
        x_specs = [pl.BlockSpec((1, ROW_TILE, D_MODEL), lambda b, j: (b, jnp.maximum(j - 1, 0), 0)),
                   pl.BlockSpec((1, ROW_TILE, D_MODEL), lambda b, j: (b, 0, 0))]
        x_args = list(x_src)
    else:
        x_specs, x_args = [tx], [x_src]
    return pl.pallas_call(
        functools.partial(_merge_kernel, split),
        grid=(bsz, tiles),
        in_specs=[t256(0), t256(0), t256(COL_AQ // BRANCH_W + 4), t256(0), t256(0), t256(0),
                  t256(COL_Z // BRANCH_W), t256(0), t512,
                  tgate(0), tgate(1), tgate(2), tgate(3)] + x_specs + [gate]
                 + [full(a) for a in consts],
        out_specs=pl.BlockSpec((1, ROW_TILE, D_MODEL), lambda b, j: (b, j, 0)),
        out_shape=jax.ShapeDtypeStruct((bsz, tiles * ROW_TILE, D_MODEL), F32),
        compiler_params=_cparams(("parallel", "parallel")),
        name="merge_postnorm",
    )(hf, hb, p3, sf, sb, xact, p3, yc, yd, p3, p3, p3, p3, *x_args, mods, *consts)


FFN_CHUNK = 256


FFN_SUB = 2


def _ffn_kernel(x_ref, *refs):
    mod_refs, (wu_ref, wd_ref, ln_g, ln_b, o_ref) = refs[:3 * FFN_SUB], refs[3 * FFN_SUB:]
    tm = FFN_SUB * ROW_TILE
    x = x_ref[...]
    ln = _layer_norm(x)
    h = jnp.concatenate(
        [ln[u * ROW_TILE:(u + 1) * ROW_TILE] * (1.0 + mod_refs[3 * u + 1][0]) + mod_refs[3 * u][0]
         for u in range(FFN_SUB)], axis=0).astype(BF16)
    acc = jnp.zeros((tm, D_MODEL), F32)
    for c in range(FFN_HIDDEN // FFN_CHUNK):
        c0 = c * FFN_CHUNK
        g = _dot(h, wu_ref[:, c0:c0 + FFN_CHUNK])
        u = _dot(h, wu_ref[:, FFN_HIDDEN + c0:FFN_HIDDEN + c0 + FFN_CHUNK])
        a = (g * _sigmoid(g) * u).astype(BF16)
        acc = acc + _dot(a, wd_ref[c0:c0 + FFN_CHUNK, :])
    for u in range(FFN_SUB):
        rows = slice(u * ROW_TILE, (u + 1) * ROW_TILE)
        y = ALPHA * x[rows] + mod_refs[3 * u + 2][0] * acc[rows]
        o_ref[rows, :] = _layer_norm(y) * ln_g[...] + ln_b[...]


def _ffn_call(layer, mods, x1_flat, w_up, w_down, ln_g, ln_b, with_ctx):
    m = x1_flat.shape[0]
    tm = FFN_SUB * ROW_TILE
    seq_tiles, first = (N_TILES, 0) if with_ctx else (LAT_TILES, 1)
    mod_specs = [_flat_mod_spec(layer, 3 + which, FFN_SUB, u, seq_tiles, first)
                 for u in range(FFN_SUB) for which in range(3)]
    tx = pl.BlockSpec((tm, D_MODEL), lambda i: (i, 0))
    resident = lambda a: pl.BlockSpec(a.shape, lambda i: (0,) * a.ndim, pipeline_mode=pl.Buffered(1))
    full = lambda a: pl.BlockSpec(a.shape, lambda i: (0,) * a.ndim)
    return pl.pallas_call(
        _ffn_kernel,
        grid=(m // tm,),
        in_specs=[tx] + mod_specs + [resident(w_up), resident(w_down), full(ln_g), full(ln_b)],
        out_specs=tx,
        out_shape=jax.ShapeDtypeStruct(x1_flat.shape, F32),
        compiler_params=_cparams(("parallel",)),
        name="ffn_postnorm",
    )(x1_flat, *([mods] * (3 * FFN_SUB)), w_up, w_down, ln_g, ln_b)


def _pack_w_in(w):
    n_dt = 2 * N_HEADS
    o_dt = 5 * BRANCH_W + BRANCH_W + SSD_CONV_CH
    o_c = o_dt + n_dt
    o_dq = o_c + 3 * BRANCH_W
    o_dk = o_dq + BRANCH_W
    o_gate = o_dk + 2 * KV_HEADS * HEAD_DIM
    zeros = jnp.zeros((w.shape[0], COL_GATE - COL_DT - n_dt), w.dtype)
    cols = [w[:, :o_dt], w[:, o_c:o_dq], _q_pad_lanes(w[:, o_dq:o_dk]), w[:, o_dk:o_gate],
            w[:, o_dt:o_c], zeros, w[:, o_gate:]]
    packed = jnp.concatenate(cols, axis=1)
    assert packed.shape[1] == P_COLS
    return packed.astype(BF16)


def _pad_rows_d(wd):
    return _q_pad_lanes(wd.T).T


def kernel(x, c, ctx, c_ctx, ada_w, ada_b, w_in, hgrn_lb, hgrn_norm, ssd_conv_w, ssd_conv_b, ssd_dt_bias,
           ssd_a_log, ssd_d, ssd_norm, na_rpb, q_norm, k_norm, w_branch, w_out, ln1_g, ln1_b,
           ffn_w_up, ffn_w_down, ln2_g, ln2_b):
    bsz = x.shape[0]
    c8 = jnp.concatenate([c, c_ctx[None, :], jnp.zeros((8 - bsz - 1, D_MODEL), F32)], axis=0)
    mods = _ada_call(c8, ada_w, ada_b).reshape(DEPTH * 8 * 6, 1, D_MODEL)
    xa = (x, ctx)
    row = lambda v: v.reshape(1, -1)
    for l in range(DEPTH):
        last = l == DEPTH - 1
        p = _inproj_call(xa, mods, _pack_w_in(w_in[l]), l)
        p3 = p.reshape(bsz, SEQ_ALL, P_COLS)
        hf, hb = _hgrn_call(p3, hgrn_lb[0], hgrn_lb[1], l)
        xact = _conv_call(p3, ssd_conv_w[l], ssd_conv_b[l])
        dt_rows = jnp.swapaxes(p3[:, :, COL_DT:COL_DT + 8], 1, 2).astype(F32)
        sf, sb = _ssd_call(xact, p3, dt_rows, ssd_dt_bias[l], ssd_a_log[l])
        yc = _na_call(p3, _na_bias_tables(na_rpb[l]))
        qz, kn, va = _qkprep_call(p3, q_norm[l], k_norm[l])
        yd, yd_ctx = _gqa_call(qz, kn, va)
        if not last:
            yd = jnp.concatenate([yd_ctx, yd], axis=1)
        wb = w_branch[l].astype(BF16)
        x1 = _merge_call(l, mods, xa, p3, hf, hb, sf, sb, xact, yc, yd,
                         row(hgrn_norm[l]), row(jnp.repeat(ssd_d[l], HEAD_DIM)), row(ssd_norm[l]),
                         wb[0], wb[1], wb[2], _pad_rows_d(wb[3]), (0.5 * w_out[l]).astype(BF16),
                         row(ln1_g[l]), row(ln1_b[l]), with_ctx=not last)
        xa = _ffn_call(l, mods, x1.reshape(-1, D_MODEL), ffn_w_up[l].astype(BF16), ffn_w_down[l].astype(BF16),
                       row(ln2_g[l]), row(ln2_b[l]), with_ctx=not last).reshape(bsz, -1, D_MODEL)
    return xa
```

```python
import functools
import math

import jax
import jax.numpy as jnp
import numpy as np
from jax import lax
from jax.experimental import pallas as pl
from jax.experimental.pallas import tpu as pltpu

F32 = jnp.float32
BF16 = jnp.bfloat16
HIGHEST = lax.Precision.HIGHEST

D_MODEL = 1024
DEPTH = 2
GRID_W = 64
N_CTX = 256
N_LAT = 4096
SEQ_ALL = N_CTX + N_LAT
HEAD_DIM = 64
BRANCH_W = 256
N_HEADS = 4
SSD_STATE = 128
SSD_GROUPS = 2
SSD_CONV_K = 5
SSD_CONV_CH = BRANCH_W + 2 * SSD_GROUPS * SSD_STATE
KV_HEADS = 2
NA_WIN_R = 8
NA_WIN_C = 16
ROPE_BASE = 10000.0
LN_EPS = 1e-6
FFN_HIDDEN = 2816
ALPHA = (2.0 * DEPTH) ** 0.25

ROW_TILE = 256
N_TILES = SEQ_ALL // ROW_TILE
LAT_TILES = N_LAT // ROW_TILE
HGRN_CHUNK = 64
HGRN_SUB = 16
SSD_CHUNK = 128
NEG_BIG = -1e30
VMEM_LIMIT = 48 * 1024 * 1024

P_COLS = 8192
COL_AQ = 0
COL_Z = 1280
COL_XBC = 1536
COL_CQ = 2304
COL_DQ = 3072
COL_DK = 3584
COL_DV = 3712
COL_DT = 3840
COL_GATE = 4096


def _cparams(sem):
    return pltpu.CompilerParams(dimension_semantics=sem, vmem_limit_bytes=VMEM_LIMIT)


def _sigmoid(x):
    return 0.5 * jnp.tanh(0.5 * x) + 0.5


def _div(x, n):
    return lax.shift_right_logical(x, int(math.log2(n)))


def _mod(x, n):
    return x & (n - 1)


def _softplus(x):
    return jnp.maximum(x, 0.0) + jnp.log(1.0 + jnp.exp(-jnp.abs(x)))


def _layer_norm(x):
    mu = jnp.mean(x, axis=-1, keepdims=True)
    xc = x - mu
    var = jnp.mean(xc * xc, axis=-1, keepdims=True)
    return xc * lax.rsqrt(var + LN_EPS)


def _dot(a, b):
    return jnp.dot(a, b, preferred_element_type=F32)


def _dot_nt(a, b):
    return lax.dot_general(a, b, (((1,), (1,)), ((), ())), preferred_element_type=F32)


def _dot_hi(a, b):
    return jnp.dot(a, b, precision=HIGHEST, preferred_element_type=F32)


def _run_staggered(stages, lag=1):
    results = [None] * len(stages)
    done = [False] * len(stages)
    rnd = 0
    while not all(done):
        for g, gen in enumerate(stages):
            if rnd >= g * lag and not done[g]:
                try:
                    val = next(gen)
                    if val is not None:
                        results[g] = val
                except StopIteration:
                    done[g] = True
        rnd += 1
    return results


def _segment_mean(x, seg):
    hi = x.astype(BF16)
    lo = (x - hi.astype(F32)).astype(BF16)
    return _dot(hi, seg) + _dot(lo, seg)


def _ada_kernel(c_ref, w_ref, b_ref, o_ref):
    cv = c_ref[...]
    a = cv * _sigmoid(cv)
    o_ref[0] = _dot_hi(a, w_ref[0]) + b_ref[0]


def _ada_call(c8, ada_w, ada_b):
    n_l = ada_w.shape[0]
    return pl.pallas_call(
        _ada_kernel,
        grid=(n_l, 6),
        in_specs=[
            pl.BlockSpec((8, D_MODEL), lambda l, j: (0, 0)),
            pl.BlockSpec((1, D_MODEL, D_MODEL), lambda l, j: (l, 0, j)),
            pl.BlockSpec((1, 1, D_MODEL), lambda l, j: (l, 0, j)),
        ],
        out_specs=pl.BlockSpec((1, 8, D_MODEL), lambda l, j: (l, 0, j)),
        out_shape=jax.ShapeDtypeStruct((n_l, 8, 6 * D_MODEL), F32),
        compiler_params=_cparams(("arbitrary", "arbitrary")),
        name="ada_mod",
    )(c8, ada_w, ada_b.reshape(n_l, 1, 6 * D_MODEL))


def _mod_index(layer, which):
    def index(b, j):
        return (layer * 48 + jnp.where(j == 0, 4, b) * 6 + which, 0, 0)
    return index


def _flat_mod_spec(layer, which, sub_tiles, u, seq_tiles, first_tile):
    def index(i, *_):
        t = i * sub_tiles + u
        b = t // seq_tiles
        j = t % seq_tiles + first_tile
        return (layer * 48 + jnp.where(j == 0, 4, b) * 6 + which, 0, 0)
    return pl.BlockSpec((1, 1, D_MODEL), index)


PROJ_SUB = 2
PROJ_TN = 1024


def _split_row_specs(sub_tiles, u):
    def tile_of(i):
        t = i * sub_tiles + u
        return t // N_TILES, t % N_TILES
    lat = pl.BlockSpec((1, ROW_TILE, D_MODEL), lambda i: (tile_of(i)[0], jnp.maximum(tile_of(i)[1] - 1, 0), 0))
    ctx = pl.BlockSpec((1, ROW_TILE, D_MODEL), lambda i: (tile_of(i)[0], 0, 0))
    return [lat, ctx]


def _inproj_kernel(split, *refs):
    n_x = 2 * PROJ_SUB if split else 1
    x_refs, mod_refs, (w_ref, o_ref) = refs[:n_x], refs[n_x:n_x + 2 * PROJ_SUB], refs[n_x + 2 * PROJ_SUB:]
    def sub_tile(u):
        rows = slice(u * ROW_TILE, (u + 1) * ROW_TILE)
        if split:
            j = (pl.program_id(0) * PROJ_SUB + u) % N_TILES
            xt = jnp.where(j == 0, x_refs[2 * u + 1][0], x_refs[2 * u][0])
        else:
            xt = x_refs[0][rows, :]
        sh, sc = mod_refs[2 * u][0], mod_refs[2 * u + 1][0]
        h = (_layer_norm(xt) * (1.0 + sc) + sh).astype(BF16)
        yield
        for n in range(P_COLS // PROJ_TN):
            cols = slice(n * PROJ_TN, (n + 1) * PROJ_TN)
            acc = _dot(h, w_ref[:, cols])
            if n * PROJ_TN >= COL_GATE:
                acc = 0.5 * acc
            o_ref[rows, cols] = acc.astype(BF16)
            yield

    _run_staggered([sub_tile(u) for u in range(PROJ_SUB)])


def _inproj_call(x_src, mods, wp, layer):
    split = isinstance(x_src, tuple)
    bsz = (x_src[0] if split else x_src).shape[0]
    m = bsz * SEQ_ALL
    tm = PROJ_SUB * ROW_TILE
    mod_specs = []
    for u in range(PROJ_SUB):
        mod_specs += [_flat_mod_spec(layer, 0, PROJ_SUB, u, N_TILES, 0),
                      _flat_mod_spec(layer, 1, PROJ_SUB, u, N_TILES, 0)]
    if split:
        x_specs = [s for u in range(PROJ_SUB) for s in _split_row_specs(PROJ_SUB, u)]
        x_args = list(x_src) * PROJ_SUB
    else:
        x_specs = [pl.BlockSpec((tm, D_MODEL), lambda i: (i, 0))]
        x_args = [x_src.reshape(m, D_MODEL)]
    return pl.pallas_call(
        functools.partial(_inproj_kernel, split),
        grid=(m // tm,),
        in_specs=x_specs + mod_specs
                 + [pl.BlockSpec((D_MODEL, P_COLS), lambda i: (0, 0), pipeline_mode=pl.Buffered(1))],
        out_specs=pl.BlockSpec((tm, P_COLS), lambda i: (i, 0)),
        out_shape=jax.ShapeDtypeStruct((m, P_COLS), BF16),
        compiler_params=_cparams(("parallel",)),
        name="in_proj",
    )(*x_args, *([mods] * (2 * PROJ_SUB)), wp)


CONV_PAD = 8


def _conv_kernel(x_ref, w_ref, b_ref, o_ref, xs_ref):
    zeros = jnp.zeros((CONV_PAD, 128), F32)
    xs_ref[0:CONV_PAD, :] = zeros
    xs_ref[CONV_PAD + SEQ_ALL:, :] = zeros
    xs_ref[CONV_PAD:CONV_PAD + SEQ_ALL, :] = x_ref[0].astype(F32)
    w = w_ref[...]
    row = lax.broadcasted_iota(jnp.int32, (ROW_TILE, 128), 0)
    half = SSD_CONV_K // 2
    for ci in range(N_TILES):
        t0 = ci * ROW_TILE
        acc = jnp.broadcast_to(b_ref[...], (ROW_TILE, 128))
        for k in range(SSD_CONV_K):
            j = k - half
            xk = xs_ref[CONV_PAD + t0 + j:CONV_PAD + t0 + j + ROW_TILE, :]
            if ci == 0 and j > 0:
                xk = jnp.where(row + j < ROW_TILE, xk, 0.0)
            if ci == 1 and j < 0:
                xk = jnp.where(row + j >= 0, xk, 0.0)
            acc = acc + xk * w[k:k + 1, :]
        o_ref[0, t0:t0 + ROW_TILE, :] = (acc * _sigmoid(acc)).astype(BF16)


def _conv_call(p3, conv_w, conv_b):
    bsz = p3.shape[0]
    n_ct = SSD_CONV_CH // 128
    return pl.pallas_call(
        _conv_kernel,
        grid=(bsz, n_ct),
        in_specs=[
            pl.BlockSpec((1, SEQ_ALL, 128), lambda b, c: (b, 0, COL_XBC // 128 + c)),
            pl.BlockSpec((SSD_CONV_K, 128), lambda b, c: (0, c)),
            pl.BlockSpec((1, 128), lambda b, c: (0, c)),
        ],
        out_specs=pl.BlockSpec((1, SEQ_ALL, 128), lambda b, c: (b, 0, c)),
        out_shape=jax.ShapeDtypeStruct((bsz, SEQ_ALL, SSD_CONV_CH), BF16),
        scratch_shapes=[pltpu.VMEM((SEQ_ALL + 2 * CONV_PAD, 128), F32)],
        compiler_params=_cparams(("parallel", "parallel")),
        name="ssd_conv",
    )(p3, conv_w, conv_b.reshape(1, SSD_CONV_CH))


def _fwd_tile(j):
    return j


def _bwd_tile(j):
    return jnp.where(j == 0, 0, N_TILES - j)


def _tri(n, upper):
    r = lax.broadcasted_iota(jnp.int32, (n, n), 0)
    c = lax.broadcasted_iota(jnp.int32, (n, n), 1)
    return (c >= r) if upper else (c <= r)


def _ssd_decays(rev, dtc_ref, dtr_ref, bias_r, alog_r, bias_c, alog_c):
    n = SSD_CHUNK
    dt_c = _softplus(dtc_ref[0].astype(F32) + bias_r)
    g_c = dt_c * (-jnp.exp(alog_r) * LOG2E)
    dt_r = _softplus(dtr_ref[0] + bias_c)
    g_r = dt_r * (-jnp.exp(alog_c) * LOG2E)
    cum_c = _cumsum_rows(g_c, rev, n)
    cum_r = [_cumsum_cols(g_r[:, ci * n:(ci + 1) * n], rev) for ci in range(ROW_TILE // n)]
    return dt_c, cum_c, cum_r


def _ssd_stream(rev, g, decays, xa_ref, s_ref, o_ref):
    d = 1 if rev else 0
    n = SSD_CHUNK
    dt_c, cum_c_all, cum_r_all = decays
    keep = _tri(n, rev)
    lane = lax.broadcasted_iota(jnp.int32, (n, 128), 1)
    first = lane < HEAD_DIM
    n_chunks = ROW_TILE // n
    col0 = 4 * d + 2 * g
    end_row = 0 if rev else n - 1
    for ci in (range(n_chunks - 1, -1, -1) if rev else range(n_chunks)):
        c0 = ci * n
        cum_c = cum_c_all[c0:c0 + n, :]
        cum_r = cum_r_all[ci]
        bm = xa_ref[0, c0:c0 + n, BRANCH_W + g * SSD_STATE:BRANCH_W + (g + 1) * SSD_STATE]
        cm = xa_ref[0, c0:c0 + n, BRANCH_W + (SSD_GROUPS + g) * SSD_STATE:
                    BRANCH_W + (SSD_GROUPS + g + 1) * SSD_STATE]
        x2 = xa_ref[0, c0:c0 + n, g * 128:(g + 1) * 128].astype(F32)
        gmat = _dot_nt(cm, bm)
        s_old = s_ref[...]
        qs = _dot(cm, s_old.astype(BF16))
        yield
        bc0 = cum_c[:, col0:col0 + 1]
        bc1 = cum_c[:, col0 + 1:col0 + 2]
        br0 = cum_r[col0:col0 + 1, :]
        br1 = cum_r[col0 + 1:col0 + 2, :]
        p0 = jnp.exp2(jnp.where(keep, bc0 - br0, NEG_BIG)) * gmat
        p1 = jnp.exp2(jnp.where(keep, bc1 - br1, NEG_BIG)) * gmat
        dt2 = jnp.where(first, dt_c[c0:c0 + n, col0:col0 + 1], dt_c[c0:c0 + n, col0 + 1:col0 + 2])
        v2 = x2 * dt2
        v2b = v2.astype(BF16)
        yield
        pv = jnp.where(first, _dot(p0.astype(BF16), v2b), _dot(p1.astype(BF16), v2b))
        bc2 = jnp.where(first, bc0, bc1)
        e0 = cum_c[end_row:end_row + 1, col0:col0 + 1]
        e1 = cum_c[end_row:end_row + 1, col0 + 1:col0 + 2]
        end2 = jnp.where(first[0:1, :], e0, e1)
        wv = (v2 * jnp.exp2(end2 - bc2)).astype(BF16)
        upd = lax.dot_general(bm, wv, (((0,), (0,)), ((), ())), preferred_element_type=F32)
        yield
        o_ref[0, c0:c0 + n, g * 128:(g + 1) * 128] = pv + jnp.exp2(bc2) * qs
        s_ref[...] = s_old * jnp.exp2(end2) + upd
        yield


def _ssd_kernel(xa_f, xa_b, dtc_f, dtc_b, dtr_f, dtr_b, bias_r, alog_r, bias_c, alog_c,
                o_f, o_b, s_ref):
    @pl.when(pl.program_id(1) == 0)
    def _():
        s_ref[...] = jnp.zeros(s_ref.shape, F32)

    br, ar, bc, ac = bias_r[...], alog_r[...], bias_c[...], alog_c[...]
    dec_f = _ssd_decays(False, dtc_f, dtr_f, br, ar, bc, ac)
    dec_b = _ssd_decays(True, dtc_b, dtr_b, br, ar, bc, ac)
    _run_staggered([_ssd_stream(False, 0, dec_f, xa_f, s_ref.at[0, 0], o_f),
                    _ssd_stream(True, 0, dec_b, xa_b, s_ref.at[1, 0], o_b),
                    _ssd_stream(False, 1, dec_f, xa_f, s_ref.at[0, 1], o_f),
                    _ssd_stream(True, 1, dec_b, xa_b, s_ref.at[1, 1], o_b)])


def _ssd_call(xact, p3, dt_rows, dt_bias, a_log):
    bsz = xact.shape[0]
    bias8 = dt_bias.reshape(8).astype(F32)
    alog8 = a_log.reshape(8).astype(F32)
    pad_row = lambda v: jnp.zeros((1, 128), F32).at[0, :8].set(v)
    col = lambda v: jnp.broadcast_to(v[:, None], (8, ROW_TILE))

    def spec_x(tile):
        return pl.BlockSpec((1, ROW_TILE, SSD_CONV_CH), lambda b, j: (b, tile(j), 0))

    def spec_dtc(tile):
        return pl.BlockSpec((1, ROW_TILE, 128), lambda b, j: (b, tile(j), COL_DT // 128))

    def spec_dtr(tile):
        return pl.BlockSpec((1, 8, ROW_TILE), lambda b, j: (b, 0, tile(j)))

    def spec_o(tile):
        return pl.BlockSpec((1, ROW_TILE, BRANCH_W), lambda b, j: (b, tile(j), 0))

    small_r = pl.BlockSpec((1, 128), lambda b, j: (0, 0))
    small_c = pl.BlockSpec((8, ROW_TILE), lambda b, j: (0, 0))
    out = jax.ShapeDtypeStruct((bsz, SEQ_ALL, BRANCH_W), F32)
    return pl.pallas_call(
        _ssd_kernel,
        grid=(bsz, N_TILES),
        in_specs=[spec_x(_fwd_tile), spec_x(_bwd_tile), spec_dtc(_fwd_tile), spec_dtc(_bwd_tile),
                  spec_dtr(_fwd_tile), spec_dtr(_bwd_tile), small_r, small_r, small_c, small_c],
        out_specs=[spec_o(_fwd_tile), spec_o(_bwd_tile)],
        out_shape=[out, out],
        scratch_shapes=[pltpu.VMEM((2, SSD_GROUPS, SSD_STATE, 128), F32)],
        compiler_params=_cparams(("parallel", "arbitrary")),
        name="ssd_scan",
    )(xact, xact, p3, p3, dt_rows, dt_rows, pad_row(bias8), pad_row(alog8), col(bias8), col(alog8))


def _hgrn_lower_bound(lb_ref, layer):
    p = lb_ref[...]
    e = jnp.exp(p - jnp.max(p, axis=0, keepdims=True))
    sm = e / jnp.sum(e, axis=0, keepdims=True)
    acc = sm[0:1, :]
    for i in range(1, layer + 1):
        acc = acc + sm[i:i + 1, :]
    return acc - sm[0:1, :]


LOG2E = 1.4426950408889634


def _cumsum_rows(x, rev, block):
    n = x.shape[0]
    r_i = lax.broadcasted_iota(jnp.int32, (n, n), 0)
    c_i = lax.broadcasted_iota(jnp.int32, (n, n), 1)
    tri = ((c_i >= r_i) if rev else (c_i <= r_i)) & (_div(r_i, block) == _div(c_i, block))
    tri = tri.astype(BF16)
    hi = x.astype(BF16)
    rest = x - hi.astype(F32)
    mid = rest.astype(BF16)
    lo = (rest - mid.astype(F32)).astype(BF16)
    return _dot(tri, hi) + _dot(tri, mid) + _dot(tri, lo)


def _cumsum_cols(x, rev):
    n = x.shape[1]
    r_i = lax.broadcasted_iota(jnp.int32, (n, n), 0)
    c_i = lax.broadcasted_iota(jnp.int32, (n, n), 1)
    tri = ((r_i >= c_i) if rev else (r_i <= c_i)).astype(BF16)
    hi = x.astype(BF16)
    rest = x - hi.astype(F32)
    mid = rest.astype(BF16)
    lo = (rest - mid.astype(F32)).astype(BF16)
    return _dot(hi, tri) + _dot(mid, tri) + _dot(lo, tri)


def _hgrn_decay_logs(f, lb, rev):
    fg = lb + (1.0 - lb) * _sigmoid(f)
    cum2 = _cumsum_rows(jnp.log(fg) * LOG2E, rev, HGRN_CHUNK)
    return cum2, jnp.log(1.0 - fg) * LOG2E - cum2


def _hgrn_chunk(rev, q, v, cum2, lk2, st_ref, out_ref, ones_seg, bd_mask):
    n, c = HGRN_CHUNK, HGRN_SUB
    row = lax.broadcasted_iota(jnp.int32, (n, BRANCH_W), 0)
    lane = lax.broadcasted_iota(jnp.int32, (n, BRANCH_W), 1)
    offset = (_mod(lane, HEAD_DIM) - row) if rev else (row - _mod(lane, HEAD_DIM))
    pos = _mod(row, c)

    es = []
    for dlt in range(c):
        if dlt == 0:
            arg = cum2 + lk2
        else:
            lks = pltpu.roll(lk2, (n - dlt) if rev else dlt, axis=0)
            ok = (pos + dlt < c) if rev else (pos >= dlt)
            arg = jnp.where(ok, cum2 + lks, NEG_BIG)
        es.append((q * jnp.exp2(arg)).astype(BF16))
        if dlt % 4 == 3:
            yield
    seg = _dot(jnp.concatenate(es, axis=0), ones_seg)

    cl_parts = []
    for i in range(n // c):
        blk_rows = cum2[i * c:(i + 1) * c, :]
        if rev:
            cl_parts.append(blk_rows - cum2[(i + 1) * c:(i + 1) * c + 1, :] if i < n // c - 1 else blk_rows)
        else:
            cl_parts.append(blk_rows - cum2[i * c - 1:i * c, :] if i > 0 else blk_rows)
    qe = (q * jnp.exp2(jnp.concatenate(cl_parts, axis=0))).astype(BF16)
    parts = []
    for i in range(n // c):
        if (rev and i == n // c - 1) or (not rev and i == 0):
            parts.append(jnp.zeros((c, BRANCH_W), F32))
            continue
        if rev:
            ref = cum2[(i + 1) * c:(i + 1) * c + 1, :]
            valid = row >= (i + 1) * c
        else:
            ref = cum2[i * c - 1:i * c, :]
            valid = row < i * c
        rhs = jnp.exp2(jnp.where(valid, lk2 + ref, NEG_BIG))
        rhs_bd = jnp.where(bd_mask, jnp.concatenate([rhs] * N_HEADS, axis=0), 0.0).astype(BF16)
        parts.append(_dot_nt(qe[i * c:(i + 1) * c, :], rhs_bd))
        yield
    v_bd = jnp.where(bd_mask, jnp.concatenate([v] * N_HEADS, axis=0), 0.0).astype(BF16)
    st = st_ref[...]
    o_prev = _dot_nt((q * jnp.exp2(cum2)).astype(BF16), st.astype(BF16))
    end = cum2[0:1, :] if rev else cum2[n - 1:n, :]
    kp = jnp.exp2(lk2 + end)
    upd = _dot(v.T.astype(BF16), kp.astype(BF16))
    st_ref[...] = st * jnp.exp2(end) + jnp.where(bd_mask, upd, 0.0)
    yield
    sc = jnp.zeros((n, BRANCH_W), F32)
    for dlt in range(c):
        sc = jnp.where(offset == dlt, seg[dlt * n:(dlt + 1) * n, :], sc)
        if dlt % 8 == 7:
            yield
    sc = sc + jnp.concatenate(parts, axis=0)
    out_ref[...] = _dot(sc.astype(BF16), v_bd) + o_prev
    yield


def _hgrn_kernel(layer, q_f, f_f, v_f, q_b, f_b, v_b, lbf_ref, lbb_ref, o_f, o_b,
                 st_f, st_b, cum_f, cum_b, lk_f, lk_b):
    @pl.when(pl.program_id(1) == 0)
    def _():
        st_f[...] = jnp.zeros(st_f.shape, F32)
        st_b[...] = jnp.zeros(st_b.shape, F32)

    rb = _div(lax.broadcasted_iota(jnp.int32, (BRANCH_W, BRANCH_W), 0), HEAD_DIM)
    cb = _div(lax.broadcasted_iota(jnp.int32, (BRANCH_W, BRANCH_W), 1), HEAD_DIM)
    bd_mask = rb == cb
    ones_seg = bd_mask.astype(BF16)
    n_chunks = ROW_TILE // HGRN_CHUNK

    def stream(rev, q_ref, f_ref, v_ref, lb_ref, cum_ref, lk_ref, st_ref, o_ref):
        cum2, lk2 = _hgrn_decay_logs(f_ref[0].astype(F32), _hgrn_lower_bound(lb_ref, layer), rev)
        cum_ref[...] = cum2
        lk_ref[...] = lk2
        yield
        for i in (range(n_chunks - 1, -1, -1) if rev else range(n_chunks)):
            sl = slice(i * HGRN_CHUNK, (i + 1) * HGRN_CHUNK)
            yield from _hgrn_chunk(rev, q_ref[0, sl, :].astype(F32), v_ref[0, sl, :].astype(F32),
                                   cum_ref[sl, :], lk_ref[sl, :], st_ref, o_ref.at[0, sl, :], ones_seg, bd_mask)

    _run_staggered([stream(False, q_f, f_f, v_f, lbf_ref, cum_f, lk_f, st_f, o_f),
                    stream(True, q_b, f_b, v_b, lbb_ref, cum_b, lk_b, st_b, o_b)], lag=2)


def _hgrn_call(p3, lb_fwd, lb_bwd, layer):
    bsz = p3.shape[0]
    base = COL_AQ // BRANCH_W

    def spec(tile, col):
        return pl.BlockSpec((1, ROW_TILE, BRANCH_W), lambda b, j: (b, tile(j), col))

    lb_spec = pl.BlockSpec((DEPTH, BRANCH_W), lambda b, j: (0, 0))
    out = jax.ShapeDtypeStruct((bsz, SEQ_ALL, BRANCH_W), F32)
    return pl.pallas_call(
        functools.partial(_hgrn_kernel, layer),
        grid=(bsz, N_TILES),
        in_specs=[spec(_fwd_tile, base), spec(_fwd_tile, base + 1), spec(_fwd_tile, base + 3),
                  spec(_bwd_tile, base), spec(_bwd_tile, base + 2), spec(_bwd_tile, base + 3),
                  lb_spec, lb_spec],
        out_specs=[spec(_fwd_tile, 0), spec(_bwd_tile, 0)],
        out_shape=[out, out],
        scratch_shapes=[pltpu.VMEM((BRANCH_W, BRANCH_W), F32)] * 2 + [pltpu.VMEM((ROW_TILE, BRANCH_W), F32)] * 4,
        compiler_params=_cparams(("parallel", "arbitrary")),
        name="hgrn_scan",
    )(p3, p3, p3, p3, p3, p3, lb_fwd, lb_bwd)


def _rope(xn, cos, sin):
    w = xn.shape[1]
    lane = lax.broadcasted_iota(jnp.int32, xn.shape, 1)
    nxt = pltpu.roll(xn, w - HEAD_DIM // 4, axis=1)
    prv = pltpu.roll(xn, HEAD_DIM // 4, axis=1)
    partner = jnp.where(_mod(lane, HEAD_DIM // 2) < HEAD_DIM // 4, nxt, prv)
    return xn * cos + partner * sin


def _qkprep_kernel(q_ref, k_ref, v_ref, cq_ref, sq_ref, ck_ref, sk_ref, wq_ref, wk_ref, segk_ref,
                   qo_ref, ko_ref, vo_ref):
    def norm_rope(x_ref, w_ref, cos_ref, sin_ref, out_ref, lanes, scale):
        x = x_ref[0, :, lanes].astype(F32)
        sq = x * x
        yield
        ms = _segment_mean(sq, segk_ref[...])
        yield
        xn = x * lax.rsqrt(ms + LN_EPS) * w_ref[:, lanes]
        out_ref[0, :, lanes] = (_rope(xn, cos_ref[:, lanes], sin_ref[:, lanes]) * scale).astype(BF16)
        yield

    slots = [slice(s * 128, (s + 1) * 128) for s in range(N_HEADS)]
    _run_staggered([norm_rope(q_ref, wq_ref, cq_ref, sq_ref, qo_ref, s, HEAD_DIM ** -0.5 * LOG2E) for s in slots]
                   + [norm_rope(k_ref, wk_ref, ck_ref, sk_ref, ko_ref, slice(0, 128), 1.0)])
    v = v_ref[0].astype(F32)
    lane = lax.broadcasted_iota(jnp.int32, v.shape, 1)
    vo_ref[0] = jnp.concatenate([jnp.where(lane < HEAD_DIM, v, 1.0), jnp.where(lane < HEAD_DIM, 1.0, v)],
                                axis=1).astype(BF16)


def _rope_tables():
    t = np.arange(N_LAT)
    nf = HEAD_DIM // 4
    inv_freq = (ROPE_BASE ** (-np.arange(nf, dtype=np.float32) / nf)).astype(np.float32)
    ang_r = ((t // GRID_W).astype(np.float32)[:, None] * inv_freq).astype(np.float64)
    ang_c = ((t % GRID_W).astype(np.float32)[:, None] * inv_freq).astype(np.float64)
    ang = np.concatenate([ang_r, ang_r, ang_c, ang_c], axis=1)
    sign = np.tile(np.concatenate([-np.ones(nf), np.ones(nf)]), 2)
    cos = np.concatenate([np.ones((N_CTX, HEAD_DIM)), np.cos(ang)], axis=0).astype(np.float32)
    sin = np.concatenate([np.zeros((N_CTX, HEAD_DIM)), np.sin(ang) * sign], axis=0).astype(np.float32)
    return cos, sin


def _q_pad_lanes_np(a):
    out = np.zeros(a.shape[:-1] + (N_HEADS * 128,), a.dtype)
    for h in range(N_HEADS):
        lo = h * 128 + (h // 2) * HEAD_DIM
        out[..., lo:lo + HEAD_DIM] = a[..., h * HEAD_DIM:(h + 1) * HEAD_DIM]
    return out


def _q_pad_lanes(a):
    parts = []
    zero = jnp.zeros(a.shape[:-1] + (HEAD_DIM,), a.dtype)
    for h in range(N_HEADS):
        ah = a[..., h * HEAD_DIM:(h + 1) * HEAD_DIM]
        parts += [ah, zero] if h // 2 == 0 else [zero, ah]
    return jnp.concatenate(parts, axis=-1)


def _seg_matrix(width):
    idx = np.arange(width) // HEAD_DIM
    return jnp.asarray((idx[:, None] == idx[None, :]).astype(np.float32) / HEAD_DIM, dtype=BF16)


def _qkprep_call(p3, q_norm, k_norm):
    bsz = p3.shape[0]
    cos, sin = _rope_tables()
    cq = _q_pad_lanes_np(np.tile(cos, (1, N_HEADS)))
    sq = _q_pad_lanes_np(np.tile(sin, (1, N_HEADS)))
    ck = np.tile(cos, (1, KV_HEADS))
    sk = np.tile(sin, (1, KV_HEADS))
    wq = _q_pad_lanes(jnp.tile(q_norm, N_HEADS)[None, :])
    wk = jnp.tile(k_norm, KV_HEADS)[None, :]
    qw, kw = N_HEADS * 128, KV_HEADS * HEAD_DIM
    tile = lambda w, col: pl.BlockSpec((1, ROW_TILE, w), lambda j, b: (b, j, col))
    tab = lambda w: pl.BlockSpec((ROW_TILE, w), lambda j, b: (j, 0))
    full = lambda r, w: pl.BlockSpec((r, w), lambda j, b: (0, 0))
    return pl.pallas_call(
        _qkprep_kernel,
        grid=(N_TILES, bsz),
        in_specs=[tile(qw, COL_DQ // qw), tile(kw, COL_DK // kw), tile(kw, COL_DV // kw),
                  tab(qw), tab(qw), tab(kw), tab(kw),
                  full(1, qw), full(1, kw), full(kw, kw)],
        out_specs=[pl.BlockSpec((1, ROW_TILE, qw), lambda j, b: (b, jnp.where(j == 0, LAT_TILES, j - 1), 0)),
                   tile(kw, 0), tile(KV_HEADS * kw, 0)],
        out_shape=[jax.ShapeDtypeStruct((bsz, SEQ_ALL, qw), BF16),
                   jax.ShapeDtypeStruct((bsz, SEQ_ALL, kw), BF16),
                   jax.ShapeDtypeStruct((bsz, SEQ_ALL, KV_HEADS * kw), BF16)],
        compiler_params=_cparams(("parallel", "parallel")),
        name="gqa_qk_prep",
    )(p3, p3, p3, cq, sq, ck, sk, wq, wk, _seg_matrix(kw))


GQA_KCHUNK = 2048


def _gqa_attend(q2, k_ref, v_ref, g, chunks):
    m = acc = None
    for (k0, kn) in chunks:
        kc = k_ref[0, k0:k0 + kn, :]
        vc = v_ref[0, k0:k0 + kn, g * 128:(g + 1) * 128]
        s = _dot_nt(q2, kc)
        mc = jnp.max(s, axis=-1, keepdims=True)
        if m is None:
            m = mc
            acc = _dot(jnp.exp2(s - m).astype(BF16), vc)
        else:
            mn = jnp.maximum(m, mc)
            acc = jnp.exp2(m - mn) * acc + _dot(jnp.exp2(s - mn).astype(BF16), vc)
            m = mn
    sum_lane = (1 - g) * HEAD_DIM
    return acc / acc[:, sum_lane:sum_lane + 1]


GQA_LAT_ROWS = 2 * ROW_TILE


def _gqa_kernel(rows, chunks, q_ref, k_ref, v_ref, o_ref):
    lane = lax.broadcasted_iota(jnp.int32, (2 * rows, 128), 1)
    for g in range(KV_HEADS):
        q2 = jnp.concatenate([q_ref[0, :, (2 * g) * 128:(2 * g + 1) * 128],
                              q_ref[0, :, (2 * g + 1) * 128:(2 * g + 2) * 128]], axis=0)
        o2 = _gqa_attend(q2, k_ref, v_ref, g, chunks)
        o2 = jnp.where(_div(lane, HEAD_DIM) == g, o2, 0.0).astype(BF16)
        o_ref[0, :, (2 * g) * 128:(2 * g + 1) * 128] = o2[:rows]
        o_ref[0, :, (2 * g + 1) * 128:(2 * g + 2) * 128] = o2[rows:]


def _gqa_call(qz, kn, va):
    bsz = qz.shape[0]
    qw, kw = N_HEADS * 128, KV_HEADS * HEAD_DIM
    ctx_chunks = [(0, N_CTX)]
    all_chunks = ctx_chunks + [(N_CTX + i * GQA_KCHUNK, GQA_KCHUNK) for i in range(N_LAT // GQA_KCHUNK)]
    lat = pl.pallas_call(
        functools.partial(_gqa_kernel, GQA_LAT_ROWS, all_chunks),
        grid=(bsz, N_LAT // GQA_LAT_ROWS),
        in_specs=[
            pl.BlockSpec((1, GQA_LAT_ROWS, qw), lambda b, j: (b, j, 0)),
            pl.BlockSpec((1, SEQ_ALL, kw), lambda b, j: (b, 0, 0)),
            pl.BlockSpec((1, SEQ_ALL, KV_HEADS * kw), lambda b, j: (b, 0, 0)),
        ],
        out_specs=pl.BlockSpec((1, GQA_LAT_ROWS, qw), lambda b, j: (b, j, 0)),
        out_shape=jax.ShapeDtypeStruct((bsz, N_LAT, qw), BF16),
        compiler_params=_cparams(("parallel", "parallel")),
        name="gqa_attention",
    )(qz, kn, va)
    ctx = pl.pallas_call(
        functools.partial(_gqa_kernel, N_CTX, ctx_chunks),
        grid=(bsz,),
        in_specs=[
            pl.BlockSpec((1, N_CTX, qw), lambda b: (b, N_LAT // N_CTX, 0)),
            pl.BlockSpec((1, N_CTX, kw), lambda b: (b, 0, 0)),
            pl.BlockSpec((1, N_CTX, KV_HEADS * kw), lambda b: (b, 0, 0)),
        ],
        out_specs=pl.BlockSpec((1, N_CTX, qw), lambda b: (b, 0, 0)),
        out_shape=jax.ShapeDtypeStruct((bsz, N_CTX, qw), BF16),
        compiler_params=_cparams(("parallel",)),
        name="gqa_attention_ctx",
    )(qz, kn, va)
    return lat, ctx


NA_ROWS = ROW_TILE // GRID_W
NA_KTILES = 3
NA_KW = NA_KTILES * ROW_TILE


def _na_bias_tables(rpb):
    rows = N_LAT // GRID_W
    cq = np.arange(GRID_W)
    kc = np.arange(GRID_W)
    start_c = np.clip(cq - NA_WIN_C // 2, 0, GRID_W - NA_WIN_C)
    ok_c = (kc[None, :] >= start_c[:, None]) & (kc[None, :] < start_c[:, None] + NA_WIN_C)
    dcol = np.clip(kc[None, :] - cq[:, None] + NA_WIN_C - 1, 0, 2 * NA_WIN_C - 2)
    pick_c = jnp.asarray(dcol[:, :, None] == np.arange(2 * NA_WIN_C - 1), dtype=BF16)
    pick_r, ok_r = [], []
    for jb in (0, 1, LAT_TILES - 1):
        base = min(max(jb - 1, 0), LAT_TILES - NA_KTILES)
        r = jb * NA_ROWS + np.arange(NA_ROWS)
        kr = base * NA_ROWS + np.arange(NA_KTILES * NA_ROWS)
        start_r = np.clip(r - NA_WIN_R // 2, 0, rows - NA_WIN_R)
        ok_r.append((kr[None, :] >= start_r[:, None]) & (kr[None, :] < start_r[:, None] + NA_WIN_R))
        drow = np.clip(kr[None, :] - r[:, None] + NA_WIN_R - 1, 0, 2 * NA_WIN_R - 2)
        pick_r.append((drow[:, :, None] == np.arange(2 * NA_WIN_R - 1)).astype(np.float32))
    pick_r, ok_r = np.stack(pick_r), np.stack(ok_r)
    rows_sel = (jnp.einsum('vamr,hrd->vhamd', pick_r, rpb, precision=HIGHEST) * LOG2E).astype(BF16)
    bias = jnp.einsum('vhamd,ckd->vhacmk', rows_sel, pick_c, preferred_element_type=BF16)
    ok = ok_r[:, None, :, None, :, None] & ok_c[None, None, None, :, None, :]
    bias = bias + jnp.asarray(np.where(ok, 0.0, NEG_BIG), dtype=BF16)
    return bias.reshape(3, N_HEADS, ROW_TILE, NA_KW)


def _na_kernel(q_ref, k0_ref, k1_ref, k2_ref, kc_ref, v0_ref, v1_ref, v2_ref, vc_ref, bias_ref, o_ref):
    lane = lax.broadcasted_iota(jnp.int32, (ROW_TILE, BRANCH_W), 1)
    blk = _div(lane, HEAD_DIM)
    q = q_ref[0].astype(F32) * (HEAD_DIM ** -0.5 * LOG2E)

    def attend(key_refs, val_refs, bias_of):
        def head(h):
            qz = jnp.where(blk == h, q, 0.0).astype(BF16)
            s = [_dot_nt(qz, kr[0]) for kr in key_refs]
            yield
            s = [si if bias_of(h, i) is None else si + bias_of(h, i) for i, si in enumerate(s)]
            yield
            m = functools.reduce(jnp.maximum, [jnp.max(si, axis=-1, keepdims=True) for si in s])
            yield
            p = [jnp.exp2(si - m) for si in s]
            yield
            l = functools.reduce(jnp.add, [jnp.sum(pi, axis=-1, keepdims=True) for pi in p])
            yield
            o = functools.reduce(jnp.add, [_dot(pi.astype(BF16), vr[0]) for pi, vr in zip(p, val_refs)])
            yield o / l

        outs = _run_staggered([head(h) for h in range(N_HEADS)])
        acc = jnp.zeros((ROW_TILE, BRANCH_W), F32)
        for h in range(N_HEADS):
            acc = jnp.where(blk == h, outs[h], acc)
        o_ref[0] = acc.astype(BF16)

    @pl.when(pl.program_id(1) == 0)
    def _():
        attend([kc_ref], [vc_ref], lambda h, i: None)

    @pl.when(pl.program_id(1) > 0)
    def _():
        def bias_of(h, i):
            if i == NA_KTILES:
                return None
            return bias_ref[0, h, :, i * ROW_TILE:(i + 1) * ROW_TILE].astype(F32)
        attend([k0_ref, k1_ref, k2_ref, kc_ref], [v0_ref, v1_ref, v2_ref, vc_ref], bias_of)


def _na_call(p3, bias_tabs):
    bsz = p3.shape[0]
    cq = COL_CQ // BRANCH_W

    def key_tile(j, m):
        return 1 + jnp.clip(j - 2, 0, LAT_TILES - NA_KTILES) + m

    def spec_k(col, m):
        return pl.BlockSpec((1, ROW_TILE, BRANCH_W), lambda b, j: (b, key_tile(j, m), col))

    def spec_c(col):
        return pl.BlockSpec((1, ROW_TILE, BRANCH_W), lambda b, j: (b, 0, col))

    def variant(j):
        return jnp.where(j <= 1, 0, jnp.where(j == LAT_TILES, 2, 1))

    return pl.pallas_call(
        _na_kernel,
        grid=(bsz, N_TILES),
        in_specs=[pl.BlockSpec((1, ROW_TILE, BRANCH_W), lambda b, j: (b, j, cq)),
                  spec_k(cq + 1, 0), spec_k(cq + 1, 1), spec_k(cq + 1, 2), spec_c(cq + 1),
                  spec_k(cq + 2, 0), spec_k(cq + 2, 1), spec_k(cq + 2, 2), spec_c(cq + 2),
                  pl.BlockSpec((1, N_HEADS, ROW_TILE, NA_KW), lambda b, j: (variant(j), 0, 0, 0))],
        out_specs=pl.BlockSpec((1, ROW_TILE, BRANCH_W), lambda b, j: (b, j, 0)),
        out_shape=jax.ShapeDtypeStruct((bsz, SEQ_ALL, BRANCH_W), BF16),
        compiler_params=_cparams(("parallel", "parallel")),
        name="na_attention",
    )(p3, p3, p3, p3, p3, p3, p3, p3, p3, bias_tabs)


def _merge_kernel(split, hf_ref, hb_ref, ag_ref, sf_ref, sb_ref, xs_ref, z_ref, yc_ref, yd_ref,
                  g0_ref, g1_ref, g2_ref, g3_ref, *rest):
    if split:
        x_ref, xc_ref, *rest = rest
    else:
        x_ref, *rest = rest
    gate_ref, hn_w, dsk, sn_w, seg_ref, wa, wb, wc, wd, wo, ln_g, ln_b, o_ref = rest
    if split:
        x_res = jnp.where(pl.program_id(1) == 0, xc_ref[0], x_ref[0])
    else:
        x_res = x_ref[0]
    oa = hf_ref[0] + hb_ref[0]
    ag = ag_ref[0].astype(F32)
    oa = oa * lax.rsqrt(_segment_mean(oa * oa, seg_ref[...]) + LN_EPS) * hn_w[...]
    br_a = oa * (ag * _sigmoid(ag))
    z = z_ref[0].astype(F32)
    yb = (sf_ref[0] + sb_ref[0] + xs_ref[0].astype(F32) * dsk[...]) * (z * _sigmoid(z))
    br_b = yb * lax.rsqrt(jnp.mean(yb * yb, axis=-1, keepdims=True) + LN_EPS) * sn_w[...]
    merged = (1.0 + jnp.tanh(g0_ref[0].astype(F32))) * _dot(br_a.astype(BF16), wa[...])
    merged += (1.0 + jnp.tanh(g1_ref[0].astype(F32))) * _dot(br_b.astype(BF16), wb[...])
    merged += (1.0 + jnp.tanh(g2_ref[0].astype(F32))) * _dot(yc_ref[0], wc[...])
    merged += (1.0 + jnp.tanh(g3_ref[0].astype(F32))) * _dot(yd_ref[0], wd[...])
    y = _dot(merged.astype(BF16), wo[...])
    o_ref[0] = _layer_norm(ALPHA * x_res + gate_ref[0] * y) * ln_g[...] + ln_b[...]


def _merge_call(layer, mods, x_src, p3, hf, hb, sf, sb, xact, yc, yd, hn_w, dsk, sn_w, wa, wb, wc, wd, wo,
                ln_g, ln_b, with_ctx):
    split = isinstance(x_src, tuple)
    assert with_ctx or not split
    bsz = p3.shape[0]
    t0 = 0 if with_ctx else 1
    tiles = N_TILES - t0
    t256 = lambda col: pl.BlockSpec((1, ROW_TILE, BRANCH_W), lambda b, j: (b, j + t0, col))
    t512 = pl.BlockSpec((1, ROW_TILE, 512), lambda b, j: (b, j, 0))
    tgate = lambda n: pl.BlockSpec((1, ROW_TILE, D_MODEL), lambda b, j: (b, j + t0, COL_GATE // D_MODEL + n))
    tx = pl.BlockSpec((1, ROW_TILE, D_MODEL), lambda b, j: (b, j + t0, 0))
    gate = pl.BlockSpec((1, 1, D_MODEL), lambda b, j: _mod_index(layer, 2)(b, j + t0))
    full = lambda a: pl.BlockSpec(a.shape, lambda b, j: (0,) * a.ndim)
    seg = _seg_matrix(BRANCH_W)
    consts = [hn_w, dsk, sn_w, seg, wa, wb, wc, wd, wo, ln_g, ln_b]
    if split:
        x_specs = [pl.BlockSpec((1, ROW_TILE, D_MODEL), lambda b, j: (b, jnp.maximum(j - 1, 0), 0)),
                   pl.BlockSpec((1, ROW_TILE, D_MODEL), lambda b, j: (b, 0, 0))]
        x_args = list(x_src)
    else:
        x_specs, x_args = [tx], [x_src]
    return pl.pallas_call(
        functools.partial(_merge_kernel, split),
        grid=(bsz, tiles),
        in_specs=[t256(0), t256(0), t256(COL_AQ // BRANCH_W + 4), t256(0), t256(0), t256(0),
                  t256(COL_Z // BRANCH_W), t256(0), t512,
                  tgate(0), tgate(1), tgate(2), tgate(3)] + x_specs + [gate]
                 + [full(a) for a in consts],
        out_specs=pl.BlockSpec((1, ROW_TILE, D_MODEL), lambda b, j: (b, j, 0)),
        out_shape=jax.ShapeDtypeStruct((bsz, tiles * ROW_TILE, D_MODEL), F32),
        compiler_params=_cparams(("parallel", "parallel")),
        name="merge_postnorm",
    )(hf, hb, p3, sf, sb, xact, p3, yc, yd, p3, p3, p3, p3, *x_args, mods, *consts)


FFN_CHUNK = 256


FFN_SUB = 2


def _ffn_kernel(x_ref, *refs):
    mod_refs, (wu_ref, wd_ref, ln_g, ln_b, o_ref) = refs[:3 * FFN_SUB], refs[3 * FFN_SUB:]
    tm = FFN_SUB * ROW_TILE
    x = x_ref[...]
    ln = _layer_norm(x)
    h = jnp.concatenate(
        [ln[u * ROW_TILE:(u + 1) * ROW_TILE] * (1.0 + mod_refs[3 * u + 1][0]) + mod_refs[3 * u][0]
         for u in range(FFN_SUB)], axis=0).astype(BF16)
    acc = jnp.zeros((tm, D_MODEL), F32)
    for c in range(FFN_HIDDEN // FFN_CHUNK):
        c0 = c * FFN_CHUNK
        g = _dot(h, wu_ref[:, c0:c0 + FFN_CHUNK])
        u = _dot(h, wu_ref[:, FFN_HIDDEN + c0:FFN_HIDDEN + c0 + FFN_CHUNK])
        a = (g * _sigmoid(g) * u).astype(BF16)
        acc = acc + _dot(a, wd_ref[c0:c0 + FFN_CHUNK, :])
    for u in range(FFN_SUB):
        rows = slice(u * ROW_TILE, (u + 1) * ROW_TILE)
        y = ALPHA * x[rows] + mod_refs[3 * u + 2][0] * acc[rows]
        o_ref[rows, :] = _layer_norm(y) * ln_g[...] + ln_b[...]


def _ffn_call(layer, mods, x1_flat, w_up, w_down, ln_g, ln_b, with_ctx):
    m = x1_flat.shape[0]
    tm = FFN_SUB * ROW_TILE
    seq_tiles, first = (N_TILES, 0) if with_ctx else (LAT_TILES, 1)
    mod_specs = [_flat_mod_spec(layer, 3 + which, FFN_SUB, u, seq_tiles, first)
                 for u in range(FFN_SUB) for which in range(3)]
    tx = pl.BlockSpec((tm, D_MODEL), lambda i: (i, 0))
    resident = lambda a: pl.BlockSpec(a.shape, lambda i: (0,) * a.ndim, pipeline_mode=pl.Buffered(1))
    full = lambda a: pl.BlockSpec(a.shape, lambda i: (0,) * a.ndim)
    return pl.pallas_call(
        _ffn_kernel,
        grid=(m // tm,),
        in_specs=[tx] + mod_specs + [resident(w_up), resident(w_down), full(ln_g), full(ln_b)],
        out_specs=tx,
        out_shape=jax.ShapeDtypeStruct(x1_flat.shape, F32),
        compiler_params=_cparams(("parallel",)),
        name="ffn_postnorm",
    )(x1_flat, *([mods] * (3 * FFN_SUB)), w_up, w_down, ln_g, ln_b)


def _pack_w_in(w):
    n_dt = 2 * N_HEADS
    o_dt = 5 * BRANCH_W + BRANCH_W + SSD_CONV_CH
    o_c = o_dt + n_dt
    o_dq = o_c + 3 * BRANCH_W
    o_dk = o_dq + BRANCH_W
    o_gate = o_dk + 2 * KV_HEADS * HEAD_DIM
    zeros = jnp.zeros((w.shape[0], COL_GATE - COL_DT - n_dt), w.dtype)
    cols = [w[:, :o_dt], w[:, o_c:o_dq], _q_pad_lanes(w[:, o_dq:o_dk]), w[:, o_dk:o_gate],
            w[:, o_dt:o_c], zeros, w[:, o_gate:]]
    packed = jnp.concatenate(cols, axis=1)
    assert packed.shape[1] == P_COLS
    return packed.astype(BF16)


def _pad_rows_d(wd):
    return _q_pad_lanes(wd.T).T


def kernel(x, c, ctx, c_ctx, ada_w, ada_b, w_in, hgrn_lb, hgrn_norm, ssd_conv_w, ssd_conv_b, ssd_dt_bias,
           ssd_a_log, ssd_d, ssd_norm, na_rpb, q_norm, k_norm, w_branch, w_out, ln1_g, ln1_b,
           ffn_w_up, ffn_w_down, ln2_g, ln2_b):
    bsz = x.shape[0]
    c8 = jnp.concatenate([c, c_ctx[None, :], jnp.zeros((8 - bsz - 1, D_MODEL), F32)], axis=0)
    mods = _ada_call(c8, ada_w, ada_b).reshape(DEPTH * 8 * 6, 1, D_MODEL)
    xa = (x, ctx)
    row = lambda v: v.reshape(1, -1)
    for l in range(DEPTH):
        last = l == DEPTH - 1
        p = _inproj_call(xa, mods, _pack_w_in(w_in[l]), l)
        p3 = p.reshape(bsz, SEQ_ALL, P_COLS)
        hf, hb = _hgrn_call(p3, hgrn_lb[0], hgrn_lb[1], l)
        xact = _conv_call(p3, ssd_conv_w[l], ssd_conv_b[l])
        dt_rows = jnp.swapaxes(p3[:, :, COL_DT:COL_DT + 8], 1, 2).astype(F32)
        sf, sb = _ssd_call(xact, p3, dt_rows, ssd_dt_bias[l], ssd_a_log[l])
        yc = _na_call(p3, _na_bias_tables(na_rpb[l]))
        qz, kn, va = _qkprep_call(p3, q_norm[l], k_norm[l])
        yd, yd_ctx = _gqa_call(qz, kn, va)
        if not last:
            yd = jnp.concatenate([yd_ctx, yd], axis=1)
        wb = w_branch[l].astype(BF16)
        x1 = _merge_call(l, mods, xa, p3, hf, hb, sf, sb, xact, yc, yd,
                         row(hgrn_norm[l]), row(jnp.repeat(ssd_d[l], HEAD_DIM)), row(ssd_norm[l]),
                         wb[0], wb[1], wb[2], _pad_rows_d(wb[3]), (0.5 * w_out[l]).astype(BF16),
                         row(ln1_g[l]), row(ln1_b[l]), with_ctx=not last)
        xa = _ffn_call(l, mods, x1.reshape(-1, D_MODEL), ffn_w_up[l].astype(BF16), ffn_w_down[l].astype(BF16),
                       row(ln2_g[l]), row(ln2_b[l]), with_ctx=not last).reshape(bsz, -1, D_MODEL)
    return xa
```
